```python
import jax, jax.numpy as jnp
from jax import lax
import numpy as np

D_MODEL = 1024
BATCH = 2
SEQ = 8192
DEPTH = 4

N_MIXERS = 2
N_CONV_LAYERS = (DEPTH + N_MIXERS - 1) // N_MIXERS
N_NSA_LAYERS = DEPTH // N_MIXERS
CONV_WIDTH = 31
N_HEADS = 16
HEAD_DIM = 64
N_KV = 4
GROUP = N_HEADS // N_KV
ROT_DIM = HEAD_DIM // 4
ROPE_THETA = 500000.0
CMP_LEN = 32
CMP_STRIDE = 16
CMP_HIDDEN = 256
SEL_LEN = 64
SEL_TOPK = 16
WINDOW = 512
Q_BLOCK = 128
D_FF = 2816
FFN_CONV_WIDTH = 3
EPS = 1e-6
PROJ_SIZES = (N_HEADS * HEAD_DIM,) + (N_KV * HEAD_DIM,) * 6 + (3 * N_HEADS,)
PROJ_COLS = sum(PROJ_SIZES)

kernel_name = "hybrid_conformer_nsa_convffn"


def rmsnorm(x, g):
    xf = x.astype(jnp.float32)
    y = xf * lax.rsqrt(jnp.mean(xf * xf, axis=-1, keepdims=True) + EPS)
    return (y * g.astype(jnp.float32)).astype(x.dtype)


def layernorm(x, g, b):
    xf = x.astype(jnp.float32)
    mu = jnp.mean(xf, axis=-1, keepdims=True)
    var = jnp.mean(jnp.square(xf - mu), axis=-1, keepdims=True)
    y = (xf - mu) * lax.rsqrt(var + EPS)
    return (y * g.astype(jnp.float32) + b.astype(jnp.float32)).astype(x.dtype)


def causal_dwconv(x, w, b):
    width, ch = w.shape
    y = lax.conv_general_dilated(
        x, w[:, None, :].astype(x.dtype), window_strides=(1,),
        padding=[(width - 1, 0)], dimension_numbers=("NWC", "WIO", "NWC"),
        feature_group_count=ch)
    return y + b.astype(x.dtype)


def partial_rope(x):
    seq = x.shape[1]
    half = ROT_DIM // 2
    inv_freq = ROPE_THETA ** (-jnp.arange(half, dtype=jnp.float32) * (2.0 / ROT_DIM))
    ang = jnp.arange(seq, dtype=jnp.float32)[:, None] * inv_freq[None, :]
    cos = jnp.cos(ang)[:, None, :]
    sin = jnp.sin(ang)[:, None, :]
    xf = x.astype(jnp.float32)
    x1 = xf[..., :half]
    x2 = xf[..., half:ROT_DIM]
    out = jnp.concatenate([x1 * cos - x2 * sin, x2 * cos + x1 * sin, xf[..., ROT_DIM:]], axis=-1)
    return out.astype(x.dtype)


def masked_softmax(s, mask):
    s = jnp.where(mask, s.astype(jnp.float32), -1e30)
    m = jnp.max(s, axis=-1, keepdims=True)
    e = jnp.where(mask, jnp.exp(s - m), 0.0)
    return e / jnp.maximum(jnp.sum(e, axis=-1, keepdims=True), 1e-30)


def cmp_to_sel_matrix(n_cmp, n_sel):
    start_c = np.arange(n_cmp)[:, None] * CMP_STRIDE
    start_s = np.arange(n_sel)[None, :] * SEL_LEN
    ov = np.minimum(start_c + CMP_LEN, start_s + SEL_LEN) - np.maximum(start_c, start_s)
    return jnp.asarray(np.maximum(ov, 0).astype(np.float32) / CMP_LEN)


def conformer_conv(h, w_pw1, b_pw1, w_dw, b_dw, ln_g, ln_b, w_pw2, b_pw2):
    u = h @ w_pw1 + b_pw1
    a, gate = jnp.split(u, 2, axis=-1)
    u = a * jax.nn.sigmoid(gate)
    u = causal_dwconv(u, w_dw, b_dw)
    u = jax.nn.silu(layernorm(u, ln_g, ln_b))
    return u @ w_pw2 + b_pw2


def compress(blocks, pe, w1, w2):
    b, g, n, l, d = blocks.shape
    z = (blocks + pe).reshape(b, g, n, l * d)
    return jax.nn.silu(z @ w1) @ w2


def nsa_attention(h, w_in, q_g, kc_g, ks_g, kw_g, pe_k, pe_v, ck_w1, ck_w2, cv_w1, cv_w2, w_out):
    bsz, seq, _ = h.shape
    n_cmp = (seq - CMP_LEN) // CMP_STRIDE + 1
    n_sel = seq // SEL_LEN
    k_top = min(SEL_TOPK, n_sel)
    n_qb = seq // Q_BLOCK
    scale = HEAD_DIM ** -0.5

    splits = [int(c) for c in np.cumsum(PROJ_SIZES)[:-1]]
    q, kc, vc, ks, vs, kw, vw, gl = jnp.split(h @ w_in, splits, axis=-1)
    heads = lambda t, n: t.reshape(bsz, seq, n, HEAD_DIM)
    to_groups = lambda t: t.reshape(bsz, seq, N_KV, GROUP, HEAD_DIM).transpose(0, 2, 3, 1, 4)
    q = rmsnorm(heads(q, N_HEADS), q_g)
    q_nope = to_groups(q)
    q_rope = to_groups(partial_rope(q))
    gates = jax.nn.sigmoid(gl.astype(jnp.float32)).astype(h.dtype)
    gates = gates.reshape(bsz, seq, N_KV, GROUP, 3).transpose(0, 2, 3, 1, 4)

    blk_idx = np.arange(n_cmp)[:, None] * CMP_STRIDE + np.arange(CMP_LEN)[None, :]
    kc_blk = heads(kc, N_KV)[:, blk_idx].transpose(0, 3, 1, 2, 4)
    vc_blk = heads(vc, N_KV)[:, blk_idx].transpose(0, 3, 1, 2, 4)
    k_cmp = rmsnorm(compress(kc_blk, pe_k, ck_w1, ck_w2), kc_g)
    v_cmp = compress(vc_blk, pe_v, cv_w1, cv_w2)
    cmp_end = jnp.asarray(np.arange(n_cmp) * CMP_STRIDE + CMP_LEN - 1)
    cmp_map = cmp_to_sel_matrix(n_cmp, n_sel)

    k_sel = partial_rope(rmsnorm(heads(ks, N_KV), ks_g)).transpose(0, 2, 1, 3)
    k_sel = k_sel.reshape(bsz, N_KV, n_sel, SEL_LEN, HEAD_DIM)
    v_sel = heads(vs, N_KV).transpose(0, 2, 1, 3).reshape(bsz, N_KV, n_sel, SEL_LEN, HEAD_DIM)

    pad = ((0, 0), (0, 0), (WINDOW, 0), (0, 0))
    k_win = jnp.pad(partial_rope(rmsnorm(heads(kw, N_KV), kw_g)).transpose(0, 2, 1, 3), pad)
    v_win = jnp.pad(heads(vw, N_KV).transpose(0, 2, 1, 3), pad)

    b_ix = jnp.arange(bsz)[:, None, None, None]
    g_ix = jnp.arange(N_KV)[None, :, None, None]
    sel_off = jnp.arange(SEL_LEN)
    win_off = jnp.arange(Q_BLOCK + WINDOW)
    sel_ids = jnp.arange(n_sel)

    def query_block(qb):
        t0 = qb * Q_BLOCK
        pos = t0 + jnp.arange(Q_BLOCK)
        qn = lax.dynamic_slice_in_dim(q_nope, t0, Q_BLOCK, axis=3)
        qr = lax.dynamic_slice_in_dim(q_rope, t0, Q_BLOCK, axis=3)
        gt = lax.dynamic_slice_in_dim(gates, t0, Q_BLOCK, axis=3)

        s_c = jnp.einsum("bgrtd,bgnd->bgrtn", qn, k_cmp) * scale
        p_c = masked_softmax(s_c, cmp_end[None, :] <= pos[:, None])
        o_c = jnp.einsum("bgrtn,bgnd->bgrtd", p_c.astype(v_cmp.dtype), v_cmp)

        cur = pos // SEL_LEN
        imp = jnp.einsum("bgrtn,nj->bgtj", p_c, cmp_map)
        forced = (sel_ids[None, :] == 0) | (sel_ids[None, :] == cur[:, None]) | (sel_ids[None, :] == cur[:, None] - 1)
        imp = jnp.where(sel_ids[None, :] > cur[:, None], -1.0, jnp.where(forced, 1e4, imp))
        _, idx = lax.top_k(imp, k_top)
        k_g = k_sel[b_ix, g_ix, idx]
        v_g = v_sel[b_ix, g_ix, idx]
        s_s = jnp.einsum("bgrtd,bgtkld->bgrtkl", qr, k_g) * scale
        key_pos = idx[..., None] * SEL_LEN + sel_off
        mask_s = (key_pos <= pos[:, None, None]).reshape(bsz, N_KV, 1, Q_BLOCK, k_top * SEL_LEN)
        p_s = masked_softmax(s_s.reshape(bsz, N_KV, GROUP, Q_BLOCK, k_top * SEL_LEN), mask_s)
        o_s = jnp.einsum("bgrtm,bgtmd->bgrtd", p_s.astype(v_g.dtype),
                         v_g.reshape(bsz, N_KV, Q_BLOCK, k_top * SEL_LEN, HEAD_DIM))

        kwb = lax.dynamic_slice_in_dim(k_win, t0, Q_BLOCK + WINDOW, axis=2)
        vwb = lax.dynamic_slice_in_dim(v_win, t0, Q_BLOCK + WINDOW, axis=2)
        kp = t0 - WINDOW + win_off
        mask_w = (kp[None, :] >= 0) & (kp[None, :] <= pos[:, None]) & (kp[None, :] > pos[:, None] - WINDOW)
        s_w = jnp.einsum("bgrtd,bgsd->bgrts", qr, kwb) * scale
        p_w = masked_softmax(s_w, mask_w)
        o_w = jnp.einsum("bgrts,bgsd->bgrtd", p_w.astype(vwb.dtype), vwb)

        return gt[..., 0:1] * o_c + gt[..., 1:2] * o_s + gt[..., 2:3] * o_w

    o = lax.map(query_block, jnp.arange(n_qb))
    o = o.transpose(1, 0, 4, 2, 3, 5).reshape(bsz, seq, N_HEADS * HEAD_DIM)
    return o @ w_out


def conv_ffn(h, w_up, w_dw, b_dw, w_down):
    a, v = jnp.split(h @ w_up, 2, axis=-1)
    return (jax.nn.silu(causal_dwconv(a, w_dw, b_dw)) * v) @ w_down


def setup_inputs(seed: int = 0) -> dict:
    key = jax.random.key(seed)
    ks = iter(jax.random.split(key, 40))
    f32 = jnp.float32
    D = D_MODEL
    hd = HEAD_DIM
    Lc = N_CONV_LAYERS
    Ln = N_NSA_LAYERS

    def w(shape, fan_in):
        return jax.random.normal(next(ks), shape, f32) * fan_in ** -0.5

    def gain(shape):
        return 1.0 + 0.05 * jax.random.normal(next(ks), shape, f32)

    def small(shape, s=0.02):
        return s * jax.random.normal(next(ks), shape, f32)

    return {
        "x": jax.random.normal(next(ks), (BATCH, SEQ, D), f32),
        "mix_norm_g": gain((DEPTH, D)),
        "ffn_norm_g": gain((DEPTH, D)),
        "conv_w_pw1": w((Lc, D, 2 * D), D),
        "conv_b_pw1": small((Lc, 2 * D)),
        "conv_w_dw": w((Lc, CONV_WIDTH, D), CONV_WIDTH),
        "conv_b_dw": small((Lc, D)),
        "conv_ln_g": gain((Lc, D)),
        "conv_ln_b": small((Lc, D)),
        "conv_w_pw2": w((Lc, D, D), D),
        "conv_b_pw2": small((Lc, D)),
        "nsa_w_in": w((Ln, D, PROJ_COLS), D),
        "nsa_q_norm": gain((Ln, hd)),
        "nsa_kc_norm": gain((Ln, hd)),
        "nsa_ks_norm": gain((Ln, hd)),
        "nsa_kw_norm": gain((Ln, hd)),
        "nsa_pe_k": small((Ln, CMP_LEN, hd), 0.1),
        "nsa_pe_v": small((Ln, CMP_LEN, hd), 0.1),
        "nsa_ck_w1": w((Ln, CMP_LEN * hd, CMP_HIDDEN), CMP_LEN * hd),
        "nsa_ck_w2": w((Ln, CMP_HIDDEN, hd), CMP_HIDDEN),
        "nsa_cv_w1": w((Ln, CMP_LEN * hd, CMP_HIDDEN), CMP_LEN * hd),
        "nsa_cv_w2": w((Ln, CMP_HIDDEN, hd), CMP_HIDDEN),
        "nsa_w_out": w((Ln, N_HEADS * hd, D), N_HEADS * hd),
        "ffn_w_up": w((DEPTH, D, 2 * D_FF), D),
        "ffn_w_dw": w((DEPTH, FFN_CONV_WIDTH, D_FF), FFN_CONV_WIDTH),
        "ffn_b_dw": small((DEPTH, D_FF)),
        "ffn_w_down": w((DEPTH, D_FF, D), D_FF),
    }


def reference(x, mix_norm_g, ffn_norm_g,
              conv_w_pw1, conv_b_pw1, conv_w_dw, conv_b_dw, conv_ln_g, conv_ln_b, conv_w_pw2, conv_b_pw2,
              nsa_w_in, nsa_q_norm, nsa_kc_norm, nsa_ks_norm, nsa_kw_norm, nsa_pe_k, nsa_pe_v,
              nsa_ck_w1, nsa_ck_w2, nsa_cv_w1, nsa_cv_w2, nsa_w_out,
              ffn_w_up, ffn_w_dw, ffn_b_dw, ffn_w_down):
    for i in range(DEPTH):
        h = rmsnorm(x, mix_norm_g[i])
        j = i // N_MIXERS
        if i % N_MIXERS == 0:
            x = x + conformer_conv(h, conv_w_pw1[j], conv_b_pw1[j], conv_w_dw[j], conv_b_dw[j],
                                   conv_ln_g[j], conv_ln_b[j], conv_w_pw2[j], conv_b_pw2[j])
        else:
            x = x + nsa_attention(h, nsa_w_in[j], nsa_q_norm[j], nsa_kc_norm[j], nsa_ks_norm[j], nsa_kw_norm[j],
                                  nsa_pe_k[j], nsa_pe_v[j], nsa_ck_w1[j], nsa_ck_w2[j], nsa_cv_w1[j], nsa_cv_w2[j],
                                  nsa_w_out[j])
        x = x + conv_ffn(rmsnorm(x, ffn_norm_g[i]), ffn_w_up[i], ffn_w_dw[i], ffn_b_dw[i], ffn_w_down[i])
    return x
```

```python
import functools

import numpy as np
import jax
import jax.numpy as jnp
from jax import lax
from jax.experimental import pallas as pl
from jax.experimental.pallas import tpu as pltpu

N_HEADS = 16
HEAD_DIM = 64
N_KV = 4
GROUP = N_HEADS // N_KV
ROT_DIM = HEAD_DIM // 4
ROPE_THETA = 500000.0
CMP_LEN = 32
CMP_STRIDE = 16
SEL_LEN = 64
SEL_TOPK = 16
WINDOW = 512
EPS = 1e-6
NEG = -1e30

LANES = 128
BF16_ROWS = 16
VMEM_LIMIT = 48 * 1024 * 1024

F32 = jnp.float32
BF16 = jnp.bfloat16
NT_DIMS = (((1,), (1,)), ((), ()))


def _params(*sem):
    return pltpu.CompilerParams(dimension_semantics=sem, vmem_limit_bytes=VMEM_LIMIT)


def _dot(a, b):
    return jnp.dot(a, b, preferred_element_type=F32)


def _dot_nt(a, b):
    return lax.dot_general(a, b, NT_DIMS, preferred_element_type=F32)


def _split_bf16(x):
    hi = x.astype(BF16)
    lo = (x - hi.astype(F32)).astype(BF16)
    return hi, lo


def _rms_rows(x, g):
    ms = jnp.mean(x * x, axis=-1, keepdims=True)
    return x * lax.rsqrt(ms + EPS) * g


def _sigmoid(x):
    return 1.0 / (1.0 + jnp.exp(-x))


def _head_rms(xb, bd, gain):
    hi, lo = _split_bf16(xb * xb)
    ss = _dot(hi, bd) + _dot(lo, bd)
    return xb * lax.rsqrt(ss * (1.0 / HEAD_DIM) + EPS) * gain


def _rope(xb, c, s1, s2):
    half = ROT_DIM // 2
    return xb * c + pltpu.roll(xb, LANES - half, 1) * s1 + pltpu.roll(xb, half, 1) * s2


def _ffn_kernel(x_ref, halo_ref, g_ref, wa_ref, wv_ref, wdw_ref, bdw_ref, wd_ref, o_ref,
                hn_ref, a_ref, acc_ref, *, tm, tiles_per_seq):
    i = pl.program_id(0)
    c = pl.program_id(1)
    hl = BF16_ROWS

    @pl.when(c == 0)
    def _():
        g = g_ref[...]
        hn_ref[hl:, :] = _rms_rows(x_ref[...], g).astype(BF16)
        first = (i % tiles_per_seq) == 0
        hn_ref[0:hl, :] = jnp.where(first, 0.0, _rms_rows(halo_ref[...], g)).astype(BF16)
        acc_ref[...] = jnp.zeros_like(acc_ref)

    a_ref[...] = _dot(hn_ref[...], wa_ref[...])
    v = _dot(hn_ref[hl:, :], wv_ref[...])
    w = wdw_ref[...]
    cv = (w[0:1] * a_ref[hl - 2:hl - 2 + tm, :] + w[1:2] * a_ref[hl - 1:hl - 1 + tm, :]
          + w[2:3] * a_ref[hl:hl + tm, :] + bdw_ref[...])
    act = (cv * _sigmoid(cv) * v).astype(BF16)
    acc_ref[...] += _dot(act, wd_ref[...])

    @pl.when(c == pl.num_programs(1) - 1)
    def _():
        o_ref[...] = x_ref[...] + acc_ref[...]


def _conv_ffn(x, g, w_up, w_dw, b_dw, w_down, seq):
    t, d = x.shape
    dff = w_down.shape[0]
    tm = min(1024, seq)
    tf = 256
    assert seq % tm == 0 and dff % tf == 0 and w_dw.shape[0] == 3
    nch = dff // tf
    hl = BF16_ROWS
    kern = functools.partial(_ffn_kernel, tm=tm, tiles_per_seq=seq // tm)
    return pl.pallas_call(
        kern,
        out_shape=jax.ShapeDtypeStruct((t, d), F32),
        grid=(t // tm, nch),
        in_specs=[
            pl.BlockSpec((tm, d), lambda i, c: (i, 0)),
            pl.BlockSpec((hl, d), lambda i, c: (jnp.maximum(i * (tm // hl) - 1, 0), 0)),
            pl.BlockSpec((1, d), lambda i, c: (0, 0)),
            pl.BlockSpec((d, tf), lambda i, c: (0, c)),
            pl.BlockSpec((d, tf), lambda i, c: (0, c + nch)),
            pl.BlockSpec((3, tf), lambda i, c: (0, c)),
            pl.BlockSpec((1, tf), lambda i, c: (0, c)),
            pl.BlockSpec((tf, d), lambda i, c: (c, 0)),
        ],
        out_specs=pl.BlockSpec((tm, d), lambda i, c: (i, 0)),
        scratch_shapes=[
            pltpu.VMEM((tm + hl, d), BF16),
            pltpu.VMEM((tm + hl, tf), F32),
            pltpu.VMEM((tm, d), F32),
        ],
        compiler_params=_params("parallel", "arbitrary"),
        name="conv_ffn",
    )(x, x, g.reshape(1, d), w_up.astype(BF16), w_up.astype(BF16), w_dw, b_dw.reshape(1, dff),
      w_down.astype(BF16))


def _glu_kernel(x_ref, g_ref, wa_ref, wg_ref, ba_ref, bg_ref, o_ref, hn_ref):
    @pl.when(pl.program_id(1) == 0)
    def _():
        hn_ref[...] = _rms_rows(x_ref[...], g_ref[...]).astype(BF16)

    hn = hn_ref[...]
    a = _dot(hn, wa_ref[...]) + ba_ref[...]
    gate = _dot(hn, wg_ref[...]) + bg_ref[...]
    o_ref[...] = a * _sigmoid(gate)


def _conformer_glu(x, g, w_pw1, b_pw1):
    t, d = x.shape
    tm = min(1024, t)
    tn = 256
    nj = d // tn
    w = w_pw1.astype(BF16)
    b = b_pw1.reshape(1, 2 * d)
    return pl.pallas_call(
        _glu_kernel,
        out_shape=jax.ShapeDtypeStruct((t, d), F32),
        grid=(t // tm, nj),
        in_specs=[
            pl.BlockSpec((tm, d), lambda i, j: (i, 0)),
            pl.BlockSpec((1, d), lambda i, j: (0, 0)),
            pl.BlockSpec((d, tn), lambda i, j: (0, j)),
            pl.BlockSpec((d, tn), lambda i, j: (0, j + nj)),
            pl.BlockSpec((1, tn), lambda i, j: (0, j)),
            pl.BlockSpec((1, tn), lambda i, j: (0, j + nj)),
        ],
        out_specs=pl.BlockSpec((tm, tn), lambda i, j: (i, j)),
        scratch_shapes=[pltpu.VMEM((tm, d), BF16)],
        compiler_params=_params("parallel", "arbitrary"),
        name="conformer_glu",
    )(x, g.reshape(1, d), w, w, b, b)


def _dwconv_kernel(u_ref, halo_ref, x_ref, wdw_ref, bdw_ref, lng_ref, lnb_ref, w2_ref, b2_ref, o_ref,
                   ext_ref, cv_ref, *, tm, halo, width, tiles_per_seq, row_chunk, col_chunk):
    i = pl.program_id(0)
    d = u_ref.shape[1]
    first = (i % tiles_per_seq) == 0
    ext_ref[0:halo, :] = jnp.where(first, 0.0, halo_ref[...])
    ext_ref[halo:, :] = u_ref[...]
    off = halo - (width - 1)
    for r0 in range(0, tm, row_chunk):
        for c0 in range(0, d, col_chunk):
            acc = jnp.broadcast_to(bdw_ref[:, c0:c0 + col_chunk], (row_chunk, col_chunk))
            for k in range(width):
                acc = acc + wdw_ref[k:k + 1, c0:c0 + col_chunk] * ext_ref[r0 + off + k:r0 + off + k + row_chunk,
                                                                         c0:c0 + col_chunk]
            cv_ref[r0:r0 + row_chunk, c0:c0 + col_chunk] = acc
    u = cv_ref[...]
    mu = jnp.mean(u, axis=-1, keepdims=True)
    uc = u - mu
    var = jnp.mean(uc * uc, axis=-1, keepdims=True)
    y = uc * lax.rsqrt(var + EPS) * lng_ref[...] + lnb_ref[...]
    s = (y * _sigmoid(y)).astype(BF16)
    o_ref[...] = x_ref[...] + _dot(s, w2_ref[...]) + b2_ref[...]


def _conformer_conv_out(u, x, w_dw, b_dw, ln_g, ln_b, w_pw2, b_pw2, seq):
    t, d = x.shape
    width = w_dw.shape[0]
    halo = 32
    assert width - 1 <= halo
    tm = min(256, seq)
    assert seq % tm == 0 and tm % halo == 0
    kern = functools.partial(_dwconv_kernel, tm=tm, halo=halo, width=width, tiles_per_seq=seq // tm,
                             row_chunk=32, col_chunk=512)
    vec = lambda: pl.BlockSpec((1, d), lambda i: (0, 0))
    return pl.pallas_call(
        kern,
        out_shape=jax.ShapeDtypeStruct((t, d), F32),
        grid=(t // tm,),
        in_specs=[
            pl.BlockSpec((tm, d), lambda i: (i, 0)),
            pl.BlockSpec((halo, d), lambda i: (jnp.maximum(i * (tm // halo) - 1, 0), 0)),
            pl.BlockSpec((tm, d), lambda i: (i, 0)),
            pl.BlockSpec((width, d), lambda i: (0, 0)),
            vec(), vec(), vec(),
            pl.BlockSpec((d, d), lambda i: (0, 0)),
            vec(),
        ],
        out_specs=pl.BlockSpec((tm, d), lambda i: (i, 0)),
        scratch_shapes=[pltpu.VMEM((tm + halo, d), F32), pltpu.VMEM((tm, d), F32)],
        compiler_params=_params("parallel"),
        name="conformer_dwconv_out",
    )(u, u, x, w_dw, b_dw.reshape(1, d), ln_g.reshape(1, d), ln_b.reshape(1, d), w_pw2.astype(BF16),
      b_pw2.reshape(1, d))


def _rms_matmul_kernel(x_ref, g_ref, w_ref, o_ref, hn_ref):
    @pl.when(pl.program_id(1) == 0)
    def _():
        hn_ref[...] = _rms_rows(x_ref[...], g_ref[...]).astype(BF16)

    o_ref[...] = _dot(hn_ref[...], w_ref[...])


def _rms_matmul(x, g, w):
    t, d = x.shape
    n = w.shape[1]
    tm = min(1024, t)
    tn = 512
    assert n % tn == 0
    return pl.pallas_call(
        _rms_matmul_kernel,
        out_shape=jax.ShapeDtypeStruct((t, n), F32),
        grid=(t // tm, n // tn),
        in_specs=[
            pl.BlockSpec((tm, d), lambda i, j: (i, 0)),
            pl.BlockSpec((1, d), lambda i, j: (0, 0)),
            pl.BlockSpec((d, tn), lambda i, j: (0, j)),
        ],
        out_specs=pl.BlockSpec((tm, tn), lambda i, j: (i, j)),
        scratch_shapes=[pltpu.VMEM((tm, d), BF16)],
        compiler_params=_params("parallel", "arbitrary"),
        name="nsa_in_proj",
    )(x, g.reshape(1, d), w)


def _out_proj_kernel(a_ref, x_ref, w_ref, o_ref):
    o_ref[...] = x_ref[...] + _dot(a_ref[...], w_ref[...])


def _out_proj(a, x, w):
    t, d = x.shape
    tm = min(512, t)
    return pl.pallas_call(
        _out_proj_kernel,
        out_shape=jax.ShapeDtypeStruct((t, d), F32),
        grid=(t // tm,),
        in_specs=[
            pl.BlockSpec((tm, a.shape[1]), lambda i: (i, 0)),
            pl.BlockSpec((tm, d), lambda i: (i, 0)),
            pl.BlockSpec(w.shape, lambda i: (0, 0)),
        ],
        out_specs=pl.BlockSpec((tm, d), lambda i: (i, 0)),
        compiler_params=_params("parallel"),
        name="nsa_out_proj",
    )(a, x, w.astype(BF16))


def _kprep_kernel(k_ref, v_ref, gain_ref, bd_ref, c_ref, s1_ref, s2_ref, kaug_ref, vout_ref, *, ts, seq):
    i = pl.program_id(0)
    bd = bd_ref[...]
    gain = gain_ref[...]
    c, s1, s2 = c_ref[...], s1_ref[...], s2_ref[...]
    tpos = (i * ts) % seq + lax.broadcasted_iota(jnp.int32, (ts, LANES), 0)
    lane = lax.broadcasted_iota(jnp.int32, (ts, LANES), 1)
    onehot = jnp.where(tpos // SEL_LEN == lane, 1.0, 0.0).astype(BF16)
    for g in range(N_KV):
        xb = k_ref[:, g * LANES:(g + 1) * LANES]
        xr = _rope(_head_rms(xb, bd, gain), c, s1, s2)
        kaug_ref[:, 2 * g * LANES:(2 * g + 1) * LANES] = xr.astype(BF16)
        kaug_ref[:, (2 * g + 1) * LANES:(2 * g + 2) * LANES] = onehot
    vout_ref[...] = v_ref[...].astype(BF16)


def _kprep(proj, gain, bd, tabs, seq):
    t = proj.shape[0]
    ts = min(512, seq)
    assert seq % ts == 0
    nk = N_KV * LANES
    kern = functools.partial(_kprep_kernel, ts=ts, seq=seq)
    tab = lambda: pl.BlockSpec((ts, LANES), lambda i: (i % (seq // ts), 0))
    return pl.pallas_call(
        kern,
        out_shape=(jax.ShapeDtypeStruct((t, 2 * nk), BF16), jax.ShapeDtypeStruct((t, nk), BF16)),
        grid=(t // ts,),
        in_specs=[
            pl.BlockSpec((ts, nk), lambda i: (i, 2)),
            pl.BlockSpec((ts, nk), lambda i: (i, 3)),
            pl.BlockSpec((1, LANES), lambda i: (0, 0)),
            pl.BlockSpec((LANES, LANES), lambda i: (0, 0)),
            tab(), tab(), tab(),
        ],
        out_specs=(pl.BlockSpec((ts, 2 * nk), lambda i: (i, 0)), pl.BlockSpec((ts, nk), lambda i: (i, 0))),
        compiler_params=_params("parallel"),
        name="nsa_key_prep",
    )(proj, proj, gain, bd, *tabs)


def _compress_kernel(kc_ref, vc_ref, pek_ref, pev_ref, w1k_ref, w1v_ref, w2k_ref, w2v_ref, gain_ref, o_ref):
    nc = kc_ref.shape[1]

    def hidden(c_ref, pe_ref, w1_ref):
        x = c_ref[0]
        pe = pe_ref[...]
        first = _dot((x + pe[0:1]).astype(BF16), w1_ref[0])
        second = _dot((x + pe[1:2]).astype(BF16), w1_ref[1])
        pre = first + pltpu.roll(second, nc - 1, 0)
        return (pre * _sigmoid(pre)).astype(BF16)

    kv = _dot(hidden(kc_ref, pek_ref, w1k_ref), w2k_ref[...]) + _dot(hidden(vc_ref, pev_ref, w1v_ref), w2v_ref[...])
    is_k = lax.broadcasted_iota(jnp.int32, kv.shape, 1) < HEAD_DIM
    ss = jnp.sum(jnp.where(is_k, kv * kv, 0.0), axis=-1, keepdims=True)
    kn = kv * lax.rsqrt(ss * (1.0 / HEAD_DIM) + EPS) * gain_ref[...]
    o_ref[0] = jnp.where(is_k, kn, kv).astype(BF16)


def _compress(kc, vc, pe_k, pe_v, ck_w1, ck_w2, cv_w1, cv_w2, kc_g):
    bg, nc, cw = kc.shape
    hid = ck_w1.shape[1]
    half = CMP_STRIDE * HEAD_DIM
    assert cw == half and CMP_LEN == 2 * CMP_STRIDE
    pad = lambda w, left: jnp.pad(w, ((0, 0), (HEAD_DIM, 0) if left else (0, HEAD_DIM))).astype(BF16)
    gain = jnp.concatenate([kc_g, jnp.ones((HEAD_DIM,), F32)]).reshape(1, LANES)
    chunk = lambda: pl.BlockSpec((1, nc, cw), lambda i: (i, 0, 0))
    full = lambda shape: pl.BlockSpec(shape, lambda i: (0,) * len(shape))
    return pl.pallas_call(
        _compress_kernel,
        out_shape=jax.ShapeDtypeStruct((bg, nc, LANES), BF16),
        grid=(bg,),
        in_specs=[chunk(), chunk(), full((2, half)), full((2, half)), full((2, half, hid)), full((2, half, hid)),
                  full((hid, LANES)), full((hid, LANES)), full((1, LANES))],
        out_specs=pl.BlockSpec((1, nc, LANES), lambda i: (i, 0, 0)),
        compiler_params=_params("parallel"),
        name="nsa_compress",
    )(kc, vc, pe_k.reshape(2, half), pe_v.reshape(2, half), ck_w1.reshape(2, half, hid).astype(BF16),
      cv_w1.reshape(2, half, hid).astype(BF16), pad(ck_w2, False), pad(cv_w2, True), gain)


def _attn_kernel(q_ref, gate_ref, kaug_ref, v_ref, kvc_ref, cmap_ref, bd_ref, qg_ref, c_ref, s1_ref, s2_ref,
                 o_ref,
                 qsel_ref, qwin_ref, oc_ref, ms_ref, ls_ref, accs_ref, mw_ref, lw_ref, accw_ref,
                 *, tq, tk, tw, k_top):
    qi = pl.program_id(2)
    t0 = qi * tq
    rows = GROUP * tq
    scale = HEAD_DIM ** -0.5
    lane = lax.broadcasted_iota(jnp.int32, (tq, LANES), 1)
    low = lane < HEAD_DIM

    bd = bd_ref[...]
    c, s1, s2 = c_ref[...], s1_ref[...], s2_ref[...]
    nope_lo, rope_lo, rope_hi = [], [], []
    for pair in range(GROUP // 2):
        sl = slice(pair * LANES, (pair + 1) * LANES)
        qn = _head_rms(q_ref[:, sl], bd, qg_ref[...]) * scale
        qr = _rope(qn, c, s1, s2)
        qn_sw = pltpu.roll(qn, HEAD_DIM, 1)
        qr_sw = pltpu.roll(qr, HEAD_DIM, 1)
        nope_lo += [jnp.where(low, qn, 0.0), jnp.where(low, qn_sw, 0.0)]
        rope_lo += [jnp.where(low, qr, 0.0), jnp.where(low, qr_sw, 0.0)]
        rope_hi += [jnp.where(low, 0.0, qr_sw), jnp.where(low, 0.0, qr)]
    q_nope = jnp.concatenate(nope_lo, axis=0).astype(BF16)
    qwin_ref[...] = jnp.concatenate(rope_hi, axis=0).astype(BF16)

    rpos = t0 + lax.broadcasted_iota(jnp.int32, (rows, 1), 0) % tq

    kvc = kvc_ref[0]
    nc = kvc.shape[0]
    s_c = _dot_nt(q_nope, kvc)
    cmp_end = lax.broadcasted_iota(jnp.int32, (1, nc), 1) * CMP_STRIDE + (CMP_LEN - 1)
    valid = cmp_end <= rpos
    s_c = jnp.where(valid, s_c, NEG)
    e = jnp.where(valid, jnp.exp(s_c - jnp.max(s_c, axis=-1, keepdims=True)), 0.0)
    p_c = e / jnp.maximum(jnp.sum(e, axis=-1, keepdims=True), 1e-30)
    oc_ref[...] = _dot(p_c.astype(BF16), kvc)

    p_sum = p_c[0:tq]
    for r in range(1, GROUP):
        p_sum = p_sum + p_c[r * tq:(r + 1) * tq]
    hi, lo = _split_bf16(p_sum)
    imp = _dot(hi, cmap_ref[...]) + _dot(lo, cmap_ref[...])
    cur = (t0 + lax.broadcasted_iota(jnp.int32, (tq, 1), 0)) // SEL_LEN
    forced = (lane == 0) | (lane == cur) | (lane == cur - 1)
    imp = jnp.where(lane > cur, -1.0, jnp.where(forced, 1e4, imp))
    lane_f = lane.astype(F32)
    sel = jnp.zeros((tq, LANES), F32)
    for _ in range(k_top):
        mx = jnp.max(imp, axis=-1, keepdims=True)
        idx = jnp.min(jnp.where(imp == mx, lane_f, float(LANES)), axis=-1, keepdims=True)
        pick = lane_f == idx
        sel = jnp.where(pick, 1.0, sel)
        imp = jnp.where(pick, -3e38, imp)
    bias = jnp.where(sel > 0.0, 0.0, NEG)
    q_rope = jnp.concatenate(rope_lo, axis=0).astype(BF16)
    qsel_ref[...] = jnp.concatenate([q_rope, jnp.concatenate([bias] * GROUP, axis=0).astype(BF16)], axis=1)

    def online_update(s, keep, v, m_ref, l_ref, acc_ref):
        m_prev = m_ref[...]
        m_new = jnp.maximum(m_prev, jnp.max(s, axis=-1, keepdims=True))
        p = jnp.exp(s - m_new)
        if keep is not None:
            p = jnp.where(keep, p, 0.0)
        alpha = jnp.exp(m_prev - m_new)
        l_ref[...] = alpha * l_ref[...] + jnp.sum(p, axis=-1, keepdims=True)
        acc_ref[...] = alpha * acc_ref[...] + _dot(p.astype(BF16), v)
        m_ref[...] = m_new

    for m_ref, l_ref, acc_ref in ((ms_ref, ls_ref, accs_ref), (mw_ref, lw_ref, accw_ref)):
        m_ref[...] = jnp.full_like(m_ref, NEG)
        l_ref[...] = jnp.zeros_like(l_ref)
        acc_ref[...] = jnp.zeros_like(acc_ref)

    def sel_tile(j, masked):
        k0 = pl.multiple_of(j * tk, tk)
        s = _dot_nt(qsel_ref[...], kaug_ref[pl.ds(k0, tk), :])
        keep = None
        if masked:
            keep = (k0 + lax.broadcasted_iota(jnp.int32, (1, tk), 1)) <= rpos
            s = jnp.where(keep, s, NEG)
        online_update(s, keep, v_ref[pl.ds(k0, tk), :], ms_ref, ls_ref, accs_ref)

    n_full = t0 // tk

    def sel_body(j, carry):
        sel_tile(j, False)
        return carry

    lax.fori_loop(0, n_full, sel_body, 0)
    sel_tile(n_full, True)

    def win_body(j, carry):
        k0 = pl.multiple_of(j * tw, tw)
        s = _dot_nt(qwin_ref[...], kaug_ref[pl.ds(k0, tw), 0:LANES])
        back = rpos - (k0 + lax.broadcasted_iota(jnp.int32, (1, tw), 1))
        keep = (back >= 0) & (back < WINDOW)
        s = jnp.where(keep, s, NEG)
        online_update(s, keep, v_ref[pl.ds(k0, tw), :], mw_ref, lw_ref, accw_ref)
        return carry

    j_last = t0 // tw + (tq // tw - 1)
    j_first = jnp.maximum(t0 - WINDOW, 0) // tw
    lax.fori_loop(j_first, j_last + 1, win_body, 0)

    o_s = accs_ref[...] / ls_ref[...]
    o_w = accw_ref[...] / lw_ref[...]
    o_c = oc_ref[...]
    sig = _sigmoid(gate_ref[...])
    heads = []
    for r in range(GROUP):
        sl = slice(r * tq, (r + 1) * tq)
        g_c, g_s, g_w = (sig[:, 3 * r + b:3 * r + b + 1] for b in range(3))
        upper = g_c * o_c[sl] + g_w * o_w[sl]
        heads.append(g_s * o_s[sl] + pltpu.roll(upper, HEAD_DIM, 1))
    pairs = [jnp.where(low, heads[2 * p], pltpu.roll(heads[2 * p + 1], HEAD_DIM, 1)) for p in range(GROUP // 2)]
    o_ref[...] = jnp.concatenate(pairs, axis=1).astype(o_ref.dtype)


def _attention(proj, kaug, vall, kvc, cmap, bd, qg, tabs, batch, seq):
    t = proj.shape[0]
    tq = min(256, seq)
    tk = min(512, seq)
    tw = tq
    nq = seq // tq
    nc = kvc.shape[1]
    assert seq % tk == 0 and tk % tq == 0 and WINDOW % tw == 0 and tq & (tq - 1) == 0
    rows = GROUP * tq
    gate_col0 = (N_HEADS * HEAD_DIM + 3 * N_KV * LANES) // LANES
    kern = functools.partial(_attn_kernel, tq=tq, tk=tk, tw=tw, k_top=min(SEL_TOPK, seq // SEL_LEN))
    tab = lambda: pl.BlockSpec((tq, LANES), lambda b, g, qi: (qi, 0))
    col = lambda: pltpu.VMEM((rows, 1), F32)
    acc = lambda: pltpu.VMEM((rows, LANES), F32)
    return pl.pallas_call(
        kern,
        out_shape=jax.ShapeDtypeStruct((t, N_HEADS * HEAD_DIM), BF16),
        grid=(batch, N_KV, nq),
        in_specs=[
            pl.BlockSpec((tq, GROUP * HEAD_DIM), lambda b, g, qi: (b * nq + qi, g)),
            pl.BlockSpec((tq, LANES), lambda b, g, qi: (b * nq + qi, gate_col0 + g)),
            pl.BlockSpec((seq, 2 * LANES), lambda b, g, qi: (b, g)),
            pl.BlockSpec((seq, LANES), lambda b, g, qi: (b, g)),
            pl.BlockSpec((1, nc, LANES), lambda b, g, qi: (b * N_KV + g, 0, 0)),
            pl.BlockSpec((nc, LANES), lambda b, g, qi: (0, 0)),
            pl.BlockSpec((LANES, LANES), lambda b, g, qi: (0, 0)),
            pl.BlockSpec((1, LANES), lambda b, g, qi: (0, 0)),
            tab(), tab(), tab(),
        ],
        out_specs=pl.BlockSpec((tq, GROUP * HEAD_DIM), lambda b, g, qi: (b * nq + qi, g)),
        scratch_shapes=[
            pltpu.VMEM((rows, 2 * LANES), BF16), pltpu.VMEM((rows, LANES), BF16), acc(),
            col(), col(), acc(), col(), col(), acc(),
        ],
        compiler_params=_params("parallel", "parallel", "arbitrary"),
        name="nsa_attention",
    )(proj, proj, kaug, vall, kvc, cmap, bd, qg, *tabs)


def _rope_tables(seq):
    half = ROT_DIM // 2
    inv_freq = ROPE_THETA ** (-jnp.arange(half, dtype=F32) * (2.0 / ROT_DIM))
    ang = jnp.arange(seq, dtype=F32)[:, None] * inv_freq[None, :]
    cos, sin = jnp.cos(ang), jnp.sin(ang)
    zeros = jnp.zeros((seq, HEAD_DIM - ROT_DIM), F32)
    zh = jnp.zeros((seq, half), F32)
    c = jnp.concatenate([cos, cos, zeros + 1.0], axis=1)
    s1 = jnp.concatenate([-sin, zh, zeros], axis=1)
    s2 = jnp.concatenate([zh, sin, zeros], axis=1)
    reps = LANES // HEAD_DIM
    return tuple(jnp.tile(a, (1, reps)) for a in (c, s1, s2))


def _cmp_to_sel(nc, n_sel):
    start_c = np.arange(nc)[:, None] * CMP_STRIDE
    start_s = np.arange(LANES)[None, :] * SEL_LEN
    ov = np.minimum(start_c + CMP_LEN, start_s + SEL_LEN) - np.maximum(start_c, start_s)
    m = np.maximum(ov, 0).astype(np.float32) / CMP_LEN
    m[:, n_sel:] = 0.0
    m[nc - 1:, :] = 0.0
    return jnp.asarray(m, BF16)


def _nsa_in_weight(w_in):
    d = w_in.shape[0]
    hq, hk = N_HEADS * HEAD_DIM, N_KV * HEAD_DIM
    sizes = [hq] + [hk] * 6
    offs = np.cumsum([0] + sizes)
    q, kc, vc, ks, vs, kw, vw = (w_in[:, offs[n]:offs[n + 1]] for n in range(7))
    gl = w_in[:, offs[7]:]
    per_group = lambda a, b: jnp.stack([a.reshape(d, N_KV, HEAD_DIM), b.reshape(d, N_KV, HEAD_DIM)],
                                       axis=2).reshape(d, N_KV * LANES)
    gates = jnp.pad(gl.reshape(d, N_KV, 3 * GROUP), ((0, 0), (0, 0), (0, LANES - 3 * GROUP))).reshape(d, N_KV * LANES)
    return jnp.concatenate([q, per_group(ks, kw), per_group(vs, vw), kc, vc, gates], axis=1).astype(BF16)


def _nsa_layer(x, g, w_in, q_g, kc_g, ks_g, kw_g, pe_k, pe_v, ck_w1, ck_w2, cv_w1, cv_w2, w_out, batch, seq):
    nc = seq // CMP_STRIDE
    n_sel = seq // SEL_LEN
    assert n_sel <= LANES and seq % CMP_STRIDE == 0
    proj = _rms_matmul(x, g, _nsa_in_weight(w_in))
    tabs = _rope_tables(seq)
    seg = np.arange(LANES) // HEAD_DIM
    bd = jnp.asarray(seg[:, None] == seg[None, :], BF16)
    kaug, vall = _kprep(proj, jnp.concatenate([ks_g, kw_g]).reshape(1, LANES), bd, tabs, seq)
    hk = N_KV * HEAD_DIM
    c0 = N_HEADS * HEAD_DIM + 2 * N_KV * LANES
    chunks = lambda cols: (cols.reshape(batch, seq, N_KV, HEAD_DIM).transpose(0, 2, 1, 3)
                           .reshape(batch * N_KV, nc, CMP_STRIDE * HEAD_DIM))
    kvc = _compress(chunks(proj[:, c0:c0 + hk]), chunks(proj[:, c0 + hk:c0 + 2 * hk]),
                    pe_k, pe_v, ck_w1, ck_w2, cv_w1, cv_w2, kc_g)
    attn = _attention(proj, kaug, vall, kvc, _cmp_to_sel(nc, n_sel), bd, jnp.tile(q_g, 2).reshape(1, LANES),
                      tabs, batch, seq)
    return _out_proj(attn, x, w_out)


def kernel(x, mix_norm_g, ffn_norm_g, conv_w_pw1, conv_b_pw1, conv_w_dw, conv_b_dw, conv_ln_g, conv_ln_b, conv_w_pw2, conv_b_pw2, nsa_w_in, nsa_q_norm, nsa_kc_norm, nsa_ks_norm, nsa_kw_norm, nsa_pe_k, nsa_pe_v, nsa_ck_w1, nsa_ck_w2, nsa_cv_w1, nsa_cv_w2, nsa_w_out, ffn_w_up, ffn_w_dw, ffn_b_dw, ffn_w_down):
    batch, seq, d = x.shape
    depth = mix_norm_g.shape[0]
    n_mixers = 2
    h = x.reshape(batch * seq, d)
    for i in range(depth):
        j = i // n_mixers
        if i % n_mixers == 0:
            u = _conformer_glu(h, mix_norm_g[i], conv_w_pw1[j], conv_b_pw1[j])
            h = _conformer_conv_out(u, h, conv_w_dw[j], conv_b_dw[j], conv_ln_g[j], conv_ln_b[j],
                                    conv_w_pw2[j], conv_b_pw2[j], seq)
        else:
            h = _nsa_layer(h, mix_norm_g[i], nsa_w_in[j], nsa_q_norm[j], nsa_kc_norm[j], nsa_ks_norm[j],
                           nsa_kw_norm[j], nsa_pe_k[j], nsa_pe_v[j], nsa_ck_w1[j], nsa_ck_w2[j], nsa_cv_w1[j],
                           nsa_cv_w2[j], nsa_w_out[j], batch, seq)
        h = _conv_ffn(h, ffn_norm_g[i], ffn_w_up[i], ffn_w_dw[i], ffn_b_dw[i], ffn_w_down[i], seq)
    return h.reshape(batch, seq, d)
```

```python
import functools

import numpy as np
import jax
import jax.numpy as jnp
from jax import lax
from jax.experimental import pallas as pl
from jax.experimental.pallas import tpu as pltpu

N_HEADS = 16
HEAD_DIM = 64
N_KV = 4
GROUP = N_HEADS // N_KV
ROT_DIM = HEAD_DIM // 4
ROPE_THETA = 500000.0
CMP_LEN = 32
CMP_STRIDE = 16
SEL_LEN = 64
SEL_TOPK = 16
WINDOW = 512
EPS = 1e-6
NEG = -1e30

LANES = 128
SUBLANES = 8
BF16_ROWS = 16
V_TILE = 256
GATE_ROWS = 16
VMEM_LIMIT = 48 * 1024 * 1024

F32 = jnp.float32
BF16 = jnp.bfloat16
NT_DIMS = (((1,), (1,)), ((), ()))


def _params(*sem):
    return pltpu.CompilerParams(dimension_semantics=sem, vmem_limit_bytes=VMEM_LIMIT)


def _dot(a, b):
    return jnp.dot(a, b, preferred_element_type=F32)


def _dot_nt(a, b):
    return lax.dot_general(a, b, NT_DIMS, preferred_element_type=F32)


def _split_bf16(x):
    hi = x.astype(BF16)
    lo = (x - hi.astype(F32)).astype(BF16)
    return hi, lo


def _rms_rows(x, g):
    ms = jnp.mean(x * x, axis=-1, keepdims=True)
    return x * lax.rsqrt(ms + EPS) * g


def _sigmoid(x):
    return 1.0 / (1.0 + jnp.exp(-x))


def _head_rms(xb, bd, gain):
    hi, lo = _split_bf16(xb * xb)
    ss = _dot(hi, bd) + _dot(lo, bd)
    return xb * lax.rsqrt(ss * (1.0 / HEAD_DIM) + EPS) * gain


def _rope(xb, c, s1, s2):
    half = ROT_DIM // 2
    return xb * c + pltpu.roll(xb, LANES - half, 1) * s1 + pltpu.roll(xb, half, 1) * s2


def _ffn_kernel(x_ref, halo_ref, g_ref, wa_ref, wv_ref, wdw_ref, bdw_ref, wd_ref, o_ref,
                hn_ref, a_ref, acc_ref, *, tm, tiles_per_seq):
    i = pl.program_id(0)
    c = pl.program_id(1)
    hl = BF16_ROWS

    @pl.when(c == 0)
    def _():
        g = g_ref[...]
        hn_ref[hl:, :] = _rms_rows(x_ref[...], g).astype(BF16)
        first = (i % tiles_per_seq) == 0
        hn_ref[0:hl, :] = jnp.where(first, 0.0, _rms_rows(halo_ref[...], g)).astype(BF16)
        acc_ref[...] = jnp.zeros_like(acc_ref)

    a_ref[...] = _dot(hn_ref[...], wa_ref[...])
    v = _dot(hn_ref[hl:, :], wv_ref[...])
    w = wdw_ref[...]
    cv = (w[0:1] * a_ref[hl - 2:hl - 2 + tm, :] + w[1:2] * a_ref[hl - 1:hl - 1 + tm, :]
          + w[2:3] * a_ref[hl:hl + tm, :] + bdw_ref[...])
    act = (cv * _sigmoid(cv) * v).astype(BF16)
    acc_ref[...] += _dot(act, wd_ref[...])

    @pl.when(c == pl.num_programs(1) - 1)
    def _():
        o_ref[...] = x_ref[...] + acc_ref[...]


def _conv_ffn(x, g, w_up, w_dw, b_dw, w_down, seq):
    t, d = x.shape
    dff = w_down.shape[0]
    tm = min(1024, seq)
    tf = 256
    assert seq % tm == 0 and dff % tf == 0 and w_dw.shape[0] == 3
    nch = dff // tf
    hl = BF16_ROWS
    kern = functools.partial(_ffn_kernel, tm=tm, tiles_per_seq=seq // tm)
    return pl.pallas_call(
        kern,
        out_shape=jax.ShapeDtypeStruct((t, d), F32),
        grid=(t // tm, nch),
        in_specs=[
            pl.BlockSpec((tm, d), lambda i, c: (i, 0)),
            pl.BlockSpec((hl, d), lambda i, c: (jnp.maximum(i * (tm // hl) - 1, 0), 0)),
            pl.BlockSpec((1, d), lambda i, c: (0, 0)),
            pl.BlockSpec((d, tf), lambda i, c: (0, c)),
            pl.BlockSpec((d, tf), lambda i, c: (0, c + nch)),
            pl.BlockSpec((3, tf), lambda i, c: (0, c)),
            pl.BlockSpec((1, tf), lambda i, c: (0, c)),
            pl.BlockSpec((tf, d), lambda i, c: (c, 0)),
        ],
        out_specs=pl.BlockSpec((tm, d), lambda i, c: (i, 0)),
        scratch_shapes=[
            pltpu.VMEM((tm + hl, d), BF16),
            pltpu.VMEM((tm + hl, tf), F32),
            pltpu.VMEM((tm, d), F32),
        ],
        compiler_params=_params("parallel", "arbitrary"),
        name="conv_ffn",
    )(x, x, g.reshape(1, d), w_up.astype(BF16), w_up.astype(BF16), w_dw, b_dw.reshape(1, dff),
      w_down.astype(BF16))


def _glu_kernel(x_ref, g_ref, wa_ref, wg_ref, ba_ref, bg_ref, o_ref, hn_ref):
    @pl.when(pl.program_id(1) == 0)
    def _():
        hn_ref[...] = _rms_rows(x_ref[...], g_ref[...]).astype(BF16)

    hn = hn_ref[...]
    a = _dot(hn, wa_ref[...]) + ba_ref[...]
    gate = _dot(hn, wg_ref[...]) + bg_ref[...]
    o_ref[...] = a * _sigmoid(gate)


def _conformer_glu(x, g, w_pw1, b_pw1):
    t, d = x.shape
    tm = min(1024, t)
    tn = 256
    nj = d // tn
    w = w_pw1.astype(BF16)
    b = b_pw1.reshape(1, 2 * d)
    return pl.pallas_call(
        _glu_kernel,
        out_shape=jax.ShapeDtypeStruct((t, d), F32),
        grid=(t // tm, nj),
        in_specs=[
            pl.BlockSpec((tm, d), lambda i, j: (i, 0)),
            pl.BlockSpec((1, d), lambda i, j: (0, 0)),
            pl.BlockSpec((d, tn), lambda i, j: (0, j)),
            pl.BlockSpec((d, tn), lambda i, j: (0, j + nj)),
            pl.BlockSpec((1, tn), lambda i, j: (0, j)),
            pl.BlockSpec((1, tn), lambda i, j: (0, j + nj)),
        ],
        out_specs=pl.BlockSpec((tm, tn), lambda i, j: (i, j)),
        scratch_shapes=[pltpu.VMEM((tm, d), BF16)],
        compiler_params=_params("parallel", "arbitrary"),
        name="conformer_glu",
    )(x, g.reshape(1, d), w, w, b, b)


def _dwconv_kernel(u_ref, halo_ref, x_ref, wdw_ref, bdw_ref, lng_ref, lnb_ref, w2_ref, b2_ref, o_ref,
                   ext_ref, cv_ref, *, tm, halo, width, tiles_per_seq, row_chunk, col_chunk):
    i = pl.program_id(0)
    d = u_ref.shape[1]
    first = (i % tiles_per_seq) == 0
    ext_ref[0:halo, :] = jnp.where(first, 0.0, halo_ref[...])
    ext_ref[halo:, :] = u_ref[...]
    off = halo - (width - 1)
    for r0 in range(0, tm, row_chunk):
        for c0 in range(0, d, col_chunk):
            acc = jnp.broadcast_to(bdw_ref[:, c0:c0 + col_chunk], (row_chunk, col_chunk))
            for k in range(width):
                acc = acc + wdw_ref[k:k + 1, c0:c0 + col_chunk] * ext_ref[r0 + off + k:r0 + off + k + row_chunk,
                                                                         c0:c0 + col_chunk]
            cv_ref[r0:r0 + row_chunk, c0:c0 + col_chunk] = acc
    u = cv_ref[...]
    mu = jnp.mean(u, axis=-1, keepdims=True)
    uc = u - mu
    var = jnp.mean(uc * uc, axis=-1, keepdims=True)
    y = uc * lax.rsqrt(var + EPS) * lng_ref[...] + lnb_ref[...]
    s = (y * _sigmoid(y)).astype(BF16)
    o_ref[...] = x_ref[...] + _dot(s, w2_ref[...]) + b2_ref[...]


def _conformer_conv_out(u, x, w_dw, b_dw, ln_g, ln_b, w_pw2, b_pw2, seq):
    t, d = x.shape
    width = w_dw.shape[0]
    halo = 32
    assert width - 1 <= halo
    tm = min(256, seq)
    assert seq % tm == 0 and tm % halo == 0
    kern = functools.partial(_dwconv_kernel, tm=tm, halo=halo, width=width, tiles_per_seq=seq // tm,
                             row_chunk=32, col_chunk=512)
    vec = lambda: pl.BlockSpec((1, d), lambda i: (0, 0))
    return pl.pallas_call(
        kern,
        out_shape=jax.ShapeDtypeStruct((t, d), F32),
        grid=(t // tm,),
        in_specs=[
            pl.BlockSpec((tm, d), lambda i: (i, 0)),
            pl.BlockSpec((halo, d), lambda i: (jnp.maximum(i * (tm // halo) - 1, 0), 0)),
            pl.BlockSpec((tm, d), lambda i: (i, 0)),
            pl.BlockSpec((width, d), lambda i: (0, 0)),
            vec(), vec(), vec(),
            pl.BlockSpec((d, d), lambda i: (0, 0)),
            vec(),
        ],
        out_specs=pl.BlockSpec((tm, d), lambda i: (i, 0)),
        scratch_shapes=[pltpu.VMEM((tm + halo, d), F32), pltpu.VMEM((tm, d), F32)],
        compiler_params=_params("parallel"),
        name="conformer_dwconv_out",
    )(u, u, x, w_dw, b_dw.reshape(1, d), ln_g.reshape(1, d), ln_b.reshape(1, d), w_pw2.astype(BF16),
      b_pw2.reshape(1, d))


def _in_proj_kernel(x_ref, g_ref, wn_ref, wt_ref, on_ref, oq_ref, ov_ref, *, n_qg, row_chunk):
    hn = _rms_rows(x_ref[...], g_ref[...]).astype(BF16)
    on_ref[...] = _dot(hn, wn_ref[...])
    for r0 in range(0, n_qg, row_chunk):
        r1 = min(r0 + row_chunk, n_qg)
        oq_ref[r0:r1, :] = _dot_nt(wt_ref[r0:r1, :], hn)
    vt = _dot_nt(wt_ref[n_qg:, :], hn).astype(BF16)
    for jj in range(ov_ref.shape[1]):
        for g in range(N_KV):
            ov_ref[0, jj, g] = vt[g * LANES:(g + 1) * LANES, jj * V_TILE:(jj + 1) * V_TILE]


def _in_proj(x, g, wn, wt, batch, seq):
    t, d = x.shape
    tm = min(512, seq)
    n_qg = N_HEADS * HEAD_DIM + N_KV * GATE_ROWS
    assert seq % tm == 0 and tm % V_TILE == 0 and wt.shape[0] == n_qg + N_KV * LANES
    per_seq = seq // tm
    kern = functools.partial(_in_proj_kernel, n_qg=n_qg, row_chunk=512)
    return pl.pallas_call(
        kern,
        out_shape=(jax.ShapeDtypeStruct((t, wn.shape[1]), F32),
                   jax.ShapeDtypeStruct((n_qg, t), F32),
                   jax.ShapeDtypeStruct((batch, seq // V_TILE, N_KV, LANES, V_TILE), BF16)),
        grid=(t // tm,),
        in_specs=[
            pl.BlockSpec((tm, d), lambda i: (i, 0)),
            pl.BlockSpec((1, d), lambda i: (0, 0)),
            pl.BlockSpec(wn.shape, lambda i: (0, 0)),
            pl.BlockSpec(wt.shape, lambda i: (0, 0)),
        ],
        out_specs=(pl.BlockSpec((tm, wn.shape[1]), lambda i: (i, 0)),
                   pl.BlockSpec((n_qg, tm), lambda i: (0, i)),
                   pl.BlockSpec((1, tm // V_TILE, N_KV, LANES, V_TILE),
                                lambda i: (i // per_seq, i % per_seq, 0, 0, 0))),
        compiler_params=_params("parallel"),
        name="nsa_in_proj",
    )(x, g.reshape(1, d), wn, wt)


def _out_proj_kernel(a_ref, x_ref, w_ref, o_ref):
    o_ref[...] = x_ref[...] + _dot(a_ref[...], w_ref[...])


def _out_proj(a, x, w):
    t, d = x.shape
    tm = min(512, t)
    return pl.pallas_call(
        _out_proj_kernel,
        out_shape=jax.ShapeDtypeStruct((t, d), F32),
        grid=(t // tm,),
        in_specs=[
            pl.BlockSpec((tm, a.shape[1]), lambda i: (i, 0)),
            pl.BlockSpec((tm, d), lambda i: (i, 0)),
            pl.BlockSpec(w.shape, lambda i: (0, 0)),
        ],
        out_specs=pl.BlockSpec((tm, d), lambda i: (i, 0)),
        compiler_params=_params("parallel"),
        name="nsa_out_proj",
    )(a, x, w.astype(BF16))


def _kprep_kernel(k_ref, gain_ref, bd_ref, c_ref, s1_ref, s2_ref, kaug_ref, *, ts, seq):
    i = pl.program_id(0)
    bd = bd_ref[...]
    gain = gain_ref[...]
    c, s1, s2 = c_ref[...], s1_ref[...], s2_ref[...]
    tpos = (i * ts) % seq + lax.broadcasted_iota(jnp.int32, (ts, LANES), 0)
    lane = lax.broadcasted_iota(jnp.int32, (ts, LANES), 1)
    onehot = jnp.where(tpos // SEL_LEN == lane, 1.0, 0.0).astype(BF16)
    for g in range(N_KV):
        xb = k_ref[:, g * LANES:(g + 1) * LANES]
        xr = _rope(_head_rms(xb, bd, gain), c, s1, s2)
        kaug_ref[:, 2 * g * LANES:(2 * g + 1) * LANES] = xr.astype(BF16)
        kaug_ref[:, (2 * g + 1) * LANES:(2 * g + 2) * LANES] = onehot


def _kprep(proj, gain, bd, tabs, seq):
    t = proj.shape[0]
    ts = min(512, seq)
    assert seq % ts == 0
    nk = N_KV * LANES
    kern = functools.partial(_kprep_kernel, ts=ts, seq=seq)
    tab = lambda: pl.BlockSpec((ts, LANES), lambda i: (i % (seq // ts), 0))
    return pl.pallas_call(
        kern,
        out_shape=jax.ShapeDtypeStruct((t, 2 * nk), BF16),
        grid=(t // ts,),
        in_specs=[
            pl.BlockSpec((ts, nk), lambda i: (i, 0)),
            pl.BlockSpec((1, LANES), lambda i: (0, 0)),
            pl.BlockSpec((LANES, LANES), lambda i: (0, 0)),
            tab(), tab(), tab(),
        ],
        out_specs=pl.BlockSpec((ts, 2 * nk), lambda i: (i, 0)),
        compiler_params=_params("parallel"),
        name="nsa_key_prep",
    )(proj, gain, bd, *tabs)


def _compress_kernel(kc_ref, vc_ref, pek_ref, pev_ref, w1k_ref, w1v_ref, w2k_ref, w2v_ref, gain_ref, o_ref, ot_ref):
    nc = kc_ref.shape[1]

    def hidden(c_ref, pe_ref, w1_ref):
        x = c_ref[0]
        pe = pe_ref[...]
        first = _dot((x + pe[0:1]).astype(BF16), w1_ref[0])
        second = _dot((x + pe[1:2]).astype(BF16), w1_ref[1])
        pre = first + pltpu.roll(second, nc - 1, 0)
        return (pre * _sigmoid(pre)).astype(BF16)

    kv = _dot(hidden(kc_ref, pek_ref, w1k_ref), w2k_ref[...]) + _dot(hidden(vc_ref, pev_ref, w1v_ref), w2v_ref[...])
    is_k = lax.broadcasted_iota(jnp.int32, kv.shape, 1) < HEAD_DIM
    ss = jnp.sum(jnp.where(is_k, kv * kv, 0.0), axis=-1, keepdims=True)
    kn = kv * lax.rsqrt(ss * (1.0 / HEAD_DIM) + EPS) * gain_ref[...]
    out = jnp.where(is_k, kn, kv)
    o_ref[0] = out.astype(BF16)
    ot_ref[0] = out.T.astype(BF16)


def _compress(kc, vc, pe_k, pe_v, ck_w1, ck_w2, cv_w1, cv_w2, kc_g):
    bg, nc, cw = kc.shape
    hid = ck_w1.shape[1]
    half = CMP_STRIDE * HEAD_DIM
    assert cw == half and CMP_LEN == 2 * CMP_STRIDE
    pad = lambda w, left: jnp.pad(w, ((0, 0), (HEAD_DIM, 0) if left else (0, HEAD_DIM))).astype(BF16)
    gain = jnp.concatenate([kc_g, jnp.ones((HEAD_DIM,), F32)]).reshape(1, LANES)
    chunk = lambda: pl.BlockSpec((1, nc, cw), lambda i: (i, 0, 0))
    full = lambda shape: pl.BlockSpec(shape, lambda i: (0,) * len(shape))
    return pl.pallas_call(
        _compress_kernel,
        out_shape=(jax.ShapeDtypeStruct((bg, nc, LANES), BF16), jax.ShapeDtypeStruct((bg, LANES, nc), BF16)),
        grid=(bg,),
        in_specs=[chunk(), chunk(), full((2, half)), full((2, half)), full((2, half, hid)), full((2, half, hid)),
                  full((hid, LANES)), full((hid, LANES)), full((1, LANES))],
        out_specs=(pl.BlockSpec((1, nc, LANES), lambda i: (i, 0, 0)), pl.BlockSpec((1, LANES, nc), lambda i: (i, 0, 0))),
        compiler_params=_params("parallel"),
        name="nsa_compress",
    )(kc, vc, pe_k.reshape(2, half), pe_v.reshape(2, half), ck_w1.reshape(2, half, hid).astype(BF16),
      cv_w1.reshape(2, half, hid).astype(BF16), pad(ck_w2, False), pad(cv_w2, True), gain)


def _attn_kernel(qt_ref, gate_ref, kaug_ref, vt_ref, kvc_ref, kvct_ref, cmapt_ref, qg_ref, cos_ref, sin_ref,
                 o_ref,
                 qaug_ref, qwin_ref, oc_ref, ms_ref, ls_ref, accs_ref, mw_ref, lw_ref, accw_ref,
                 *, tq, tk, k_top):
    qi = pl.program_id(2)
    t0 = qi * tq
    cols = GROUP * tq
    half = ROT_DIM // 2

    gain = qg_ref[...]
    cos, sin = cos_ref[...], sin_ref[...]
    nope, rope = [], []
    for r in range(GROUP):
        x = qt_ref[r * HEAD_DIM:(r + 1) * HEAD_DIM, :]
        ss = jnp.sum(x * x, axis=0, keepdims=True)
        xn = x * lax.rsqrt(ss * (1.0 / HEAD_DIM) + EPS) * gain
        x1, x2 = xn[0:half], xn[half:ROT_DIM]
        nope.append(xn)
        rope.append(jnp.concatenate([x1 * cos - x2 * sin, x2 * cos + x1 * sin, xn[ROT_DIM:]], axis=0))
    zeros = jnp.zeros((HEAD_DIM, cols), BF16)
    q_nope = jnp.concatenate([jnp.concatenate(nope, axis=1).astype(BF16), zeros], axis=0)
    q_rope = jnp.concatenate(rope, axis=1).astype(BF16)
    qwin_ref[...] = jnp.concatenate([zeros, q_rope], axis=0)

    cpos = t0 + lax.broadcasted_iota(jnp.int32, (1, cols), 1) % tq

    kvc = kvc_ref[0]
    nc = kvc.shape[0]
    s_c = _dot(kvc, q_nope)
    cmp_end = lax.broadcasted_iota(jnp.int32, (nc, 1), 0) * CMP_STRIDE + (CMP_LEN - 1)
    valid = cmp_end <= cpos
    s_c = jnp.where(valid, s_c, NEG)
    e = jnp.where(valid, jnp.exp(s_c - jnp.max(s_c, axis=0, keepdims=True)), 0.0)
    p_c = e / jnp.maximum(jnp.sum(e, axis=0, keepdims=True), 1e-30)
    oc_ref[...] = _dot(kvct_ref[0], p_c.astype(BF16))

    p_sum = p_c[:, 0:tq]
    for r in range(1, GROUP):
        p_sum = p_sum + p_c[:, r * tq:(r + 1) * tq]
    hi, lo = _split_bf16(p_sum)
    imp = _dot(cmapt_ref[...], hi) + _dot(cmapt_ref[...], lo)
    blk = lax.broadcasted_iota(jnp.int32, (LANES, tq), 0)
    cur = (t0 + lax.broadcasted_iota(jnp.int32, (1, tq), 1)) // SEL_LEN
    forced = (blk == 0) | (blk == cur) | (blk == cur - 1)
    imp = jnp.where(blk > cur, -1.0, jnp.where(forced, 1e4, imp))
    blk_f = blk.astype(F32)
    sel = jnp.zeros((LANES, tq), F32)
    for _ in range(k_top):
        mx = jnp.max(imp, axis=0, keepdims=True)
        idx = jnp.min(jnp.where(imp == mx, blk_f, float(LANES)), axis=0, keepdims=True)
        pick = blk_f == idx
        sel = jnp.where(pick, 1.0, sel)
        imp = jnp.where(pick, -3e38, imp)
    bias = jnp.where(sel > 0.0, 0.0, NEG).astype(BF16)
    qaug_ref[...] = jnp.concatenate([q_rope, zeros, jnp.concatenate([bias] * GROUP, axis=1)], axis=0)

    for m_ref, l_ref, acc_ref in ((ms_ref, ls_ref, accs_ref), (mw_ref, lw_ref, accw_ref)):
        m_ref[...] = jnp.full_like(m_ref, NEG)
        l_ref[...] = jnp.zeros_like(l_ref)
        acc_ref[...] = jnp.zeros_like(acc_ref)

    def online_update(s, v_tile0, m_ref, l_ref, acc_ref):
        m_prev = m_ref[...]
        m_new = jnp.maximum(m_prev, jnp.max(s, axis=0, keepdims=True))
        p = jnp.exp(s - m_new)
        alpha = jnp.exp(m_prev - m_new)
        l_ref[...] = alpha * l_ref[...] + jnp.sum(p, axis=0, keepdims=True)
        pb = p.astype(BF16)
        pv = _dot(vt_ref[0, v_tile0, 0], pb[0:V_TILE])
        for c in range(1, s.shape[0] // V_TILE):
            pv = pv + _dot(vt_ref[0, v_tile0 + c, 0], pb[c * V_TILE:(c + 1) * V_TILE])
        acc_ref[...] = alpha * acc_ref[...] + pv
        m_ref[...] = m_new

    def key_pos(k0, size):
        return k0 + lax.broadcasted_iota(jnp.int32, (size, 1), 0)

    def sel_tile(k0, size, causal):
        s = _dot(kaug_ref[pl.ds(k0, size), :], qaug_ref[...])
        if causal:
            s = jnp.where(key_pos(k0, size) <= cpos, s, NEG)
        online_update(s, k0 // V_TILE, ms_ref, ls_ref, accs_ref)

    def sel_body(j, carry):
        sel_tile(pl.multiple_of(j * tk, tk), tk, False)
        return carry

    lax.fori_loop(0, t0 // tk, sel_body, 0)

    @pl.when(t0 % tk != 0)
    def _():
        sel_tile(pl.multiple_of(t0 - tq, tq), tq, False)

    t0a = pl.multiple_of(t0, tq)
    sel_tile(t0a, tq, True)

    def win_tile(k0, mode):
        s = _dot(kaug_ref[pl.ds(k0, tq), 0:LANES], qwin_ref[...])
        if mode == "diagonal":
            s = jnp.where(key_pos(k0, tq) <= cpos, s, NEG)
        elif mode == "oldest":
            s = jnp.where(cpos - key_pos(k0, tq) < WINDOW, s, NEG)
        online_update(s, k0 // V_TILE, mw_ref, lw_ref, accw_ref)

    win_tile(t0a, "diagonal")

    @pl.when(qi >= 1)
    def _():
        win_tile(pl.multiple_of(t0 - tq, tq), "inside")

    @pl.when(qi >= 2)
    def _():
        win_tile(pl.multiple_of(t0 - 2 * tq, tq), "oldest")

    o_s = accs_ref[0:HEAD_DIM, :] / ls_ref[...]
    o_w = accw_ref[HEAD_DIM:, :] / lw_ref[...]
    o_c = oc_ref[HEAD_DIM:, :]
    sig = _sigmoid(gate_ref[...])
    heads = []
    for r in range(GROUP):
        cs = slice(r * tq, (r + 1) * tq)
        g_c, g_s, g_w = (sig[3 * r + b:3 * r + b + 1, :] for b in range(3))
        heads.append(g_c * o_c[:, cs] + g_s * o_s[:, cs] + g_w * o_w[:, cs])
    o_ref[...] = jnp.concatenate(heads, axis=0).T.astype(o_ref.dtype)


def _attention(qgt, kaug, vt, kvc, kvct, cmapt, qg, cos, sin, batch, seq):
    t = qgt.shape[1]
    tq = min(256, seq)
    tk = min(512, seq)
    nq = seq // tq
    nc = kvc.shape[1]
    assert seq % tk == 0 and tk in (tq, 2 * tq) and WINDOW == 2 * tq and tq == V_TILE
    cols = GROUP * tq
    hq = GROUP * HEAD_DIM
    kern = functools.partial(_attn_kernel, tq=tq, tk=tk, k_top=min(SEL_TOPK, seq // SEL_LEN))
    tab = lambda: pl.BlockSpec((ROT_DIM // 2, tq), lambda b, g, qi: (0, qi))
    stat = lambda: pltpu.VMEM((1, cols), F32)
    acc = lambda: pltpu.VMEM((LANES, cols), F32)
    return pl.pallas_call(
        kern,
        out_shape=jax.ShapeDtypeStruct((t, N_HEADS * HEAD_DIM), BF16),
        grid=(batch, N_KV, nq),
        in_specs=[
            pl.BlockSpec((hq, tq), lambda b, g, qi: (g, b * nq + qi)),
            pl.BlockSpec((GATE_ROWS, tq), lambda b, g, qi: (N_HEADS * HEAD_DIM // GATE_ROWS + g, b * nq + qi)),
            pl.BlockSpec((seq, 2 * LANES), lambda b, g, qi: (b, g)),
            pl.BlockSpec((1, seq // V_TILE, 1, LANES, V_TILE), lambda b, g, qi: (b, 0, g, 0, 0)),
            pl.BlockSpec((1, nc, LANES), lambda b, g, qi: (b * N_KV + g, 0, 0)),
            pl.BlockSpec((1, LANES, nc), lambda b, g, qi: (b * N_KV + g, 0, 0)),
            pl.BlockSpec((LANES, nc), lambda b, g, qi: (0, 0)),
            pl.BlockSpec((HEAD_DIM, tq), lambda b, g, qi: (0, 0)),
            tab(), tab(),
        ],
        out_specs=pl.BlockSpec((tq, hq), lambda b, g, qi: (b * nq + qi, g)),
        scratch_shapes=[
            pltpu.VMEM((2 * LANES, cols), BF16), pltpu.VMEM((LANES, cols), BF16), acc(),
            stat(), stat(), acc(), stat(), stat(), acc(),
        ],
        compiler_params=_params("parallel", "parallel", "arbitrary"),
        name="nsa_attention",
    )(qgt, qgt, kaug, vt, kvc, kvct, cmapt, qg, cos, sin)


def _rope_angles(seq):
    half = ROT_DIM // 2
    inv_freq = ROPE_THETA ** (-jnp.arange(half, dtype=F32) * (2.0 / ROT_DIM))
    ang = jnp.arange(seq, dtype=F32)[:, None] * inv_freq[None, :]
    return jnp.cos(ang), jnp.sin(ang)


def _rope_tables(cos, sin):
    seq, half = cos.shape
    zeros = jnp.zeros((seq, HEAD_DIM - ROT_DIM), F32)
    zh = jnp.zeros((seq, half), F32)
    c = jnp.concatenate([cos, cos, zeros + 1.0], axis=1)
    s1 = jnp.concatenate([-sin, zh, zeros], axis=1)
    s2 = jnp.concatenate([zh, sin, zeros], axis=1)
    reps = LANES // HEAD_DIM
    return tuple(jnp.tile(a, (1, reps)) for a in (c, s1, s2))


def _cmp_to_sel_t(nc, n_sel):
    start_c = np.arange(nc)[None, :] * CMP_STRIDE
    start_s = np.arange(LANES)[:, None] * SEL_LEN
    ov = np.minimum(start_c + CMP_LEN, start_s + SEL_LEN) - np.maximum(start_c, start_s)
    m = np.maximum(ov, 0).astype(np.float32) / CMP_LEN
    m[n_sel:, :] = 0.0
    m[:, nc - 1:] = 0.0
    return jnp.asarray(m, BF16)


def _nsa_in_weights(w_in):
    d = w_in.shape[0]
    hq, hk = N_HEADS * HEAD_DIM, N_KV * HEAD_DIM
    offs = np.cumsum([0, hq] + [hk] * 6)
    q, kc, vc, ks, vs, kw, vw = (w_in[:, offs[n]:offs[n + 1]] for n in range(7))
    gl = w_in[:, offs[7]:]
    per_group = lambda a, b: jnp.stack([a.reshape(d, N_KV, HEAD_DIM), b.reshape(d, N_KV, HEAD_DIM)],
                                       axis=2).reshape(d, N_KV * LANES)
    gates = jnp.pad(gl.reshape(d, N_KV, 3 * GROUP), ((0, 0), (0, 0), (0, GATE_ROWS - 3 * GROUP)))
    wn = jnp.concatenate([per_group(ks, kw), kc, vc], axis=1).astype(BF16)
    wt = jnp.concatenate([q, gates.reshape(d, N_KV * GATE_ROWS), per_group(vs, vw)], axis=1).T.astype(BF16)
    return wn, wt


def _nsa_layer(x, g, w_in, q_g, kc_g, ks_g, kw_g, pe_k, pe_v, ck_w1, ck_w2, cv_w1, cv_w2, w_out, batch, seq):
    nc = seq // CMP_STRIDE
    n_sel = seq // SEL_LEN
    assert n_sel <= LANES and seq % CMP_STRIDE == 0 and 3 * GROUP <= GATE_ROWS
    wn, wt = _nsa_in_weights(w_in)
    proj, qgt, vt = _in_proj(x, g, wn, wt, batch, seq)
    cos, sin = _rope_angles(seq)
    seg = np.arange(LANES) // HEAD_DIM
    bd = jnp.asarray(seg[:, None] == seg[None, :], BF16)
    kaug = _kprep(proj, jnp.concatenate([ks_g, kw_g]).reshape(1, LANES), bd, _rope_tables(cos, sin), seq)
    hk = N_KV * HEAD_DIM
    c0 = N_KV * LANES
    chunks = lambda cols: (cols.reshape(batch, seq, N_KV, HEAD_DIM).transpose(0, 2, 1, 3)
                           .reshape(batch * N_KV, nc, CMP_STRIDE * HEAD_DIM))
    kvc, kvct = _compress(chunks(proj[:, c0:c0 + hk]), chunks(proj[:, c0 + hk:c0 + 2 * hk]),
                          pe_k, pe_v, ck_w1, ck_w2, cv_w1, cv_w2, kc_g)
    tq = min(256, seq)
    qg = jnp.broadcast_to((q_g * HEAD_DIM ** -0.5)[:, None], (HEAD_DIM, tq))
    attn = _attention(qgt, kaug, vt, kvc, kvct, _cmp_to_sel_t(nc, n_sel), qg, cos.T, sin.T, batch, seq)
    return _out_proj(attn, x, w_out)


def kernel(x, mix_norm_g, ffn_norm_g, conv_w_pw1, conv_b_pw1, conv_w_dw, conv_b_dw, conv_ln_g, conv_ln_b, conv_w_pw2, conv_b_pw2, nsa_w_in, nsa_q_norm, nsa_kc_norm, nsa_ks_norm, nsa_kw_norm, nsa_pe_k, nsa_pe_v, nsa_ck_w1, nsa_ck_w2, nsa_cv_w1, nsa_cv_w2, nsa_w_out, ffn_w_up, ffn_w_dw, ffn_b_dw, ffn_w_down):
    batch, seq, d = x.shape
    depth = mix_norm_g.shape[0]
    n_mixers = 2
    h = x.reshape(batch * seq, d)
    for i in range(depth):
        j = i // n_mixers
        if i % n_mixers == 0:
            u = _conformer_glu(h, mix_norm_g[i], conv_w_pw1[j], conv_b_pw1[j])
            h = _conformer_conv_out(u, h, conv_w_dw[j], conv_b_dw[j], conv_ln_g[j], conv_ln_b[j],
                                    conv_w_pw2[j], conv_b_pw2[j], seq)
        else:
            h = _nsa_layer(h, mix_norm_g[i], nsa_w_in[j], nsa_q_norm[j], nsa_kc_norm[j], nsa_ks_norm[j],
                           nsa_kw_norm[j], nsa_pe_k[j], nsa_pe_v[j], nsa_ck_w1[j], nsa_ck_w2[j], nsa_cv_w1[j],
                           nsa_cv_w2[j], nsa_w_out[j], batch, seq)
        h = _conv_ffn(h, ffn_norm_g[i], ffn_w_up[i], ffn_w_dw[i], ffn_b_dw[i], ffn_w_down[i], seq)
    return h.reshape(batch, seq, d)
```

```python
import functools

import numpy as np
import jax
import jax.numpy as jnp
from jax import lax
from jax.experimental import pallas as pl
from jax.experimental.pallas import tpu as pltpu

N_HEADS = 16
HEAD_DIM = 64
N_KV = 4
GROUP = N_HEADS // N_KV
ROT_DIM = HEAD_DIM // 4
ROPE_THETA = 500000.0
CMP_LEN = 32
CMP_STRIDE = 16
SEL_LEN = 64
SEL_TOPK = 16
WINDOW = 512
EPS = 1e-6
NEG = -1e30

LANES = 128
SUBLANES = 8
BF16_ROWS = 16
V_TILE = 256
GATE_ROWS = 16
VMEM_LIMIT = 48 * 1024 * 1024

F32 = jnp.float32
BF16 = jnp.bfloat16
NT_DIMS = (((1,), (1,)), ((), ()))


def _params(*sem):
    return pltpu.CompilerParams(dimension_semantics=sem, vmem_limit_bytes=VMEM_LIMIT)


def _dot(a, b):
    return jnp.dot(a, b, preferred_element_type=F32)


def _dot_nt(a, b):
    return lax.dot_general(a, b, NT_DIMS, preferred_element_type=F32)


def _split_bf16(x):
    hi = x.astype(BF16)
    lo = (x - hi.astype(F32)).astype(BF16)
    return hi, lo


def _rms_rows(x, g):
    ms = jnp.mean(x * x, axis=-1, keepdims=True)
    return x * lax.rsqrt(ms + EPS) * g


def _sigmoid(x):
    return 1.0 / (1.0 + jnp.exp(-x))


def _head_rms(xb, bd, gain):
    hi, lo = _split_bf16(xb * xb)
    ss = _dot(hi, bd) + _dot(lo, bd)
    return xb * lax.rsqrt(ss * (1.0 / HEAD_DIM) + EPS) * gain


def _rope(xb, c, s1, s2):
    half = ROT_DIM // 2
    return xb * c + pltpu.roll(xb, LANES - half, 1) * s1 + pltpu.roll(xb, half, 1) * s2


def _ffn_kernel(x_ref, halo_ref, g_ref, wa_ref, wv_ref, wdw_ref, bdw_ref, wd_ref, o_ref,
                hn_ref, a_ref, acc_ref, *, tm, tiles_per_seq):
    i = pl.program_id(0)
    c = pl.program_id(1)
    hl = BF16_ROWS

    @pl.when(c == 0)
    def _():
        g = g_ref[...]
        hn_ref[hl:, :] = _rms_rows(x_ref[...], g).astype(BF16)
        first = (i % tiles_per_seq) == 0
        hn_ref[0:hl, :] = jnp.where(first, 0.0, _rms_rows(halo_ref[...], g)).astype(BF16)
        acc_ref[...] = jnp.zeros_like(acc_ref)

    th = tm // 2
    w = wdw_ref[...]
    a_ref[0:hl + th, :] = _dot(hn_ref[0:hl + th, :], wa_ref[...])
    v_halves = [_dot(hn_ref[hl:hl + th, :], wv_ref[...])]
    a_ref[hl + th:, :] = _dot(hn_ref[hl + th:, :], wa_ref[...])
    v_halves.append(_dot(hn_ref[hl + th:, :], wv_ref[...]))
    for h, v in enumerate(v_halves):
        r0 = hl + h * th
        cv = (w[0:1] * a_ref[r0 - 2:r0 - 2 + th, :] + w[1:2] * a_ref[r0 - 1:r0 - 1 + th, :]
              + w[2:3] * a_ref[r0:r0 + th, :] + bdw_ref[...])
        act = (cv * _sigmoid(cv) * v).astype(BF16)
        acc_ref[h * th:(h + 1) * th, :] += _dot(act, wd_ref[...])

    @pl.when(c == pl.num_programs(1) - 1)
    def _():
        o_ref[...] = x_ref[...] + acc_ref[...]


def _conv_ffn(x, g, w_up, w_dw, b_dw, w_down, seq):
    t, d = x.shape
    dff = w_down.shape[0]
    tm = min(1024, seq)
    tf = 256
    assert seq % tm == 0 and dff % tf == 0 and w_dw.shape[0] == 3
    nch = dff // tf
    hl = BF16_ROWS
    kern = functools.partial(_ffn_kernel, tm=tm, tiles_per_seq=seq // tm)
    return pl.pallas_call(
        kern,
        out_shape=jax.ShapeDtypeStruct((t, d), F32),
        grid=(t // tm, nch),
        in_specs=[
            pl.BlockSpec((tm, d), lambda i, c: (i, 0)),
            pl.BlockSpec((hl, d), lambda i, c: (jnp.maximum(i * (tm // hl) - 1, 0), 0)),
            pl.BlockSpec((1, d), lambda i, c: (0, 0)),
            pl.BlockSpec((d, tf), lambda i, c: (0, c)),
            pl.BlockSpec((d, tf), lambda i, c: (0, c + nch)),
            pl.BlockSpec((3, tf), lambda i, c: (0, c)),
            pl.BlockSpec((1, tf), lambda i, c: (0, c)),
            pl.BlockSpec((tf, d), lambda i, c: (c, 0)),
        ],
        out_specs=pl.BlockSpec((tm, d), lambda i, c: (i, 0)),
        scratch_shapes=[
            pltpu.VMEM((tm + hl, d), BF16),
            pltpu.VMEM((tm + hl, tf), F32),
            pltpu.VMEM((tm, d), F32),
        ],
        compiler_params=_params("parallel", "arbitrary"),
        name="conv_ffn",
    )(x, x, g.reshape(1, d), w_up.astype(BF16), w_up.astype(BF16), w_dw, b_dw.reshape(1, dff),
      w_down.astype(BF16))


def _glu_kernel(x_ref, g_ref, wa_ref, wg_ref, ba_ref, bg_ref, o_ref, hn_ref):
    @pl.when(pl.program_id(1) == 0)
    def _():
        hn_ref[...] = _rms_rows(x_ref[...], g_ref[...]).astype(BF16)

    hn = hn_ref[...]
    a = _dot(hn, wa_ref[...]) + ba_ref[...]
    gate = _dot(hn, wg_ref[...]) + bg_ref[...]
    o_ref[...] = a * _sigmoid(gate)


def _conformer_glu(x, g, w_pw1, b_pw1):
    t, d = x.shape
    tm = min(1024, t)
    tn = 256
    nj = d // tn
    w = w_pw1.astype(BF16)
    b = b_pw1.reshape(1, 2 * d)
    return pl.pallas_call(
        _glu_kernel,
        out_shape=jax.ShapeDtypeStruct((t, d), F32),
        grid=(t // tm, nj),
        in_specs=[
            pl.BlockSpec((tm, d), lambda i, j: (i, 0)),
            pl.BlockSpec((1, d), lambda i, j: (0, 0)),
            pl.BlockSpec((d, tn), lambda i, j: (0, j)),
            pl.BlockSpec((d, tn), lambda i, j: (0, j + nj)),
            pl.BlockSpec((1, tn), lambda i, j: (0, j)),
            pl.BlockSpec((1, tn), lambda i, j: (0, j + nj)),
        ],
        out_specs=pl.BlockSpec((tm, tn), lambda i, j: (i, j)),
        scratch_shapes=[pltpu.VMEM((tm, d), BF16)],
        compiler_params=_params("parallel", "arbitrary"),
        name="conformer_glu",
    )(x, g.reshape(1, d), w, w, b, b)


def _dwconv_kernel(u_ref, halo_ref, x_ref, wdw_ref, bdw_ref, lng_ref, lnb_ref, w2_ref, b2_ref, o_ref,
                   ext_ref, cv_ref, slab_ref, *, tm, halo, width, tiles_per_seq, row_chunk, col_chunk):
    i = pl.program_id(0)
    d = u_ref.shape[1]
    first = (i % tiles_per_seq) == 0
    ext_ref[0:halo, :] = jnp.where(first, 0.0, halo_ref[...])
    ext_ref[halo:, :] = u_ref[...]
    off = halo - (width - 1)
    for r0 in range(0, tm, row_chunk):
        for c0 in range(0, d, col_chunk):
            cs = slice(c0, c0 + col_chunk)
            acc = jnp.broadcast_to(bdw_ref[:, cs], (row_chunk, col_chunk))
            for r in range(min(SUBLANES, width)):
                taps = range(r, width, SUBLANES)
                lo = r0 + off + r
                rows = row_chunk + (len(taps) - 1) * SUBLANES
                slab_ref[r, 0:rows, :] = ext_ref[lo:lo + rows, cs]
                for q, k in enumerate(taps):
                    acc = acc + wdw_ref[k:k + 1, cs] * slab_ref[r, q * SUBLANES:q * SUBLANES + row_chunk, :]
            cv_ref[r0:r0 + row_chunk, cs] = acc
    u = cv_ref[...]
    mu = jnp.mean(u, axis=-1, keepdims=True)
    uc = u - mu
    var = jnp.mean(uc * uc, axis=-1, keepdims=True)
    y = uc * lax.rsqrt(var + EPS) * lng_ref[...] + lnb_ref[...]
    s = (y * _sigmoid(y)).astype(BF16)
    o_ref[...] = x_ref[...] + _dot(s, w2_ref[...]) + b2_ref[...]


def _conformer_conv_out(u, x, w_dw, b_dw, ln_g, ln_b, w_pw2, b_pw2, seq):
    t, d = x.shape
    width = w_dw.shape[0]
    halo = 32
    assert width - 1 <= halo
    tm = min(256, seq)
    assert seq % tm == 0 and tm % halo == 0
    row_chunk, col_chunk = 64, 256
    kern = functools.partial(_dwconv_kernel, tm=tm, halo=halo, width=width, tiles_per_seq=seq // tm,
                             row_chunk=row_chunk, col_chunk=col_chunk)
    slab_rows = row_chunk + (width - 1) // SUBLANES * SUBLANES
    vec = lambda: pl.BlockSpec((1, d), lambda i: (0, 0))
    return pl.pallas_call(
        kern,
        out_shape=jax.ShapeDtypeStruct((t, d), F32),
        grid=(t // tm,),
        in_specs=[
            pl.BlockSpec((tm, d), lambda i: (i, 0)),
            pl.BlockSpec((halo, d), lambda i: (jnp.maximum(i * (tm // halo) - 1, 0), 0)),
            pl.BlockSpec((tm, d), lambda i: (i, 0)),
            pl.BlockSpec((width, d), lambda i: (0, 0)),
            vec(), vec(), vec(),
            pl.BlockSpec((d, d), lambda i: (0, 0)),
            vec(),
        ],
        out_specs=pl.BlockSpec((tm, d), lambda i: (i, 0)),
        scratch_shapes=[pltpu.VMEM((tm + halo, d), F32), pltpu.VMEM((tm, d), F32),
                        pltpu.VMEM((SUBLANES, slab_rows, col_chunk), F32)],
        compiler_params=_params("parallel"),
        name="conformer_dwconv_out",
    )(u, u, x, w_dw, b_dw.reshape(1, d), ln_g.reshape(1, d), ln_b.reshape(1, d), w_pw2.astype(BF16),
      b_pw2.reshape(1, d))


def _in_proj_kernel(x_ref, g_ref, wn_ref, wt_ref, on_ref, oq_ref, ov_ref, *, n_qg, row_chunk):
    hn = _rms_rows(x_ref[...], g_ref[...]).astype(BF16)
    on_ref[...] = _dot(hn, wn_ref[...])
    for r0 in range(0, n_qg, row_chunk):
        r1 = min(r0 + row_chunk, n_qg)
        oq_ref[r0:r1, :] = _dot_nt(wt_ref[r0:r1, :], hn)
    vt = _dot_nt(wt_ref[n_qg:, :], hn).astype(BF16)
    for jj in range(ov_ref.shape[1]):
        for g in range(N_KV):
            ov_ref[0, jj, g] = vt[g * LANES:(g + 1) * LANES, jj * V_TILE:(jj + 1) * V_TILE]


def _in_proj(x, g, wn, wt, batch, seq):
    t, d = x.shape
    tm = min(512, seq)
    n_qg = N_HEADS * HEAD_DIM + N_KV * GATE_ROWS
    assert seq % tm == 0 and tm % V_TILE == 0 and wt.shape[0] == n_qg + N_KV * LANES
    per_seq = seq // tm
    kern = functools.partial(_in_proj_kernel, n_qg=n_qg, row_chunk=512)
    return pl.pallas_call(
        kern,
        out_shape=(jax.ShapeDtypeStruct((t, wn.shape[1]), F32),
                   jax.ShapeDtypeStruct((n_qg, t), F32),
                   jax.ShapeDtypeStruct((batch, seq // V_TILE, N_KV, LANES, V_TILE), BF16)),
        grid=(t // tm,),
        in_specs=[
            pl.BlockSpec((tm, d), lambda i: (i, 0)),
            pl.BlockSpec((1, d), lambda i: (0, 0)),
            pl.BlockSpec(wn.shape, lambda i: (0, 0)),
            pl.BlockSpec(wt.shape, lambda i: (0, 0)),
        ],
        out_specs=(pl.BlockSpec((tm, wn.shape[1]), lambda i: (i, 0)),
                   pl.BlockSpec((n_qg, tm), lambda i: (0, i)),
                   pl.BlockSpec((1, tm // V_TILE, N_KV, LANES, V_TILE),
                                lambda i: (i // per_seq, i % per_seq, 0, 0, 0))),
        compiler_params=_params("parallel"),
        name="nsa_in_proj",
    )(x, g.reshape(1, d), wn, wt)


def _out_proj_kernel(a_ref, x_ref, w_ref, o_ref):
    o_ref[...] = x_ref[...] + _dot(a_ref[...], w_ref[...])


def _out_proj(a, x, w):
    t, d = x.shape
    tm = min(512, t)
    return pl.pallas_call(
        _out_proj_kernel,
        out_shape=jax.ShapeDtypeStruct((t, d), F32),
        grid=(t // tm,),
        in_specs=[
            pl.BlockSpec((tm, a.shape[1]), lambda i: (i, 0)),
            pl.BlockSpec((tm, d), lambda i: (i, 0)),
            pl.BlockSpec(w.shape, lambda i: (0, 0)),
        ],
        out_specs=pl.BlockSpec((tm, d), lambda i: (i, 0)),
        compiler_params=_params("parallel"),
        name="nsa_out_proj",
    )(a, x, w.astype(BF16))


def _kprep_kernel(k_ref, gain_ref, bd_ref, c_ref, s1_ref, s2_ref, kaug_ref, *, ts, seq):
    i = pl.program_id(0)
    bd = bd_ref[...]
    gain = gain_ref[...]
    c, s1, s2 = c_ref[...], s1_ref[...], s2_ref[...]
    tpos = (i * ts) % seq + lax.broadcasted_iota(jnp.int32, (ts, LANES), 0)
    lane = lax.broadcasted_iota(jnp.int32, (ts, LANES), 1)
    onehot = jnp.where(tpos // SEL_LEN == lane, 1.0, 0.0).astype(BF16)
    for g in range(N_KV):
        xb = k_ref[:, g * LANES:(g + 1) * LANES]
        xr = _rope(_head_rms(xb, bd, gain), c, s1, s2)
        kaug_ref[:, 2 * g * LANES:(2 * g + 1) * LANES] = xr.astype(BF16)
        kaug_ref[:, (2 * g + 1) * LANES:(2 * g + 2) * LANES] = onehot


def _kprep(proj, gain, bd, tabs, seq):
    t = proj.shape[0]
    ts = min(512, seq)
    assert seq % ts == 0
    nk = N_KV * LANES
    kern = functools.partial(_kprep_kernel, ts=ts, seq=seq)
    tab = lambda: pl.BlockSpec((ts, LANES), lambda i: (i % (seq // ts), 0))
    return pl.pallas_call(
        kern,
        out_shape=jax.ShapeDtypeStruct((t, 2 * nk), BF16),
        grid=(t // ts,),
        in_specs=[
            pl.BlockSpec((ts, nk), lambda i: (i, 0)),
            pl.BlockSpec((1, LANES), lambda i: (0, 0)),
            pl.BlockSpec((LANES, LANES), lambda i: (0, 0)),
            tab(), tab(), tab(),
        ],
        out_specs=pl.BlockSpec((ts, 2 * nk), lambda i: (i, 0)),
        compiler_params=_params("parallel"),
        name="nsa_key_prep",
    )(proj, gain, bd, *tabs)


def _compress_kernel(kc_ref, vc_ref, pek_ref, pev_ref, w1k_ref, w1v_ref, w2k_ref, w2v_ref, gain_ref, o_ref, ot_ref):
    nc = kc_ref.shape[1]

    def hidden(c_ref, pe_ref, w1_ref):
        x = c_ref[0]
        pe = pe_ref[...]
        first = _dot((x + pe[0:1]).astype(BF16), w1_ref[0])
        second = _dot((x + pe[1:2]).astype(BF16), w1_ref[1])
        pre = first + pltpu.roll(second, nc - 1, 0)
        return (pre * _sigmoid(pre)).astype(BF16)

    kv = _dot(hidden(kc_ref, pek_ref, w1k_ref), w2k_ref[...]) + _dot(hidden(vc_ref, pev_ref, w1v_ref), w2v_ref[...])
    is_k = lax.broadcasted_iota(jnp.int32, kv.shape, 1) < HEAD_DIM
    ss = jnp.sum(jnp.where(is_k, kv * kv, 0.0), axis=-1, keepdims=True)
    kn = kv * lax.rsqrt(ss * (1.0 / HEAD_DIM) + EPS) * gain_ref[...]
    out = jnp.where(is_k, kn, kv)
    o_ref[0] = out.astype(BF16)
    ot_ref[0] = out.T.astype(BF16)


def _compress(kc, vc, pe_k, pe_v, ck_w1, ck_w2, cv_w1, cv_w2, kc_g):
    bg, nc, cw = kc.shape
    hid = ck_w1.shape[1]
    half = CMP_STRIDE * HEAD_DIM
    assert cw == half and CMP_LEN == 2 * CMP_STRIDE
    pad = lambda w, left: jnp.pad(w, ((0, 0), (HEAD_DIM, 0) if left else (0, HEAD_DIM))).astype(BF16)
    gain = jnp.concatenate([kc_g, jnp.ones((HEAD_DIM,), F32)]).reshape(1, LANES)
    chunk = lambda: pl.BlockSpec((1, nc, cw), lambda i: (i, 0, 0))
    full = lambda shape: pl.BlockSpec(shape, lambda i: (0,) * len(shape))
    return pl.pallas_call(
        _compress_kernel,
        out_shape=(jax.ShapeDtypeStruct((bg, nc, LANES), BF16), jax.ShapeDtypeStruct((bg, LANES, nc), BF16)),
        grid=(bg,),
        in_specs=[chunk(), chunk(), full((2, half)), full((2, half)), full((2, half, hid)), full((2, half, hid)),
                  full((hid, LANES)), full((hid, LANES)), full((1, LANES))],
        out_specs=(pl.BlockSpec((1, nc, LANES), lambda i: (i, 0, 0)), pl.BlockSpec((1, LANES, nc), lambda i: (i, 0, 0))),
        compiler_params=_params("parallel"),
        name="nsa_compress",
    )(kc, vc, pe_k.reshape(2, half), pe_v.reshape(2, half), ck_w1.reshape(2, half, hid).astype(BF16),
      cv_w1.reshape(2, half, hid).astype(BF16), pad(ck_w2, False), pad(cv_w2, True), gain)


def _attn_kernel(qt_ref, gate_ref, kaug_ref, vt_ref, kvc_ref, kvct_ref, cmapt_ref, qg_ref, cos_ref, sin_ref,
                 o_ref,
                 qaug_ref, qwin_ref, oc_ref, ms_ref, ls_ref, accs_ref, mw_ref, lw_ref, accw_ref,
                 sa_ref, sb_ref, wdiag_ref, wold_ref,
                 *, tq, tk, k_top):
    qi = pl.program_id(2)
    t0 = qi * tq
    cols = GROUP * tq
    half = ROT_DIM // 2

    gain = qg_ref[...]
    cos, sin = cos_ref[...], sin_ref[...]
    nope, rope = [], []
    for r in range(GROUP):
        x = qt_ref[r * HEAD_DIM:(r + 1) * HEAD_DIM, :]
        ss = jnp.sum(x * x, axis=0, keepdims=True)
        xn = x * lax.rsqrt(ss * (1.0 / HEAD_DIM) + EPS) * gain
        x1, x2 = xn[0:half], xn[half:ROT_DIM]
        nope.append(xn)
        rope.append(jnp.concatenate([x1 * cos - x2 * sin, x2 * cos + x1 * sin, xn[ROT_DIM:]], axis=0))
    zeros = jnp.zeros((HEAD_DIM, cols), BF16)
    q_nope = jnp.concatenate([jnp.concatenate(nope, axis=1).astype(BF16), zeros], axis=0)
    q_rope = jnp.concatenate(rope, axis=1).astype(BF16)
    qwin_ref[...] = jnp.concatenate([zeros, q_rope], axis=0)

    tpos = t0 + lax.broadcasted_iota(jnp.int32, (1, tq), 1)

    kvc = kvc_ref[0]
    nc = kvc.shape[0]
    cmp_end = lax.broadcasted_iota(jnp.int32, (nc, 1), 0) * CMP_STRIDE + (CMP_LEN - 1)
    valid = cmp_end <= tpos
    p_sum = None
    for r in range(GROUP):
        cs = slice(r * tq, (r + 1) * tq)
        s_c = jnp.where(valid, _dot(kvc, q_nope[:, cs]), NEG)
        e = jnp.where(valid, jnp.exp2(s_c - jnp.max(s_c, axis=0, keepdims=True)), 0.0)
        p_c = e / jnp.maximum(jnp.sum(e, axis=0, keepdims=True), 1e-30)
        oc_ref[:, cs] = _dot(kvct_ref[0], p_c.astype(BF16))
        p_sum = p_c if p_sum is None else p_sum + p_c

    hi, lo = _split_bf16(p_sum)
    imp = _dot(cmapt_ref[...], hi) + _dot(cmapt_ref[...], lo)
    blk = lax.broadcasted_iota(jnp.int32, (LANES, tq), 0)
    cur = tpos // SEL_LEN
    forced = (blk == 0) | (blk == cur) | (blk == cur - 1)
    imp = jnp.where(blk > cur, -1.0, jnp.where(forced, 1e4, imp))
    blk_f = blk.astype(F32)
    sel = jnp.zeros((LANES, tq), F32)
    for _ in range(k_top):
        mx = jnp.max(imp, axis=0, keepdims=True)
        idx = jnp.min(jnp.where(imp == mx, blk_f, float(LANES)), axis=0, keepdims=True)
        pick = blk_f == idx
        sel = jnp.where(pick, 1.0, sel)
        imp = jnp.where(pick, -3e38, imp)
    bias = jnp.where(sel > 0.0, 0.0, NEG).astype(BF16)
    qaug_ref[...] = jnp.concatenate([q_rope, zeros, jnp.concatenate([bias] * GROUP, axis=1)], axis=0)

    for m_ref, l_ref, acc_ref in ((ms_ref, ls_ref, accs_ref), (mw_ref, lw_ref, accw_ref)):
        m_ref[...] = jnp.full_like(m_ref, NEG)
        l_ref[...] = jnp.zeros_like(l_ref)
        acc_ref[...] = jnp.zeros_like(acc_ref)

    def fold_tile(s, keep, k0, m_ref, l_ref, acc_ref):
        if keep is not None:
            s = jnp.where(jnp.concatenate([keep] * GROUP, axis=1), s, NEG)
        m_prev = m_ref[...]
        m_new = jnp.maximum(m_prev, jnp.max(s, axis=0, keepdims=True))
        p = jnp.exp2(s - m_new)
        alpha = jnp.exp2(m_prev - m_new)
        l_ref[...] = alpha * l_ref[...] + jnp.sum(p, axis=0, keepdims=True)
        pb = p.astype(BF16)
        v_tile0 = k0 // V_TILE
        pv = _dot(vt_ref[0, v_tile0, 0], pb[0:V_TILE])
        for c in range(1, s.shape[0] // V_TILE):
            pv = pv + _dot(vt_ref[0, v_tile0 + c, 0], pb[c * V_TILE:(c + 1) * V_TILE])
        acc_ref[...] = alpha * acc_ref[...] + pv
        m_ref[...] = m_new

    def back(k0, size):
        return tpos - (k0 + lax.broadcasted_iota(jnp.int32, (size, 1), 0))

    def sel_scores(k0):
        return _dot(kaug_ref[pl.ds(k0, tk), :], qaug_ref[...])

    def win_scores(k0, size):
        return _dot(kaug_ref[pl.ds(k0, size), 0:LANES], qwin_ref[...])

    n = t0 // tk
    sel_state = (ms_ref, ls_ref, accs_ref)
    sa_ref[...] = sel_scores(0)

    def pair_body(i, carry):
        k0 = pl.multiple_of(2 * i * tk, tk)
        sb_ref[...] = sel_scores(k0 + tk)
        fold_tile(sa_ref[...], None, k0, *sel_state)
        sa_ref[...] = sel_scores(k0 + 2 * tk)
        fold_tile(sb_ref[...], None, k0 + tk, *sel_state)
        return carry

    lax.fori_loop(0, n // 2, pair_body, 0)

    t0a = pl.multiple_of(t0, tq)
    k_diag = pl.multiple_of(n * tk, tk)
    k_old = pl.multiple_of(jnp.maximum(t0 - WINDOW, 0), tq)

    @pl.when(n % 2 == 1)
    def _():
        sb_ref[...] = sel_scores(k_diag)
        fold_tile(sa_ref[...], None, k_diag - tk, *sel_state)
        wdiag_ref[...] = win_scores(t0a, tq)
        fold_tile(sb_ref[...], back(k_diag, tk) >= 0, k_diag, *sel_state)

    @pl.when(n % 2 == 0)
    def _():
        wdiag_ref[...] = win_scores(t0a, tq)
        fold_tile(sa_ref[...], back(k_diag, tk) >= 0, k_diag, *sel_state)

    wold_ref[...] = win_scores(k_old, WINDOW)
    fold_tile(wdiag_ref[...], back(t0a, tq) >= 0, t0a, mw_ref, lw_ref, accw_ref)
    b_old = back(k_old, WINDOW)
    fold_tile(wold_ref[...], (b_old < WINDOW) & (b_old > tpos - t0), k_old, mw_ref, lw_ref, accw_ref)

    o_s = accs_ref[0:HEAD_DIM, :] / ls_ref[...]
    o_w = accw_ref[HEAD_DIM:, :] / lw_ref[...]
    o_c = oc_ref[HEAD_DIM:, :]
    sig = _sigmoid(gate_ref[...])
    heads = []
    for r in range(GROUP):
        cs = slice(r * tq, (r + 1) * tq)
        g_c, g_s, g_w = (sig[3 * r + b:3 * r + b + 1, :] for b in range(3))
        heads.append(g_c * o_c[:, cs] + g_s * o_s[:, cs] + g_w * o_w[:, cs])
    o_ref[...] = jnp.concatenate(heads, axis=0).T.astype(o_ref.dtype)


def _attention(qgt, kaug, vt, kvc, kvct, cmapt, qg, cos, sin, batch, seq):
    t = qgt.shape[1]
    tq = min(256, seq)
    tk = min(512, seq)
    nq = seq // tq
    nc = kvc.shape[1]
    assert seq % tk == 0 and tk % tq == 0 and WINDOW % tq == 0 and tq == V_TILE and seq >= WINDOW
    cols = GROUP * tq
    hq = GROUP * HEAD_DIM
    kern = functools.partial(_attn_kernel, tq=tq, tk=tk, k_top=min(SEL_TOPK, seq // SEL_LEN))
    tab = lambda: pl.BlockSpec((ROT_DIM // 2, tq), lambda b, g, qi: (0, qi))
    stat = lambda: pltpu.VMEM((1, cols), F32)
    acc = lambda: pltpu.VMEM((LANES, cols), F32)
    return pl.pallas_call(
        kern,
        out_shape=jax.ShapeDtypeStruct((t, N_HEADS * HEAD_DIM), BF16),
        grid=(batch, N_KV, nq),
        in_specs=[
            pl.BlockSpec((hq, tq), lambda b, g, qi: (g, b * nq + qi)),
            pl.BlockSpec((GATE_ROWS, tq), lambda b, g, qi: (N_HEADS * HEAD_DIM // GATE_ROWS + g, b * nq + qi)),
            pl.BlockSpec((seq, 2 * LANES), lambda b, g, qi: (b, g)),
            pl.BlockSpec((1, seq // V_TILE, 1, LANES, V_TILE), lambda b, g, qi: (b, 0, g, 0, 0)),
            pl.BlockSpec((1, nc, LANES), lambda b, g, qi: (b * N_KV + g, 0, 0)),
            pl.BlockSpec((1, LANES, nc), lambda b, g, qi: (b * N_KV + g, 0, 0)),
            pl.BlockSpec((LANES, nc), lambda b, g, qi: (0, 0)),
            pl.BlockSpec((HEAD_DIM, tq), lambda b, g, qi: (0, 0)),
            tab(), tab(),
        ],
        out_specs=pl.BlockSpec((tq, hq), lambda b, g, qi: (b * nq + qi, g)),
        scratch_shapes=[
            pltpu.VMEM((2 * LANES, cols), BF16), pltpu.VMEM((LANES, cols), BF16), acc(),
            stat(), stat(), acc(), stat(), stat(), acc(),
            pltpu.VMEM((tk, cols), F32), pltpu.VMEM((tk, cols), F32), pltpu.VMEM((tq, cols), F32), pltpu.VMEM((WINDOW, cols), F32),
        ],
        compiler_params=_params("parallel", "parallel", "arbitrary"),
        name="nsa_attention",
    )(qgt, qgt, kaug, vt, kvc, kvct, cmapt, qg, cos, sin)


def _rope_angles(seq):
    half = ROT_DIM // 2
    inv_freq = ROPE_THETA ** (-jnp.arange(half, dtype=F32) * (2.0 / ROT_DIM))
    ang = jnp.arange(seq, dtype=F32)[:, None] * inv_freq[None, :]
    return jnp.cos(ang), jnp.sin(ang)


def _rope_tables(cos, sin):
    seq, half = cos.shape
    zeros = jnp.zeros((seq, HEAD_DIM - ROT_DIM), F32)
    zh = jnp.zeros((seq, half), F32)
    c = jnp.concatenate([cos, cos, zeros + 1.0], axis=1)
    s1 = jnp.concatenate([-sin, zh, zeros], axis=1)
    s2 = jnp.concatenate([zh, sin, zeros], axis=1)
    reps = LANES // HEAD_DIM
    return tuple(jnp.tile(a, (1, reps)) for a in (c, s1, s2))


def _cmp_to_sel_t(nc, n_sel):
    start_c = np.arange(nc)[None, :] * CMP_STRIDE
    start_s = np.arange(LANES)[:, None] * SEL_LEN
    ov = np.minimum(start_c + CMP_LEN, start_s + SEL_LEN) - np.maximum(start_c, start_s)
    m = np.maximum(ov, 0).astype(np.float32) / CMP_LEN
    m[n_sel:, :] = 0.0
    m[:, nc - 1:] = 0.0
    return jnp.asarray(m, BF16)


def _nsa_in_weights(w_in):
    d = w_in.shape[0]
    hq, hk = N_HEADS * HEAD_DIM, N_KV * HEAD_DIM
    offs = np.cumsum([0, hq] + [hk] * 6)
    q, kc, vc, ks, vs, kw, vw = (w_in[:, offs[n]:offs[n + 1]] for n in range(7))
    gl = w_in[:, offs[7]:]
    per_group = lambda a, b: jnp.stack([a.reshape(d, N_KV, HEAD_DIM), b.reshape(d, N_KV, HEAD_DIM)],
                                       axis=2).reshape(d, N_KV * LANES)
    gates = jnp.pad(gl.reshape(d, N_KV, 3 * GROUP), ((0, 0), (0, 0), (0, GATE_ROWS - 3 * GROUP)))
    wn = jnp.concatenate([per_group(ks, kw), kc, vc], axis=1).astype(BF16)
    wt = jnp.concatenate([q, gates.reshape(d, N_KV * GATE_ROWS), per_group(vs, vw)], axis=1).T.astype(BF16)
    return wn, wt


def _nsa_layer(x, g, w_in, q_g, kc_g, ks_g, kw_g, pe_k, pe_v, ck_w1, ck_w2, cv_w1, cv_w2, w_out, batch, seq):
    nc = seq // CMP_STRIDE
    n_sel = seq // SEL_LEN
    assert n_sel <= LANES and seq % CMP_STRIDE == 0 and 3 * GROUP <= GATE_ROWS
    wn, wt = _nsa_in_weights(w_in)
    proj, qgt, vt = _in_proj(x, g, wn, wt, batch, seq)
    cos, sin = _rope_angles(seq)
    seg = np.arange(LANES) // HEAD_DIM
    bd = jnp.asarray(seg[:, None] == seg[None, :], BF16)
    kaug = _kprep(proj, jnp.concatenate([ks_g, kw_g]).reshape(1, LANES), bd, _rope_tables(cos, sin), seq)
    hk = N_KV * HEAD_DIM
    c0 = N_KV * LANES
    chunks = lambda cols: (cols.reshape(batch, seq, N_KV, HEAD_DIM).transpose(0, 2, 1, 3)
                           .reshape(batch * N_KV, nc, CMP_STRIDE * HEAD_DIM))
    kvc, kvct = _compress(chunks(proj[:, c0:c0 + hk]), chunks(proj[:, c0 + hk:c0 + 2 * hk]),
                          pe_k, pe_v, ck_w1, ck_w2, cv_w1, cv_w2, kc_g)
    tq = min(256, seq)
    qg = jnp.broadcast_to((q_g * float(HEAD_DIM ** -0.5 * np.log2(np.e)))[:, None], (HEAD_DIM, tq))
    attn = _attention(qgt, kaug, vt, kvc, kvct, _cmp_to_sel_t(nc, n_sel), qg, cos.T, sin.T, batch, seq)
    return _out_proj(attn, x, w_out)


def kernel(x, mix_norm_g, ffn_norm_g, conv_w_pw1, conv_b_pw1, conv_w_dw, conv_b_dw, conv_ln_g, conv_ln_b, conv_w_pw2, conv_b_pw2, nsa_w_in, nsa_q_norm, nsa_kc_norm, nsa_ks_norm, nsa_kw_norm, nsa_pe_k, nsa_pe_v, nsa_ck_w1, nsa_ck_w2, nsa_cv_w1, nsa_cv_w2, nsa_w_out, ffn_w_up, ffn_w_dw, ffn_b_dw, ffn_w_down):
    batch, seq, d = x.shape
    depth = mix_norm_g.shape[0]
    n_mixers = 2
    h = x.reshape(batch * seq, d)
    for i in range(depth):
        j = i // n_mixers
        if i % n_mixers == 0:
            u = _conformer_glu(h, mix_norm_g[i], conv_w_pw1[j], conv_b_pw1[j])
            h = _conformer_conv_out(u, h, conv_w_dw[j], conv_b_dw[j], conv_ln_g[j], conv_ln_b[j],
                                    conv_w_pw2[j], conv_b_pw2[j], seq)
        else:
            h = _nsa_layer(h, mix_norm_g[i], nsa_w_in[j], nsa_q_norm[j], nsa_kc_norm[j], nsa_ks_norm[j],
                           nsa_kw_norm[j], nsa_pe_k[j], nsa_pe_v[j], nsa_ck_w1[j], nsa_ck_w2[j], nsa_cv_w1[j],
                           nsa_cv_w2[j], nsa_w_out[j], batch, seq)
        h = _conv_ffn(h, ffn_norm_g[i], ffn_w_up[i], ffn_w_dw[i], ffn_b_dw[i], ffn_w_down[i], seq)
    return h.reshape(batch, seq, d)
```

```python
import functools

import numpy as np
import jax
import jax.numpy as jnp
from jax import lax
from jax.experimental import pallas as pl
from jax.experimental.pallas import tpu as pltpu

N_HEADS = 16
HEAD_DIM = 64
N_KV = 4
GROUP = N_HEADS // N_KV
ROT_DIM = HEAD_DIM // 4
ROPE_THETA = 500000.0
CMP_LEN = 32
CMP_STRIDE = 16
SEL_LEN = 64
SEL_TOPK = 16
WINDOW = 512
EPS = 1e-6
NEG = -1e30

LANES = 128
SUBLANES = 8
BF16_ROWS = 16
V_TILE = 256
GATE_ROWS = 16
VMEM_LIMIT = 48 * 1024 * 1024

F32 = jnp.float32
BF16 = jnp.bfloat16
NT_DIMS = (((1,), (1,)), ((), ()))


def _params(*sem):
    return pltpu.CompilerParams(dimension_semantics=sem, vmem_limit_bytes=VMEM_LIMIT)


def _dot(a, b):
    return jnp.dot(a, b, preferred_element_type=F32)


def _dot_nt(a, b):
    return lax.dot_general(a, b, NT_DIMS, preferred_element_type=F32)


def _split_bf16(x):
    hi = x.astype(BF16)
    lo = (x - hi.astype(F32)).astype(BF16)
    return hi, lo


def _rms_rows(x, g):
    ms = jnp.mean(x * x, axis=-1, keepdims=True)
    return x * lax.rsqrt(ms + EPS) * g


def _sigmoid(x):
    return 1.0 / (1.0 + jnp.exp(-x))


def _head_rms(xb, bd, gain):
    hi, lo = _split_bf16(xb * xb)
    ss = _dot(hi, bd) + _dot(lo, bd)
    return xb * lax.rsqrt(ss * (1.0 / HEAD_DIM) + EPS) * gain


def _rope(xb, c, s1, s2):
    half = ROT_DIM // 2
    return xb * c + pltpu.roll(xb, LANES - half, 1) * s1 + pltpu.roll(xb, half, 1) * s2


def _ffn_kernel(x_ref, halo_ref, g_ref, wup_ref, wdw_ref, bdw_ref, wd_ref, o_ref,
                hn_ref, a0_ref, v0_ref, a1_ref, v1_ref, *, tm, tiles_per_seq, nch):
    i = pl.program_id(0)
    hl = BF16_ROWS
    g = g_ref[...]
    hn_ref[hl:, :] = _rms_rows(x_ref[...], g).astype(BF16)
    first = (i % tiles_per_seq) == 0
    hn_ref[0:hl, :] = jnp.where(first, 0.0, _rms_rows(halo_ref[...], g)).astype(BF16)
    o_ref[...] = x_ref[...]

    def up(c, a_ref, v_ref):
        a_ref[...] = _dot(hn_ref[...], wup_ref[c])
        v_ref[...] = _dot(hn_ref[hl:, :], wup_ref[nch + c])

    def down(c, a_ref, v_ref):
        w = wdw_ref[c]
        cv = (w[0:1] * a_ref[hl - 2:hl - 2 + tm, :] + w[1:2] * a_ref[hl - 1:hl - 1 + tm, :]
              + w[2:3] * a_ref[hl:hl + tm, :] + bdw_ref[c])
        act = (cv * _sigmoid(cv) * v_ref[...]).astype(BF16)
        o_ref[...] += _dot(act, wd_ref[c])

    buf0, buf1 = (a0_ref, v0_ref), (a1_ref, v1_ref)
    up(0, *buf0)

    def pair_body(p, carry):
        c = 2 * p
        up(c + 1, *buf1)
        down(c, *buf0)
        up(c + 2, *buf0)
        down(c + 1, *buf1)
        return carry

    lax.fori_loop(0, (nch - 1) // 2, pair_body, 0)
    if nch % 2 == 1:
        down(nch - 1, *buf0)
    else:
        up(nch - 1, *buf1)
        down(nch - 2, *buf0)
        down(nch - 1, *buf1)


def _conv_ffn(x, g, w_up, w_dw, b_dw, w_down, seq):
    t, d = x.shape
    dff = w_down.shape[0]
    tm = min(512, seq)
    tf = 256
    assert seq % tm == 0 and dff % tf == 0 and w_dw.shape[0] == 3
    nch = dff // tf
    hl = BF16_ROWS
    kern = functools.partial(_ffn_kernel, tm=tm, tiles_per_seq=seq // tm, nch=nch)
    wup = w_up.astype(BF16).reshape(d, 2 * nch, tf).transpose(1, 0, 2)
    wdn = w_down.astype(BF16).reshape(nch, tf, d)
    wdw = w_dw.reshape(3, nch, tf).transpose(1, 0, 2)
    bdw = b_dw.reshape(nch, 1, tf)
    resident = lambda a: pl.BlockSpec(a.shape, lambda i: (0,) * a.ndim, pipeline_mode=pl.Buffered(1))
    return pl.pallas_call(
        kern,
        out_shape=jax.ShapeDtypeStruct((t, d), F32),
        grid=(t // tm,),
        in_specs=[
            pl.BlockSpec((tm, d), lambda i: (i, 0)),
            pl.BlockSpec((hl, d), lambda i: (jnp.maximum(i * (tm // hl) - 1, 0), 0)),
            pl.BlockSpec((1, d), lambda i: (0, 0)),
            resident(wup), resident(wdw), resident(bdw), resident(wdn),
        ],
        out_specs=pl.BlockSpec((tm, d), lambda i: (i, 0)),
        scratch_shapes=[
            pltpu.VMEM((tm + hl, d), BF16),
            pltpu.VMEM((tm + hl, tf), F32), pltpu.VMEM((tm, tf), F32),
            pltpu.VMEM((tm + hl, tf), F32), pltpu.VMEM((tm, tf), F32),
        ],
        compiler_params=_params("parallel"),
        name="conv_ffn",
    )(x, x, g.reshape(1, d), wup, wdw, bdw, wdn)


def _glu_kernel(x_ref, g_ref, wa_ref, wg_ref, ba_ref, bg_ref, o_ref, hn_ref):
    @pl.when(pl.program_id(1) == 0)
    def _():
        hn_ref[...] = _rms_rows(x_ref[...], g_ref[...]).astype(BF16)

    hn = hn_ref[...]
    a = _dot(hn, wa_ref[...]) + ba_ref[...]
    gate = _dot(hn, wg_ref[...]) + bg_ref[...]
    o_ref[...] = a * _sigmoid(gate)


def _conformer_glu(x, g, w_pw1, b_pw1):
    t, d = x.shape
    tm = min(1024, t)
    tn = 256
    nj = d // tn
    w = w_pw1.astype(BF16)
    b = b_pw1.reshape(1, 2 * d)
    return pl.pallas_call(
        _glu_kernel,
        out_shape=jax.ShapeDtypeStruct((t, d), F32),
        grid=(t // tm, nj),
        in_specs=[
            pl.BlockSpec((tm, d), lambda i, j: (i, 0)),
            pl.BlockSpec((1, d), lambda i, j: (0, 0)),
            pl.BlockSpec((d, tn), lambda i, j: (0, j)),
            pl.BlockSpec((d, tn), lambda i, j: (0, j + nj)),
            pl.BlockSpec((1, tn), lambda i, j: (0, j)),
            pl.BlockSpec((1, tn), lambda i, j: (0, j + nj)),
        ],
        out_specs=pl.BlockSpec((tm, tn), lambda i, j: (i, j)),
        scratch_shapes=[pltpu.VMEM((tm, d), BF16)],
        compiler_params=_params("parallel", "arbitrary"),
        name="conformer_glu",
    )(x, g.reshape(1, d), w, w, b, b)


def _dwconv_kernel(u_ref, halo_ref, x_ref, wdw_ref, bdw_ref, lng_ref, lnb_ref, w2_ref, b2_ref, o_ref,
                   ext_ref, cv_ref, slab_ref, *, tm, halo, width, tiles_per_seq, row_chunk, col_chunk):
    i = pl.program_id(0)
    d = u_ref.shape[1]
    first = (i % tiles_per_seq) == 0
    ext_ref[0:halo, :] = jnp.where(first, 0.0, halo_ref[...])
    ext_ref[halo:, :] = u_ref[...]
    off = halo - (width - 1)
    for r0 in range(0, tm, row_chunk):
        for c0 in range(0, d, col_chunk):
            cs = slice(c0, c0 + col_chunk)
            acc = jnp.broadcast_to(bdw_ref[:, cs], (row_chunk, col_chunk))
            for r in range(min(SUBLANES, width)):
                taps = range(r, width, SUBLANES)
                lo = r0 + off + r
                rows = row_chunk + (len(taps) - 1) * SUBLANES
                slab_ref[r, 0:rows, :] = ext_ref[lo:lo + rows, cs]
                for q, k in enumerate(taps):
                    acc = acc + wdw_ref[k:k + 1, cs] * slab_ref[r, q * SUBLANES:q * SUBLANES + row_chunk, :]
            cv_ref[r0:r0 + row_chunk, cs] = acc
    u = cv_ref[...]
    mu = jnp.mean(u, axis=-1, keepdims=True)
    uc = u - mu
    var = jnp.mean(uc * uc, axis=-1, keepdims=True)
    y = uc * lax.rsqrt(var + EPS) * lng_ref[...] + lnb_ref[...]
    s = (y * _sigmoid(y)).astype(BF16)
    o_ref[...] = x_ref[...] + _dot(s, w2_ref[...]) + b2_ref[...]


def _conformer_conv_out(u, x, w_dw, b_dw, ln_g, ln_b, w_pw2, b_pw2, seq):
    t, d = x.shape
    width = w_dw.shape[0]
    halo = 32
    assert width - 1 <= halo
    tm = min(256, seq)
    assert seq % tm == 0 and tm % halo == 0
    row_chunk, col_chunk = 64, 256
    kern = functools.partial(_dwconv_kernel, tm=tm, halo=halo, width=width, tiles_per_seq=seq // tm,
                             row_chunk=row_chunk, col_chunk=col_chunk)
    slab_rows = row_chunk + (width - 1) // SUBLANES * SUBLANES
    vec = lambda: pl.BlockSpec((1, d), lambda i: (0, 0))
    return pl.pallas_call(
        kern,
        out_shape=jax.ShapeDtypeStruct((t, d), F32),
        grid=(t // tm,),
        in_specs=[
            pl.BlockSpec((tm, d), lambda i: (i, 0)),
            pl.BlockSpec((halo, d), lambda i: (jnp.maximum(i * (tm // halo) - 1, 0), 0)),
            pl.BlockSpec((tm, d), lambda i: (i, 0)),
            pl.BlockSpec((width, d), lambda i: (0, 0)),
            vec(), vec(), vec(),
            pl.BlockSpec((d, d), lambda i: (0, 0)),
            vec(),
        ],
        out_specs=pl.BlockSpec((tm, d), lambda i: (i, 0)),
        scratch_shapes=[pltpu.VMEM((tm + halo, d), F32), pltpu.VMEM((tm, d), F32),
                        pltpu.VMEM((SUBLANES, slab_rows, col_chunk), F32)],
        compiler_params=_params("parallel"),
        name="conformer_dwconv_out",
    )(u, u, x, w_dw, b_dw.reshape(1, d), ln_g.reshape(1, d), ln_b.reshape(1, d), w_pw2.astype(BF16),
      b_pw2.reshape(1, d))


def _in_proj_kernel(x_ref, g_ref, wn_ref, wt_ref, on_ref, oq_ref, ov_ref, *, n_qg, row_chunk):
    hn = _rms_rows(x_ref[...], g_ref[...]).astype(BF16)
    on_ref[...] = _dot(hn, wn_ref[...])
    for r0 in range(0, n_qg, row_chunk):
        r1 = min(r0 + row_chunk, n_qg)
        oq_ref[r0:r1, :] = _dot_nt(wt_ref[r0:r1, :], hn)
    vt = _dot_nt(wt_ref[n_qg:, :], hn).astype(BF16)
    for jj in range(ov_ref.shape[1]):
        for g in range(N_KV):
            ov_ref[0, jj, g] = vt[g * LANES:(g + 1) * LANES, jj * V_TILE:(jj + 1) * V_TILE]


def _in_proj(x, g, wn, wt, batch, seq):
    t, d = x.shape
    tm = min(512, seq)
    n_qg = N_HEADS * HEAD_DIM + N_KV * GATE_ROWS
    assert seq % tm == 0 and tm % V_TILE == 0 and wt.shape[0] == n_qg + N_KV * LANES
    per_seq = seq // tm
    kern = functools.partial(_in_proj_kernel, n_qg=n_qg, row_chunk=512)
    return pl.pallas_call(
        kern,
        out_shape=(jax.ShapeDtypeStruct((t, wn.shape[1]), F32),
                   jax.ShapeDtypeStruct((n_qg, t), F32),
                   jax.ShapeDtypeStruct((batch, seq // V_TILE, N_KV, LANES, V_TILE), BF16)),
        grid=(t // tm,),
        in_specs=[
            pl.BlockSpec((tm, d), lambda i: (i, 0)),
            pl.BlockSpec((1, d), lambda i: (0, 0)),
            pl.BlockSpec(wn.shape, lambda i: (0, 0)),
            pl.BlockSpec(wt.shape, lambda i: (0, 0)),
        ],
        out_specs=(pl.BlockSpec((tm, wn.shape[1]), lambda i: (i, 0)),
                   pl.BlockSpec((n_qg, tm), lambda i: (0, i)),
                   pl.BlockSpec((1, tm // V_TILE, N_KV, LANES, V_TILE),
                                lambda i: (i // per_seq, i % per_seq, 0, 0, 0))),
        compiler_params=_params("parallel"),
        name="nsa_in_proj",
    )(x, g.reshape(1, d), wn, wt)


def _out_proj_kernel(a_ref, x_ref, w_ref, o_ref):
    o_ref[...] = x_ref[...] + _dot(a_ref[...], w_ref[...])


def _out_proj(a, x, w):
    t, d = x.shape
    tm = min(512, t)
    return pl.pallas_call(
        _out_proj_kernel,
        out_shape=jax.ShapeDtypeStruct((t, d), F32),
        grid=(t // tm,),
        in_specs=[
            pl.BlockSpec((tm, a.shape[1]), lambda i: (i, 0)),
            pl.BlockSpec((tm, d), lambda i: (i, 0)),
            pl.BlockSpec(w.shape, lambda i: (0, 0)),
        ],
        out_specs=pl.BlockSpec((tm, d), lambda i: (i, 0)),
        compiler_params=_params("parallel"),
        name="nsa_out_proj",
    )(a, x, w.astype(BF16))


def _kprep_kernel(k_ref, gain_ref, bd_ref, c_ref, s1_ref, s2_ref, kaug_ref, *, ts, seq):
    i = pl.program_id(0)
    bd = bd_ref[...]
    gain = gain_ref[...]
    c, s1, s2 = c_ref[...], s1_ref[...], s2_ref[...]
    tpos = (i * ts) % seq + lax.broadcasted_iota(jnp.int32, (ts, LANES), 0)
    lane = lax.broadcasted_iota(jnp.int32, (ts, LANES), 1)
    onehot = jnp.where(tpos // SEL_LEN == lane, 1.0, 0.0).astype(BF16)
    for g in range(N_KV):
        xb = k_ref[:, g * LANES:(g + 1) * LANES]
        xr = _rope(_head_rms(xb, bd, gain), c, s1, s2)
        kaug_ref[:, 2 * g * LANES:(2 * g + 1) * LANES] = xr.astype(BF16)
        kaug_ref[:, (2 * g + 1) * LANES:(2 * g + 2) * LANES] = onehot


def _kprep(proj, gain, bd, tabs, seq):
    t = proj.shape[0]
    ts = min(512, seq)
    assert seq % ts == 0
    nk = N_KV * LANES
    kern = functools.partial(_kprep_kernel, ts=ts, seq=seq)
    tab = lambda: pl.BlockSpec((ts, LANES), lambda i: (i % (seq // ts), 0))
    return pl.pallas_call(
        kern,
        out_shape=jax.ShapeDtypeStruct((t, 2 * nk), BF16),
        grid=(t // ts,),
        in_specs=[
            pl.BlockSpec((ts, nk), lambda i: (i, 0)),
            pl.BlockSpec((1, LANES), lambda i: (0, 0)),
            pl.BlockSpec((LANES, LANES), lambda i: (0, 0)),
            tab(), tab(), tab(),
        ],
        out_specs=pl.BlockSpec((ts, 2 * nk), lambda i: (i, 0)),
        compiler_params=_params("parallel"),
        name="nsa_key_prep",
    )(proj, gain, bd, *tabs)


def _compress_kernel(kc_ref, vc_ref, pek_ref, pev_ref, w1k_ref, w1v_ref, w2k_ref, w2v_ref, gain_ref, o_ref, ot_ref):
    nc = kc_ref.shape[1]

    def hidden(c_ref, pe_ref, w1_ref):
        x = c_ref[0]
        pe = pe_ref[...]
        first = _dot((x + pe[0:1]).astype(BF16), w1_ref[0])
        second = _dot((x + pe[1:2]).astype(BF16), w1_ref[1])
        pre = first + pltpu.roll(second, nc - 1, 0)
        return (pre * _sigmoid(pre)).astype(BF16)

    kv = _dot(hidden(kc_ref, pek_ref, w1k_ref), w2k_ref[...]) + _dot(hidden(vc_ref, pev_ref, w1v_ref), w2v_ref[...])
    is_k = lax.broadcasted_iota(jnp.int32, kv.shape, 1) < HEAD_DIM
    ss = jnp.sum(jnp.where(is_k, kv * kv, 0.0), axis=-1, keepdims=True)
    kn = kv * lax.rsqrt(ss * (1.0 / HEAD_DIM) + EPS) * gain_ref[...]
    out = jnp.where(is_k, kn, kv)
    o_ref[0] = out.astype(BF16)
    ot_ref[0] = out.T.astype(BF16)


def _compress(kc, vc, pe_k, pe_v, ck_w1, ck_w2, cv_w1, cv_w2, kc_g):
    bg, nc, cw = kc.shape
    hid = ck_w1.shape[1]
    half = CMP_STRIDE * HEAD_DIM
    assert cw == half and CMP_LEN == 2 * CMP_STRIDE
    pad = lambda w, left: jnp.pad(w, ((0, 0), (HEAD_DIM, 0) if left else (0, HEAD_DIM))).astype(BF16)
    gain = jnp.concatenate([kc_g, jnp.ones((HEAD_DIM,), F32)]).reshape(1, LANES)
    chunk = lambda: pl.BlockSpec((1, nc, cw), lambda i: (i, 0, 0))
    full = lambda shape: pl.BlockSpec(shape, lambda i: (0,) * len(shape))
    return pl.pallas_call(
        _compress_kernel,
        out_shape=(jax.ShapeDtypeStruct((bg, nc, LANES), BF16), jax.ShapeDtypeStruct((bg, LANES, nc), BF16)),
        grid=(bg,),
        in_specs=[chunk(), chunk(), full((2, half)), full((2, half)), full((2, half, hid)), full((2, half, hid)),
                  full((hid, LANES)), full((hid, LANES)), full((1, LANES))],
        out_specs=(pl.BlockSpec((1, nc, LANES), lambda i: (i, 0, 0)), pl.BlockSpec((1, LANES, nc), lambda i: (i, 0, 0))),
        compiler_params=_params("parallel"),
        name="nsa_compress",
    )(kc, vc, pe_k.reshape(2, half), pe_v.reshape(2, half), ck_w1.reshape(2, half, hid).astype(BF16),
      cv_w1.reshape(2, half, hid).astype(BF16), pad(ck_w2, False), pad(cv_w2, True), gain)


def _attn_kernel(qt_ref, gate_ref, kaug_ref, vt_ref, kvc_ref, kvct_ref, cmapt_ref, qg_ref, cos_ref, sin_ref,
                 o_ref,
                 qaug_ref, qwin_ref, oc_ref, ms_ref, ls_ref, accs_ref, mw_ref, lw_ref, accw_ref,
                 sa_ref, sb_ref, wdiag_ref, wold_ref, imp_ref,
                 *, tq, tk, k_top, n_cmp_var):
    qi = pl.program_id(2)
    t0 = qi * tq
    cols = GROUP * tq
    half = ROT_DIM // 2

    gain = qg_ref[...]
    cos, sin = cos_ref[...], sin_ref[...]
    nope, rope = [], []
    for r in range(GROUP):
        x = qt_ref[r * HEAD_DIM:(r + 1) * HEAD_DIM, :]
        ss = jnp.sum(x * x, axis=0, keepdims=True)
        xn = x * lax.rsqrt(ss * (1.0 / HEAD_DIM) + EPS) * gain
        x1, x2 = xn[0:half], xn[half:ROT_DIM]
        nope.append(xn)
        rope.append(jnp.concatenate([x1 * cos - x2 * sin, x2 * cos + x1 * sin, xn[ROT_DIM:]], axis=0))
    zeros = jnp.zeros((HEAD_DIM, cols), BF16)
    q_nope = jnp.concatenate([jnp.concatenate(nope, axis=1).astype(BF16), zeros], axis=0)
    q_rope = jnp.concatenate(rope, axis=1).astype(BF16)
    qwin_ref[...] = jnp.concatenate([zeros, q_rope], axis=0)

    tpos = t0 + lax.broadcasted_iota(jnp.int32, (1, tq), 1)

    def per_head(x):
        return jnp.concatenate([x] * GROUP, axis=1)

    def fold_tile(s, keep, k0, v_rows, m_ref, l_ref, acc_ref):
        if keep is not None:
            s = jnp.where(per_head(keep), s, NEG)
        m_prev = m_ref[...]
        m_new = jnp.maximum(m_prev, jnp.max(s, axis=0, keepdims=True))
        p = jnp.exp2(s - m_new)
        alpha = jnp.exp2(m_prev - m_new)
        l_ref[...] = alpha * l_ref[...] + jnp.sum(p, axis=0, keepdims=True)
        pb = p.astype(BF16)
        v_tile0 = k0 // V_TILE
        pv = _dot(vt_ref[0, v_tile0, 0, v_rows, :], pb[0:V_TILE])
        for c in range(1, s.shape[0] // V_TILE):
            pv = pv + _dot(vt_ref[0, v_tile0 + c, 0, v_rows, :], pb[c * V_TILE:(c + 1) * V_TILE])
        acc_ref[...] = alpha * acc_ref[...] + pv
        m_ref[...] = m_new

    def back(k0, size):
        return tpos - (k0 + lax.broadcasted_iota(jnp.int32, (size, 1), 0))

    def sel_scores(k0):
        return _dot(kaug_ref[pl.ds(k0, tk), :], qaug_ref[...])

    def win_scores(k0, size):
        return _dot(kaug_ref[pl.ds(k0, size), 0:LANES], qwin_ref[...])

    for m_ref, l_ref, acc_ref in ((ms_ref, ls_ref, accs_ref), (mw_ref, lw_ref, accw_ref)):
        m_ref[...] = jnp.full_like(m_ref, NEG)
        l_ref[...] = jnp.zeros_like(l_ref)
        acc_ref[...] = jnp.zeros_like(acc_ref)

    t0a = pl.multiple_of(t0, tq)
    k_old = pl.multiple_of(jnp.maximum(t0 - WINDOW, 0), tq)
    wdiag_ref[...] = win_scores(t0a, tq)
    wold_ref[...] = win_scores(k_old, WINDOW)
    qaug_ref[0:LANES, :] = jnp.concatenate([q_rope, zeros], axis=0)
    sa_ref[...] = _dot(kaug_ref[0:tk, 0:LANES], qaug_ref[0:LANES, :])
    win_state = (slice(HEAD_DIM, LANES), mw_ref, lw_ref, accw_ref)
    fold_tile(wdiag_ref[...], back(t0a, tq) >= 0, t0a, *win_state)
    b_old = back(k_old, WINDOW)
    fold_tile(wold_ref[...], (b_old < WINDOW) & (b_old > tpos - t0), k_old, *win_state)

    nc = kvc_ref.shape[1]

    def cmp_branch(rows):
        kvc = kvc_ref[0, 0:rows, :]
        cmp_end = lax.broadcasted_iota(jnp.int32, (rows, 1), 0) * CMP_STRIDE + (CMP_LEN - 1)
        s_c = jnp.where(per_head(cmp_end <= tpos), _dot(kvc, q_nope), NEG)
        e = jnp.exp2(s_c - jnp.maximum(jnp.max(s_c, axis=0, keepdims=True), 0.1 * NEG))
        p_c = e * (1.0 / jnp.maximum(jnp.sum(e, axis=0, keepdims=True), 1e-30))
        oc_ref[...] = _dot(kvct_ref[0, HEAD_DIM:, 0:rows], p_c.astype(BF16))
        p_sum = p_c[:, 0:tq]
        for r in range(1, GROUP):
            p_sum = p_sum + p_c[:, r * tq:(r + 1) * tq]
        hi, lo = _split_bf16(p_sum)
        imp_ref[...] = _dot(cmapt_ref[:, 0:rows], hi) + _dot(cmapt_ref[:, 0:rows], lo)

    variant = (t0 + tq - 1) // (CMP_STRIDE * (nc // n_cmp_var))
    for v in range(n_cmp_var):
        pl.when(variant == v)(functools.partial(cmp_branch, (v + 1) * (nc // n_cmp_var)))
    imp = imp_ref[...]

    blk = lax.broadcasted_iota(jnp.int32, (LANES, tq), 0)
    cur = tpos // SEL_LEN
    forced = (blk == 0) | (blk == cur) | (blk == cur - 1)
    taken = -3e38
    imp = jnp.where(blk > cur, -1.0, jnp.where(forced, taken, imp))
    blk_f = blk.astype(F32)
    for _ in range(max(k_top - 3, 0)):
        mx = jnp.max(imp, axis=0, keepdims=True)
        idx = jnp.min(jnp.where(imp == mx, blk_f, float(LANES)), axis=0, keepdims=True)
        imp = jnp.where(blk_f == idx, taken, imp)
    bias = jnp.where(imp == taken, 0.0, NEG)
    qaug_ref[LANES:, :] = per_head(bias.astype(BF16))
    blocks0 = tk // SEL_LEN
    bias0 = jnp.broadcast_to(bias[0:blocks0, None, :], (blocks0, SEL_LEN, tq)).reshape(tk, tq)
    sa_ref[...] = sa_ref[...] + per_head(bias0)


    n = t0 // tk
    sel_state = (slice(0, HEAD_DIM), ms_ref, ls_ref, accs_ref)

    def pair_body(i, carry):
        k0 = pl.multiple_of(2 * i * tk, tk)
        sb_ref[...] = sel_scores(k0 + tk)
        fold_tile(sa_ref[...], None, k0, *sel_state)
        sa_ref[...] = sel_scores(k0 + 2 * tk)
        fold_tile(sb_ref[...], None, k0 + tk, *sel_state)
        return carry

    lax.fori_loop(0, n // 2, pair_body, 0)
    k_diag = pl.multiple_of(n * tk, tk)

    @pl.when(n % 2 == 1)
    def _():
        sb_ref[...] = sel_scores(k_diag)
        fold_tile(sa_ref[...], None, k_diag - tk, *sel_state)
        fold_tile(sb_ref[...], back(k_diag, tk) >= 0, k_diag, *sel_state)

    @pl.when(n % 2 == 0)
    def _():
        fold_tile(sa_ref[...], back(k_diag, tk) >= 0, k_diag, *sel_state)

    o_s = accs_ref[...] / ls_ref[...]
    o_w = accw_ref[...] / lw_ref[...]
    o_c = oc_ref[...]
    sig = _sigmoid(gate_ref[...])
    heads = []
    for r in range(GROUP):
        cs = slice(r * tq, (r + 1) * tq)
        g_c, g_s, g_w = (sig[3 * r + b:3 * r + b + 1, :] for b in range(3))
        heads.append(g_c * o_c[:, cs] + g_s * o_s[:, cs] + g_w * o_w[:, cs])
    o_ref[...] = jnp.concatenate(heads, axis=0).T.astype(o_ref.dtype)


def _attention(qgt, kaug, vt, kvc, kvct, cmapt, qg, cos, sin, batch, seq):
    t = qgt.shape[1]
    tq = min(256, seq)
    tk = min(512, seq)
    nq = seq // tq
    nc = kvc.shape[1]
    assert seq % tk == 0 and tk % tq == 0 and WINDOW % tq == 0 and tq == V_TILE and seq >= WINDOW
    cols = GROUP * tq
    hq = GROUP * HEAD_DIM
    k_top = min(SEL_TOPK, seq // SEL_LEN)
    n_cmp_var = max(1, min(4, nc // LANES))
    assert k_top >= 3 and nc % n_cmp_var == 0
    kern = functools.partial(_attn_kernel, tq=tq, tk=tk, k_top=k_top, n_cmp_var=n_cmp_var)
    tab = lambda: pl.BlockSpec((ROT_DIM // 2, tq), lambda b, g, qi: (0, qi))
    stat = lambda: pltpu.VMEM((1, cols), F32)
    acc = lambda: pltpu.VMEM((HEAD_DIM, cols), F32)
    return pl.pallas_call(
        kern,
        out_shape=jax.ShapeDtypeStruct((t, N_HEADS * HEAD_DIM), BF16),
        grid=(batch, N_KV, nq),
        in_specs=[
            pl.BlockSpec((hq, tq), lambda b, g, qi: (g, b * nq + qi)),
            pl.BlockSpec((GATE_ROWS, tq), lambda b, g, qi: (N_HEADS * HEAD_DIM // GATE_ROWS + g, b * nq + qi)),
            pl.BlockSpec((seq, 2 * LANES), lambda b, g, qi: (b, g)),
            pl.BlockSpec((1, seq // V_TILE, 1, LANES, V_TILE), lambda b, g, qi: (b, 0, g, 0, 0)),
            pl.BlockSpec((1, nc, LANES), lambda b, g, qi: (b * N_KV + g, 0, 0)),
            pl.BlockSpec((1, LANES, nc), lambda b, g, qi: (b * N_KV + g, 0, 0)),
            pl.BlockSpec((LANES, nc), lambda b, g, qi: (0, 0)),
            pl.BlockSpec((HEAD_DIM, tq), lambda b, g, qi: (0, 0)),
            tab(), tab(),
        ],
        out_specs=pl.BlockSpec((tq, hq), lambda b, g, qi: (b * nq + qi, g)),
        scratch_shapes=[
            pltpu.VMEM((2 * LANES, cols), BF16), pltpu.VMEM((LANES, cols), BF16), acc(),
            stat(), stat(), acc(), stat(), stat(), acc(),
            pltpu.VMEM((tk, cols), F32), pltpu.VMEM((tk, cols), F32), pltpu.VMEM((tq, cols), F32), pltpu.VMEM((WINDOW, cols), F32),
            pltpu.VMEM((LANES, tq), F32),
        ],
        compiler_params=_params("parallel", "parallel", "arbitrary"),
        name="nsa_attention",
    )(qgt, qgt, kaug, vt, kvc, kvct, cmapt, qg, cos, sin)


def _rope_angles(seq):
    half = ROT_DIM // 2
    inv_freq = ROPE_THETA ** (-jnp.arange(half, dtype=F32) * (2.0 / ROT_DIM))
    ang = jnp.arange(seq, dtype=F32)[:, None] * inv_freq[None, :]
    return jnp.cos(ang), jnp.sin(ang)


def _rope_tables(cos, sin):
    seq, half = cos.shape
    zeros = jnp.zeros((seq, HEAD_DIM - ROT_DIM), F32)
    zh = jnp.zeros((seq, half), F32)
    c = jnp.concatenate([cos, cos, zeros + 1.0], axis=1)
    s1 = jnp.concatenate([-sin, zh, zeros], axis=1)
    s2 = jnp.concatenate([zh, sin, zeros], axis=1)
    reps = LANES // HEAD_DIM
    return tuple(jnp.tile(a, (1, reps)) for a in (c, s1, s2))


def _cmp_to_sel_t(nc, n_sel):
    start_c = np.arange(nc)[None, :] * CMP_STRIDE
    start_s = np.arange(LANES)[:, None] * SEL_LEN
    ov = np.minimum(start_c + CMP_LEN, start_s + SEL_LEN) - np.maximum(start_c, start_s)
    m = np.maximum(ov, 0).astype(np.float32) / CMP_LEN
    m[n_sel:, :] = 0.0
    m[:, nc - 1:] = 0.0
    return jnp.asarray(m, BF16)


def _nsa_in_weights(w_in):
    d = w_in.shape[0]
    hq, hk = N_HEADS * HEAD_DIM, N_KV * HEAD_DIM
    offs = np.cumsum([0, hq] + [hk] * 6)
    q, kc, vc, ks, vs, kw, vw = (w_in[:, offs[n]:offs[n + 1]] for n in range(7))
    gl = w_in[:, offs[7]:]
    per_group = lambda a, b: jnp.stack([a.reshape(d, N_KV, HEAD_DIM), b.reshape(d, N_KV, HEAD_DIM)],
                                       axis=2).reshape(d, N_KV * LANES)
    gates = jnp.pad(gl.reshape(d, N_KV, 3 * GROUP), ((0, 0), (0, 0), (0, GATE_ROWS - 3 * GROUP)))
    wn = jnp.concatenate([per_group(ks, kw), kc, vc], axis=1).astype(BF16)
    wt = jnp.concatenate([q, gates.reshape(d, N_KV * GATE_ROWS), per_group(vs, vw)], axis=1).T.astype(BF16)
    return wn, wt


def _nsa_layer(x, g, w_in, q_g, kc_g, ks_g, kw_g, pe_k, pe_v, ck_w1, ck_w2, cv_w1, cv_w2, w_out, batch, seq):
    nc = seq // CMP_STRIDE
    n_sel = seq // SEL_LEN
    assert n_sel <= LANES and seq % CMP_STRIDE == 0 and 3 * GROUP <= GATE_ROWS
    wn, wt = _nsa_in_weights(w_in)
    proj, qgt, vt = _in_proj(x, g, wn, wt, batch, seq)
    cos, sin = _rope_angles(seq)
    seg = np.arange(LANES) // HEAD_DIM
    bd = jnp.asarray(seg[:, None] == seg[None, :], BF16)
    kaug = _kprep(proj, jnp.concatenate([ks_g, kw_g]).reshape(1, LANES), bd, _rope_tables(cos, sin), seq)
    hk = N_KV * HEAD_DIM
    c0 = N_KV * LANES
    chunks = lambda cols: (cols.reshape(batch, seq, N_KV, HEAD_DIM).transpose(0, 2, 1, 3)
                           .reshape(batch * N_KV, nc, CMP_STRIDE * HEAD_DIM))
    kvc, kvct = _compress(chunks(proj[:, c0:c0 + hk]), chunks(proj[:, c0 + hk:c0 + 2 * hk]),
                          pe_k, pe_v, ck_w1, ck_w2, cv_w1, cv_w2, kc_g)
    tq = min(256, seq)
    qg = jnp.broadcast_to((q_g * float(HEAD_DIM ** -0.5 * np.log2(np.e)))[:, None], (HEAD_DIM, tq))
    attn = _attention(qgt, kaug, vt, kvc, kvct, _cmp_to_sel_t(nc, n_sel), qg, cos.T, sin.T, batch, seq)
    return _out_proj(attn, x, w_out)


def kernel(x, mix_norm_g, ffn_norm_g, conv_w_pw1, conv_b_pw1, conv_w_dw, conv_b_dw, conv_ln_g, conv_ln_b, conv_w_pw2, conv_b_pw2, nsa_w_in, nsa_q_norm, nsa_kc_norm, nsa_ks_norm, nsa_kw_norm, nsa_pe_k, nsa_pe_v, nsa_ck_w1, nsa_ck_w2, nsa_cv_w1, nsa_cv_w2, nsa_w_out, ffn_w_up, ffn_w_dw, ffn_b_dw, ffn_w_down):
    batch, seq, d = x.shape
    depth = mix_norm_g.shape[0]
    n_mixers = 2
    h = x.reshape(batch * seq, d)
    for i in range(depth):
        j = i // n_mixers
        if i % n_mixers == 0:
            u = _conformer_glu(h, mix_norm_g[i], conv_w_pw1[j], conv_b_pw1[j])
            h = _conformer_conv_out(u, h, conv_w_dw[j], conv_b_dw[j], conv_ln_g[j], conv_ln_b[j],
                                    conv_w_pw2[j], conv_b_pw2[j], seq)
        else:
            h = _nsa_layer(h, mix_norm_g[i], nsa_w_in[j], nsa_q_norm[j], nsa_kc_norm[j], nsa_ks_norm[j],
                           nsa_kw_norm[j], nsa_pe_k[j], nsa_pe_v[j], nsa_ck_w1[j], nsa_ck_w2[j], nsa_cv_w1[j],
                           nsa_cv_w2[j], nsa_w_out[j], batch, seq)
        h = _conv_ffn(h, ffn_norm_g[i], ffn_w_up[i], ffn_w_dw[i], ffn_b_dw[i], ffn_w_down[i], seq)
    return h.reshape(batch, seq, d)
```

```python
import functools

import numpy as np
import jax
import jax.numpy as jnp
from jax import lax
from jax.experimental import pallas as pl
from jax.experimental.pallas import tpu as pltpu

N_HEADS = 16
HEAD_DIM = 64
N_KV = 4
GROUP = N_HEADS // N_KV
ROT_DIM = HEAD_DIM // 4
ROPE_THETA = 500000.0
CMP_LEN = 32
CMP_STRIDE = 16
SEL_LEN = 64
SEL_TOPK = 16
WINDOW = 512
EPS = 1e-6
NEG = -1e30

LANES = 128
SUBLANES = 8
BF16_ROWS = 16
V_TILE = 256
GATE_ROWS = 16
VMEM_LIMIT = 48 * 1024 * 1024

F32 = jnp.float32
BF16 = jnp.bfloat16
NT_DIMS = (((1,), (1,)), ((), ()))


def _params(*sem):
    return pltpu.CompilerParams(dimension_semantics=sem, vmem_limit_bytes=VMEM_LIMIT)


def _dot(a, b):
    return jnp.dot(a, b, preferred_element_type=F32)


def _dot_nt(a, b):
    return lax.dot_general(a, b, NT_DIMS, preferred_element_type=F32)


def _split_bf16(x):
    hi = x.astype(BF16)
    lo = (x - hi.astype(F32)).astype(BF16)
    return hi, lo


def _rms_rows(x, g):
    ms = jnp.mean(x * x, axis=-1, keepdims=True)
    return x * lax.rsqrt(ms + EPS) * g


def _sigmoid(x):
    return 1.0 / (1.0 + jnp.exp(-x))


def _head_rms(xb, bd, gain):
    hi, lo = _split_bf16(xb * xb)
    ss = _dot(hi, bd) + _dot(lo, bd)
    return xb * lax.rsqrt(ss * (1.0 / HEAD_DIM) + EPS) * gain


def _rope(xb, c, s1, s2):
    half = ROT_DIM // 2
    return xb * c + pltpu.roll(xb, LANES - half, 1) * s1 + pltpu.roll(xb, half, 1) * s2


def _ffn_kernel(x_ref, halo_ref, g_ref, wup_ref, wdw_ref, bdw_ref, wd_ref, o_ref,
                hn_ref, a0_ref, v0_ref, a1_ref, v1_ref, *, tm, tiles_per_seq, nch):
    i = pl.program_id(0)
    hl = BF16_ROWS
    g = g_ref[...]
    hn_ref[hl:, :] = _rms_rows(x_ref[...], g).astype(BF16)
    first = (i % tiles_per_seq) == 0
    hn_ref[0:hl, :] = jnp.where(first, 0.0, _rms_rows(halo_ref[...], g)).astype(BF16)
    o_ref[...] = x_ref[...]

    tf = a0_ref.shape[1]

    def up(c, a_ref, v_ref):
        a_ref[...] = _dot(hn_ref[...], wup_ref[:, pl.ds(pl.multiple_of(c * tf, tf), tf)])
        v_ref[...] = _dot(hn_ref[hl:, :], wup_ref[:, pl.ds(pl.multiple_of((nch + c) * tf, tf), tf)])

    def down(c, a_ref, v_ref):
        w = wdw_ref[c]
        cv = (w[0:1] * a_ref[hl - 2:hl - 2 + tm, :] + w[1:2] * a_ref[hl - 1:hl - 1 + tm, :]
              + w[2:3] * a_ref[hl:hl + tm, :] + bdw_ref[c])
        act = (cv * _sigmoid(cv) * v_ref[...]).astype(BF16)
        o_ref[...] += _dot(act, wd_ref[c])

    buf0, buf1 = (a0_ref, v0_ref), (a1_ref, v1_ref)
    up(0, *buf0)

    def pair_body(p, carry):
        c = 2 * p
        up(c + 1, *buf1)
        down(c, *buf0)
        up(c + 2, *buf0)
        down(c + 1, *buf1)
        return carry

    lax.fori_loop(0, (nch - 1) // 2, pair_body, 0)
    if nch % 2 == 1:
        down(nch - 1, *buf0)
    else:
        up(nch - 1, *buf1)
        down(nch - 2, *buf0)
        down(nch - 1, *buf1)


def _conv_ffn(x, g, w_up, w_dw, b_dw, w_down, seq):
    t, d = x.shape
    dff = w_down.shape[0]
    tm = min(512, seq)
    tf = 256
    assert seq % tm == 0 and dff % tf == 0 and w_dw.shape[0] == 3
    nch = dff // tf
    hl = BF16_ROWS
    kern = functools.partial(_ffn_kernel, tm=tm, tiles_per_seq=seq // tm, nch=nch)
    wup = w_up.astype(BF16)
    wdn = w_down.astype(BF16).reshape(nch, tf, d)
    wdw = w_dw.reshape(3, nch, tf).transpose(1, 0, 2)
    bdw = b_dw.reshape(nch, 1, tf)
    resident = lambda a: pl.BlockSpec(a.shape, lambda i: (0,) * a.ndim, pipeline_mode=pl.Buffered(1))
    return pl.pallas_call(
        kern,
        out_shape=jax.ShapeDtypeStruct((t, d), F32),
        grid=(t // tm,),
        in_specs=[
            pl.BlockSpec((tm, d), lambda i: (i, 0)),
            pl.BlockSpec((hl, d), lambda i: (jnp.maximum(i * (tm // hl) - 1, 0), 0)),
            pl.BlockSpec((1, d), lambda i: (0, 0)),
            resident(wup), resident(wdw), resident(bdw), resident(wdn),
        ],
        out_specs=pl.BlockSpec((tm, d), lambda i: (i, 0)),
        scratch_shapes=[
            pltpu.VMEM((tm + hl, d), BF16),
            pltpu.VMEM((tm + hl, tf), F32), pltpu.VMEM((tm, tf), F32),
            pltpu.VMEM((tm + hl, tf), F32), pltpu.VMEM((tm, tf), F32),
        ],
        compiler_params=_params("parallel"),
        name="conv_ffn",
    )(x, x, g.reshape(1, d), wup, wdw, bdw, wdn)


def _glu_kernel(x_ref, g_ref, wa_ref, wg_ref, ba_ref, bg_ref, o_ref, hn_ref):
    @pl.when(pl.program_id(1) == 0)
    def _():
        hn_ref[...] = _rms_rows(x_ref[...], g_ref[...]).astype(BF16)

    th = hn_ref.shape[0] // 2
    halves = [(_dot(hn_ref[h * th:(h + 1) * th, :], wa_ref[...]), _dot(hn_ref[h * th:(h + 1) * th, :], wg_ref[...]))
              for h in range(2)]
    for h, (a, gate) in enumerate(halves):
        o_ref[h * th:(h + 1) * th, :] = (a + ba_ref[...]) * _sigmoid(gate + bg_ref[...])


def _conformer_glu(x, g, w_pw1, b_pw1):
    t, d = x.shape
    tm = min(1024, t)
    tn = 256
    nj = d // tn
    w = w_pw1.astype(BF16)
    b = b_pw1.reshape(1, 2 * d)
    return pl.pallas_call(
        _glu_kernel,
        out_shape=jax.ShapeDtypeStruct((t, d), F32),
        grid=(t // tm, nj),
        in_specs=[
            pl.BlockSpec((tm, d), lambda i, j: (i, 0)),
            pl.BlockSpec((1, d), lambda i, j: (0, 0)),
            pl.BlockSpec((d, tn), lambda i, j: (0, j)),
            pl.BlockSpec((d, tn), lambda i, j: (0, j + nj)),
            pl.BlockSpec((1, tn), lambda i, j: (0, j)),
            pl.BlockSpec((1, tn), lambda i, j: (0, j + nj)),
        ],
        out_specs=pl.BlockSpec((tm, tn), lambda i, j: (i, j)),
        scratch_shapes=[pltpu.VMEM((tm, d), BF16)],
        compiler_params=_params("parallel", "arbitrary"),
        name="conformer_glu",
    )(x, g.reshape(1, d), w, w, b, b)


def _dwconv_kernel(u_ref, halo_ref, x_ref, wdw_ref, bdw_ref, lng_ref, lnb_ref, w2_ref, b2_ref, o_ref,
                   ext_ref, cv_ref, slab_ref, *, tm, halo, width, tiles_per_seq, row_chunk, col_chunk):
    i = pl.program_id(0)
    d = u_ref.shape[1]
    first = (i % tiles_per_seq) == 0
    ext_ref[0:halo, :] = jnp.where(first, 0.0, halo_ref[...])
    ext_ref[halo:, :] = u_ref[...]
    off = halo - (width - 1)
    for r0 in range(0, tm, row_chunk):
        for c0 in range(0, d, col_chunk):
            cs = slice(c0, c0 + col_chunk)
            acc = jnp.broadcast_to(bdw_ref[:, cs], (row_chunk, col_chunk))
            for r in range(min(SUBLANES, width)):
                taps = range(r, width, SUBLANES)
                lo = r0 + off + r
                rows = row_chunk + (len(taps) - 1) * SUBLANES
                slab_ref[r, 0:rows, :] = ext_ref[lo:lo + rows, cs]
                for q, k in enumerate(taps):
                    acc = acc + wdw_ref[k:k + 1, cs] * slab_ref[r, q * SUBLANES:q * SUBLANES + row_chunk, :]
            cv_ref[r0:r0 + row_chunk, cs] = acc
    u = cv_ref[...]
    mu = jnp.mean(u, axis=-1, keepdims=True)
    uc = u - mu
    var = jnp.mean(uc * uc, axis=-1, keepdims=True)
    y = uc * lax.rsqrt(var + EPS) * lng_ref[...] + lnb_ref[...]
    s = (y * _sigmoid(y)).astype(BF16)
    o_ref[...] = x_ref[...] + _dot(s, w2_ref[...]) + b2_ref[...]


def _conformer_conv_out(u, x, w_dw, b_dw, ln_g, ln_b, w_pw2, b_pw2, seq):
    t, d = x.shape
    width = w_dw.shape[0]
    halo = 32
    assert width - 1 <= halo
    tm = min(256, seq)
    assert seq % tm == 0 and tm % halo == 0
    row_chunk, col_chunk = 64, 256
    kern = functools.partial(_dwconv_kernel, tm=tm, halo=halo, width=width, tiles_per_seq=seq // tm,
                             row_chunk=row_chunk, col_chunk=col_chunk)
    slab_rows = row_chunk + (width - 1) // SUBLANES * SUBLANES
    vec = lambda: pl.BlockSpec((1, d), lambda i: (0, 0))
    return pl.pallas_call(
        kern,
        out_shape=jax.ShapeDtypeStruct((t, d), F32),
        grid=(t // tm,),
        in_specs=[
            pl.BlockSpec((tm, d), lambda i: (i, 0)),
            pl.BlockSpec((halo, d), lambda i: (jnp.maximum(i * (tm // halo) - 1, 0), 0)),
            pl.BlockSpec((tm, d), lambda i: (i, 0)),
            pl.BlockSpec((width, d), lambda i: (0, 0)),
            vec(), vec(), vec(),
            pl.BlockSpec((d, d), lambda i: (0, 0)),
            vec(),
        ],
        out_specs=pl.BlockSpec((tm, d), lambda i: (i, 0)),
        scratch_shapes=[pltpu.VMEM((tm + halo, d), F32), pltpu.VMEM((tm, d), F32),
                        pltpu.VMEM((SUBLANES, slab_rows, col_chunk), F32)],
        compiler_params=_params("parallel"),
        name="conformer_dwconv_out",
    )(u, u, x, w_dw, b_dw.reshape(1, d), ln_g.reshape(1, d), ln_b.reshape(1, d), w_pw2.astype(BF16),
      b_pw2.reshape(1, d))


def _in_proj_kernel(x_ref, g_ref, wn_ref, wt_ref, on_ref, oq_ref, ov_ref, *, n_qg, row_chunk):
    hn = _rms_rows(x_ref[...], g_ref[...]).astype(BF16)
    on_ref[...] = _dot(hn, wn_ref[...])
    for r0 in range(0, n_qg, row_chunk):
        r1 = min(r0 + row_chunk, n_qg)
        oq_ref[r0:r1, :] = _dot_nt(wt_ref[r0:r1, :], hn)
    vt = _dot_nt(wt_ref[n_qg:, :], hn).astype(BF16)
    for jj in range(ov_ref.shape[1]):
        for g in range(N_KV):
            ov_ref[0, jj, g] = vt[g * LANES:(g + 1) * LANES, jj * V_TILE:(jj + 1) * V_TILE]


def _in_proj(x, g, wn, wt, batch, seq):
    t, d = x.shape
    tm = min(512, seq)
    n_qg = N_HEADS * HEAD_DIM + N_KV * GATE_ROWS
    assert seq % tm == 0 and tm % V_TILE == 0 and wt.shape[0] == n_qg + N_KV * LANES
    per_seq = seq // tm
    kern = functools.partial(_in_proj_kernel, n_qg=n_qg, row_chunk=512)
    return pl.pallas_call(
        kern,
        out_shape=(jax.ShapeDtypeStruct((t, wn.shape[1]), F32),
                   jax.ShapeDtypeStruct((n_qg, t), F32),
                   jax.ShapeDtypeStruct((batch, seq // V_TILE, N_KV, LANES, V_TILE), BF16)),
        grid=(t // tm,),
        in_specs=[
            pl.BlockSpec((tm, d), lambda i: (i, 0)),
            pl.BlockSpec((1, d), lambda i: (0, 0)),
            pl.BlockSpec(wn.shape, lambda i: (0, 0)),
            pl.BlockSpec(wt.shape, lambda i: (0, 0)),
        ],
        out_specs=(pl.BlockSpec((tm, wn.shape[1]), lambda i: (i, 0)),
                   pl.BlockSpec((n_qg, tm), lambda i: (0, i)),
                   pl.BlockSpec((1, tm // V_TILE, N_KV, LANES, V_TILE),
                                lambda i: (i // per_seq, i % per_seq, 0, 0, 0))),
        compiler_params=_params("parallel"),
        name="nsa_in_proj",
    )(x, g.reshape(1, d), wn, wt)


def _out_proj_kernel(a_ref, x_ref, w_ref, o_ref):
    o_ref[...] = x_ref[...] + _dot(a_ref[...], w_ref[...])


def _out_proj(a, x, w):
    t, d = x.shape
    tm = min(512, t)
    return pl.pallas_call(
        _out_proj_kernel,
        out_shape=jax.ShapeDtypeStruct((t, d), F32),
        grid=(t // tm,),
        in_specs=[
            pl.BlockSpec((tm, a.shape[1]), lambda i: (i, 0)),
            pl.BlockSpec((tm, d), lambda i: (i, 0)),
            pl.BlockSpec(w.shape, lambda i: (0, 0)),
        ],
        out_specs=pl.BlockSpec((tm, d), lambda i: (i, 0)),
        compiler_params=_params("parallel"),
        name="nsa_out_proj",
    )(a, x, w.astype(BF16))


def _kprep_kernel(k_ref, gain_ref, bd_ref, c_ref, s1_ref, s2_ref, kaug_ref, *, ts, seq):
    i = pl.program_id(0)
    bd = bd_ref[...]
    gain = gain_ref[...]
    c, s1, s2 = c_ref[...], s1_ref[...], s2_ref[...]
    tpos = (i * ts) % seq + lax.broadcasted_iota(jnp.int32, (ts, LANES), 0)
    lane = lax.broadcasted_iota(jnp.int32, (ts, LANES), 1)
    onehot = jnp.where(tpos // SEL_LEN == lane, 1.0, 0.0).astype(BF16)
    for g in range(N_KV):
        xb = k_ref[:, g * LANES:(g + 1) * LANES]
        xr = _rope(_head_rms(xb, bd, gain), c, s1, s2)
        kaug_ref[:, 2 * g * LANES:(2 * g + 1) * LANES] = xr.astype(BF16)
        kaug_ref[:, (2 * g + 1) * LANES:(2 * g + 2) * LANES] = onehot


def _kprep(proj, gain, bd, tabs, seq):
    t = proj.shape[0]
    ts = min(512, seq)
    assert seq % ts == 0
    nk = N_KV * LANES
    kern = functools.partial(_kprep_kernel, ts=ts, seq=seq)
    tab = lambda: pl.BlockSpec((ts, LANES), lambda i: (i % (seq // ts), 0))
    return pl.pallas_call(
        kern,
        out_shape=jax.ShapeDtypeStruct((t, 2 * nk), BF16),
        grid=(t // ts,),
        in_specs=[
            pl.BlockSpec((ts, nk), lambda i: (i, 0)),
            pl.BlockSpec((1, LANES), lambda i: (0, 0)),
            pl.BlockSpec((LANES, LANES), lambda i: (0, 0)),
            tab(), tab(), tab(),
        ],
        out_specs=pl.BlockSpec((ts, 2 * nk), lambda i: (i, 0)),
        compiler_params=_params("parallel"),
        name="nsa_key_prep",
    )(proj, gain, bd, *tabs)


def _compress_kernel(kc_ref, vc_ref, w1k_ref, w1v_ref, ck_ref, cv_ref, w2k_ref, w2v_ref, gain_ref, o_ref, ot_ref,
                     acck_ref, accv_ref):
    l = pl.program_id(1)
    nc = kc_ref.shape[1]
    hid = w2k_ref.shape[0]

    @pl.when(l == 0)
    def _():
        acck_ref[...] = jnp.zeros_like(acck_ref)
        accv_ref[...] = jnp.zeros_like(accv_ref)

    acck_ref[...] += _dot(kc_ref[0].astype(BF16), w1k_ref[0])
    accv_ref[...] += _dot(vc_ref[0].astype(BF16), w1v_ref[0])

    @pl.when(l == pl.num_programs(1) - 1)
    def _():
        for g in range(N_KV):
            def hidden(acc_ref, c_ref):
                first = acc_ref[:, g * hid:(g + 1) * hid] + c_ref[0:1, :]
                second = acc_ref[:, (N_KV + g) * hid:(N_KV + g + 1) * hid] + c_ref[1:2, :]
                pre = first + pltpu.roll(second, nc - 1, 0)
                return (pre * _sigmoid(pre)).astype(BF16)

            kv = _dot(hidden(acck_ref, ck_ref), w2k_ref[...]) + _dot(hidden(accv_ref, cv_ref), w2v_ref[...])
            is_k = lax.broadcasted_iota(jnp.int32, kv.shape, 1) < HEAD_DIM
            ss = jnp.sum(jnp.where(is_k, kv * kv, 0.0), axis=-1, keepdims=True)
            kn = kv * lax.rsqrt(ss * (1.0 / HEAD_DIM) + EPS) * gain_ref[...]
            out = jnp.where(is_k, kn, kv)
            o_ref[g] = out.astype(BF16)
            ot_ref[g] = out.T.astype(BF16)


def _compress(proj, col_k, col_v, pe_k, pe_v, ck_w1, ck_w2, cv_w1, cv_w2, kc_g, batch, seq):
    nc = seq // CMP_STRIDE
    hid = ck_w1.shape[1]
    hk = N_KV * HEAD_DIM
    assert CMP_LEN == 2 * CMP_STRIDE and proj.shape[1] % hk == 0
    blocks_per_token = proj.shape[1] // hk
    proj3 = proj.reshape(batch, nc, CMP_STRIDE * proj.shape[1])
    pad = lambda w, left: jnp.pad(w, ((0, 0), (HEAD_DIM, 0) if left else (0, HEAD_DIM))).astype(BF16)
    gain = jnp.concatenate([kc_g, jnp.ones((HEAD_DIM,), F32)]).reshape(1, LANES)
    eye = jnp.eye(N_KV, dtype=F32)

    def first_layer(pe, w1):
        w = w1.reshape(2, CMP_STRIDE, HEAD_DIM, hid)
        bd = jnp.einsum("gk,sldh->lgdskh", eye, w).reshape(CMP_STRIDE, hk, 2 * N_KV * hid)
        const = jnp.einsum("sld,sldh->sh", pe.reshape(2, CMP_STRIDE, HEAD_DIM), w, precision=lax.Precision.HIGHEST)
        return bd.astype(BF16), const

    w1k, const_k = first_layer(pe_k, ck_w1)
    w1v, const_v = first_layer(pe_v, cv_w1)
    wide = 2 * N_KV * hid
    chunk = lambda col: pl.BlockSpec((1, nc, hk), lambda b, l: (b, 0, l * blocks_per_token + col))
    step_w = lambda: pl.BlockSpec((1, hk, wide), lambda b, l: (l, 0, 0))
    full = lambda shape: pl.BlockSpec(shape, lambda b, l: (0,) * len(shape))
    return pl.pallas_call(
        _compress_kernel,
        out_shape=(jax.ShapeDtypeStruct((batch * N_KV, nc, LANES), BF16),
                   jax.ShapeDtypeStruct((batch * N_KV, LANES, nc), BF16)),
        grid=(batch, CMP_STRIDE),
        in_specs=[chunk(col_k), chunk(col_v), step_w(), step_w(), full((2, hid)), full((2, hid)),
                  full((hid, LANES)), full((hid, LANES)), full((1, LANES))],
        out_specs=(pl.BlockSpec((N_KV, nc, LANES), lambda b, l: (b, 0, 0)),
                   pl.BlockSpec((N_KV, LANES, nc), lambda b, l: (b, 0, 0))),
        scratch_shapes=[pltpu.VMEM((nc, wide), F32), pltpu.VMEM((nc, wide), F32)],
        compiler_params=_params("parallel", "arbitrary"),
        name="nsa_compress",
    )(proj3, proj3, w1k, w1v, const_k, const_v, pad(ck_w2, False), pad(cv_w2, True), gain)


def _attn_kernel(qt_ref, gate_ref, kaug_ref, vt_ref, kvc_ref, kvct_ref, cmapt_ref, qg_ref, cos_ref, sin_ref,
                 o_ref,
                 qaug_ref, qwin_ref, oc_ref, ms_ref, ls_ref, accs_ref, mw_ref, lw_ref, accw_ref,
                 sa_ref, sb_ref, wdiag_ref, wold_ref, imp_ref,
                 *, tq, tk, k_top, n_cmp_var):
    qi = pl.program_id(2)
    t0 = qi * tq
    cols = GROUP * tq
    half = ROT_DIM // 2

    gain = qg_ref[...]
    cos, sin = cos_ref[...], sin_ref[...]
    nope, rope = [], []
    for r in range(GROUP):
        x = qt_ref[r * HEAD_DIM:(r + 1) * HEAD_DIM, :]
        ss = jnp.sum(x * x, axis=0, keepdims=True)
        xn = x * lax.rsqrt(ss * (1.0 / HEAD_DIM) + EPS) * gain
        x1, x2 = xn[0:half], xn[half:ROT_DIM]
        nope.append(xn)
        rope.append(jnp.concatenate([x1 * cos - x2 * sin, x2 * cos + x1 * sin, xn[ROT_DIM:]], axis=0))
    zeros = jnp.zeros((HEAD_DIM, cols), BF16)
    q_nope = jnp.concatenate([jnp.concatenate(nope, axis=1).astype(BF16), zeros], axis=0)
    q_rope = jnp.concatenate(rope, axis=1).astype(BF16)
    qwin_ref[...] = jnp.concatenate([zeros, q_rope], axis=0)

    tpos = t0 + lax.broadcasted_iota(jnp.int32, (1, tq), 1)

    def per_head(x):
        return jnp.concatenate([x] * GROUP, axis=1)

    def fold_tile(s, keep, k0, v_rows, m_ref, l_ref, acc_ref):
        if keep is not None:
            s = jnp.where(per_head(keep), s, NEG)
        m_prev = m_ref[...]
        m_new = jnp.maximum(m_prev, jnp.max(s, axis=0, keepdims=True))
        p = jnp.exp2(s - m_new)
        alpha = jnp.exp2(m_prev - m_new)
        l_ref[...] = alpha * l_ref[...] + jnp.sum(p, axis=0, keepdims=True)
        pb = p.astype(BF16)
        v_tile0 = k0 // V_TILE
        pv = _dot(vt_ref[0, v_tile0, 0, v_rows, :], pb[0:V_TILE])
        for c in range(1, s.shape[0] // V_TILE):
            pv = pv + _dot(vt_ref[0, v_tile0 + c, 0, v_rows, :], pb[c * V_TILE:(c + 1) * V_TILE])
        acc_ref[...] = alpha * acc_ref[...] + pv
        m_ref[...] = m_new

    def back(k0, size):
        return tpos - (k0 + lax.broadcasted_iota(jnp.int32, (size, 1), 0))

    def sel_scores(k0):
        return _dot(kaug_ref[pl.ds(k0, tk), :], qaug_ref[...])

    def win_scores(k0, size):
        return _dot(kaug_ref[pl.ds(k0, size), 0:LANES], qwin_ref[...])

    for m_ref, l_ref, acc_ref in ((ms_ref, ls_ref, accs_ref), (mw_ref, lw_ref, accw_ref)):
        m_ref[...] = jnp.full_like(m_ref, NEG)
        l_ref[...] = jnp.zeros_like(l_ref)
        acc_ref[...] = jnp.zeros_like(acc_ref)

    t0a = pl.multiple_of(t0, tq)
    k_old = pl.multiple_of(jnp.maximum(t0 - WINDOW, 0), tq)
    wdiag_ref[...] = win_scores(t0a, tq)
    wold_ref[...] = win_scores(k_old, WINDOW)
    qaug_ref[0:LANES, :] = jnp.concatenate([q_rope, zeros], axis=0)
    sa_ref[...] = _dot(kaug_ref[0:tk, 0:LANES], qaug_ref[0:LANES, :])
    win_state = (slice(HEAD_DIM, LANES), mw_ref, lw_ref, accw_ref)
    fold_tile(wdiag_ref[...], back(t0a, tq) >= 0, t0a, *win_state)
    b_old = back(k_old, WINDOW)
    fold_tile(wold_ref[...], (b_old < WINDOW) & (b_old > tpos - t0), k_old, *win_state)

    nc = kvc_ref.shape[1]

    def cmp_branch(rows):
        kvc = kvc_ref[0, 0:rows, :]
        cmp_end = lax.broadcasted_iota(jnp.int32, (rows, 1), 0) * CMP_STRIDE + (CMP_LEN - 1)
        s_c = jnp.where(per_head(cmp_end <= tpos), _dot(kvc, q_nope), NEG)
        e = jnp.exp2(s_c - jnp.maximum(jnp.max(s_c, axis=0, keepdims=True), 0.1 * NEG))
        p_c = e * (1.0 / jnp.maximum(jnp.sum(e, axis=0, keepdims=True), 1e-30))
        oc_ref[...] = _dot(kvct_ref[0, HEAD_DIM:, 0:rows], p_c.astype(BF16))
        p_sum = p_c[:, 0:tq]
        for r in range(1, GROUP):
            p_sum = p_sum + p_c[:, r * tq:(r + 1) * tq]
        hi, lo = _split_bf16(p_sum)
        imp_ref[...] = _dot(cmapt_ref[:, 0:rows], hi) + _dot(cmapt_ref[:, 0:rows], lo)

    variant = (t0 + tq - 1) // (CMP_STRIDE * (nc // n_cmp_var))
    for v in range(n_cmp_var):
        pl.when(variant == v)(functools.partial(cmp_branch, (v + 1) * (nc // n_cmp_var)))
    imp = imp_ref[...]

    blk = lax.broadcasted_iota(jnp.int32, (LANES, tq), 0)
    cur = tpos // SEL_LEN
    forced = (blk == 0) | (blk == cur) | (blk == cur - 1)
    taken = -3e38
    imp = jnp.where(blk > cur, -1.0, jnp.where(forced, taken, imp))
    blk_f = blk.astype(F32)
    for _ in range(max(k_top - 3, 0)):
        mx = jnp.max(imp, axis=0, keepdims=True)
        idx = jnp.min(jnp.where(imp == mx, blk_f, float(LANES)), axis=0, keepdims=True)
        imp = jnp.where(blk_f == idx, taken, imp)
    bias = jnp.where(imp == taken, 0.0, NEG)
    qaug_ref[LANES:, :] = per_head(bias.astype(BF16))
    blocks0 = tk // SEL_LEN
    bias0 = jnp.broadcast_to(bias[0:blocks0, None, :], (blocks0, SEL_LEN, tq)).reshape(tk, tq)
    sa_ref[...] = sa_ref[...] + per_head(bias0)


    n = t0 // tk
    sel_state = (slice(0, HEAD_DIM), ms_ref, ls_ref, accs_ref)

    def pair_body(i, carry):
        k0 = pl.multiple_of(2 * i * tk, tk)
        sb_ref[...] = sel_scores(k0 + tk)
        fold_tile(sa_ref[...], None, k0, *sel_state)
        sa_ref[...] = sel_scores(k0 + 2 * tk)
        fold_tile(sb_ref[...], None, k0 + tk, *sel_state)
        return carry

    lax.fori_loop(0, n // 2, pair_body, 0)
    k_diag = pl.multiple_of(n * tk, tk)

    @pl.when(n % 2 == 1)
    def _():
        sb_ref[...] = sel_scores(k_diag)
        fold_tile(sa_ref[...], None, k_diag - tk, *sel_state)
        fold_tile(sb_ref[...], back(k_diag, tk) >= 0, k_diag, *sel_state)

    @pl.when(n % 2 == 0)
    def _():
        fold_tile(sa_ref[...], back(k_diag, tk) >= 0, k_diag, *sel_state)

    o_s = accs_ref[...] / ls_ref[...]
    o_w = accw_ref[...] / lw_ref[...]
    o_c = oc_ref[...]
    sig = _sigmoid(gate_ref[...])
    heads = []
    for r in range(GROUP):
        cs = slice(r * tq, (r + 1) * tq)
        g_c, g_s, g_w = (sig[3 * r + b:3 * r + b + 1, :] for b in range(3))
        heads.append(g_c * o_c[:, cs] + g_s * o_s[:, cs] + g_w * o_w[:, cs])
    o_ref[...] = jnp.concatenate(heads, axis=0).T.astype(o_ref.dtype)


def _attention(qgt, kaug, vt, kvc, kvct, cmapt, qg, cos, sin, batch, seq):
    t = qgt.shape[1]
    tq = min(256, seq)
    tk = min(512, seq)
    nq = seq // tq
    nc = kvc.shape[1]
    assert seq % tk == 0 and tk % tq == 0 and WINDOW % tq == 0 and tq == V_TILE and seq >= WINDOW
    cols = GROUP * tq
    hq = GROUP * HEAD_DIM
    k_top = min(SEL_TOPK, seq // SEL_LEN)
    n_cmp_var = max(1, min(4, nc // LANES))
    assert k_top >= 3 and nc % n_cmp_var == 0
    kern = functools.partial(_attn_kernel, tq=tq, tk=tk, k_top=k_top, n_cmp_var=n_cmp_var)
    tab = lambda: pl.BlockSpec((ROT_DIM // 2, tq), lambda b, g, qi: (0, qi))
    stat = lambda: pltpu.VMEM((1, cols), F32)
    acc = lambda: pltpu.VMEM((HEAD_DIM, cols), F32)
    return pl.pallas_call(
        kern,
        out_shape=jax.ShapeDtypeStruct((t, N_HEADS * HEAD_DIM), BF16),
        grid=(batch, N_KV, nq),
        in_specs=[
            pl.BlockSpec((hq, tq), lambda b, g, qi: (g, b * nq + qi)),
            pl.BlockSpec((GATE_ROWS, tq), lambda b, g, qi: (N_HEADS * HEAD_DIM // GATE_ROWS + g, b * nq + qi)),
            pl.BlockSpec((seq, 2 * LANES), lambda b, g, qi: (b, g)),
            pl.BlockSpec((1, seq // V_TILE, 1, LANES, V_TILE), lambda b, g, qi: (b, 0, g, 0, 0)),
            pl.BlockSpec((1, nc, LANES), lambda b, g, qi: (b * N_KV + g, 0, 0)),
            pl.BlockSpec((1, LANES, nc), lambda b, g, qi: (b * N_KV + g, 0, 0)),
            pl.BlockSpec((LANES, nc), lambda b, g, qi: (0, 0)),
            pl.BlockSpec((HEAD_DIM, tq), lambda b, g, qi: (0, 0)),
            tab(), tab(),
        ],
        out_specs=pl.BlockSpec((tq, hq), lambda b, g, qi: (b * nq + qi, g)),
        scratch_shapes=[
            pltpu.VMEM((2 * LANES, cols), BF16), pltpu.VMEM((LANES, cols), BF16), acc(),
            stat(), stat(), acc(), stat(), stat(), acc(),
            pltpu.VMEM((tk, cols), F32), pltpu.VMEM((tk, cols), F32), pltpu.VMEM((tq, cols), F32), pltpu.VMEM((WINDOW, cols), F32),
            pltpu.VMEM((LANES, tq), F32),
        ],
        compiler_params=_params("parallel", "parallel", "arbitrary"),
        name="nsa_attention",
    )(qgt, qgt, kaug, vt, kvc, kvct, cmapt, qg, cos, sin)


def _rope_angles(seq):
    half = ROT_DIM // 2
    inv_freq = ROPE_THETA ** (-jnp.arange(half, dtype=F32) * (2.0 / ROT_DIM))
    ang = jnp.arange(seq, dtype=F32)[:, None] * inv_freq[None, :]
    return jnp.cos(ang), jnp.sin(ang)


def _rope_tables(cos, sin):
    seq, half = cos.shape
    zeros = jnp.zeros((seq, HEAD_DIM - ROT_DIM), F32)
    zh = jnp.zeros((seq, half), F32)
    c = jnp.concatenate([cos, cos, zeros + 1.0], axis=1)
    s1 = jnp.concatenate([-sin, zh, zeros], axis=1)
    s2 = jnp.concatenate([zh, sin, zeros], axis=1)
    reps = LANES // HEAD_DIM
    return tuple(jnp.tile(a, (1, reps)) for a in (c, s1, s2))


def _cmp_to_sel_t(nc, n_sel):
    start_c = np.arange(nc)[None, :] * CMP_STRIDE
    start_s = np.arange(LANES)[:, None] * SEL_LEN
    ov = np.minimum(start_c + CMP_LEN, start_s + SEL_LEN) - np.maximum(start_c, start_s)
    m = np.maximum(ov, 0).astype(np.float32) / CMP_LEN
    m[n_sel:, :] = 0.0
    m[:, nc - 1:] = 0.0
    return jnp.asarray(m, BF16)


def _nsa_in_weights(w_in):
    d = w_in.shape[0]
    hq, hk = N_HEADS * HEAD_DIM, N_KV * HEAD_DIM
    offs = np.cumsum([0, hq] + [hk] * 6)
    q, kc, vc, ks, vs, kw, vw = (w_in[:, offs[n]:offs[n + 1]] for n in range(7))
    gl = w_in[:, offs[7]:]
    per_group = lambda a, b: jnp.stack([a.reshape(d, N_KV, HEAD_DIM), b.reshape(d, N_KV, HEAD_DIM)],
                                       axis=2).reshape(d, N_KV * LANES)
    gates = jnp.pad(gl.reshape(d, N_KV, 3 * GROUP), ((0, 0), (0, 0), (0, GATE_ROWS - 3 * GROUP)))
    wn = jnp.concatenate([per_group(ks, kw), kc, vc], axis=1).astype(BF16)
    wt = jnp.concatenate([q, gates.reshape(d, N_KV * GATE_ROWS), per_group(vs, vw)], axis=1).T.astype(BF16)
    return wn, wt


def _nsa_layer(x, g, w_in, q_g, kc_g, ks_g, kw_g, pe_k, pe_v, ck_w1, ck_w2, cv_w1, cv_w2, w_out, batch, seq):
    nc = seq // CMP_STRIDE
    n_sel = seq // SEL_LEN
    assert n_sel <= LANES and seq % CMP_STRIDE == 0 and 3 * GROUP <= GATE_ROWS
    wn, wt = _nsa_in_weights(w_in)
    proj, qgt, vt = _in_proj(x, g, wn, wt, batch, seq)
    cos, sin = _rope_angles(seq)
    seg = np.arange(LANES) // HEAD_DIM
    bd = jnp.asarray(seg[:, None] == seg[None, :], BF16)
    kaug = _kprep(proj, jnp.concatenate([ks_g, kw_g]).reshape(1, LANES), bd, _rope_tables(cos, sin), seq)
    col_k = N_KV * LANES // (N_KV * HEAD_DIM)
    kvc, kvct = _compress(proj, col_k, col_k + 1, pe_k, pe_v, ck_w1, ck_w2, cv_w1, cv_w2, kc_g, batch, seq)
    tq = min(256, seq)
    qg = jnp.broadcast_to((q_g * float(HEAD_DIM ** -0.5 * np.log2(np.e)))[:, None], (HEAD_DIM, tq))
    attn = _attention(qgt, kaug, vt, kvc, kvct, _cmp_to_sel_t(nc, n_sel), qg, cos.T, sin.T, batch, seq)
    return _out_proj(attn, x, w_out)


def kernel(x, mix_norm_g, ffn_norm_g, conv_w_pw1, conv_b_pw1, conv_w_dw, conv_b_dw, conv_ln_g, conv_ln_b, conv_w_pw2, conv_b_pw2, nsa_w_in, nsa_q_norm, nsa_kc_norm, nsa_ks_norm, nsa_kw_norm, nsa_pe_k, nsa_pe_v, nsa_ck_w1, nsa_ck_w2, nsa_cv_w1, nsa_cv_w2, nsa_w_out, ffn_w_up, ffn_w_dw, ffn_b_dw, ffn_w_down):
    batch, seq, d = x.shape
    depth = mix_norm_g.shape[0]
    n_mixers = 2
    h = x.reshape(batch * seq, d)
    for i in range(depth):
        j = i // n_mixers
        if i % n_mixers == 0:
            u = _conformer_glu(h, mix_norm_g[i], conv_w_pw1[j], conv_b_pw1[j])
            h = _conformer_conv_out(u, h, conv_w_dw[j], conv_b_dw[j], conv_ln_g[j], conv_ln_b[j],
                                    conv_w_pw2[j], conv_b_pw2[j], seq)
        else:
            h = _nsa_layer(h, mix_norm_g[i], nsa_w_in[j], nsa_q_norm[j], nsa_kc_norm[j], nsa_ks_norm[j],
                           nsa_kw_norm[j], nsa_pe_k[j], nsa_pe_v[j], nsa_ck_w1[j], nsa_ck_w2[j], nsa_cv_w1[j],
                           nsa_cv_w2[j], nsa_w_out[j], batch, seq)
        h = _conv_ffn(h, ffn_norm_g[i], ffn_w_up[i], ffn_w_dw[i], ffn_b_dw[i], ffn_w_down[i], seq)
    return h.reshape(batch, seq, d)
```

```python
import functools

import numpy as np
import jax
import jax.numpy as jnp
from jax import lax
from jax.experimental import pallas as pl
from jax.experimental.pallas import tpu as pltpu

N_HEADS = 16
HEAD_DIM = 64
N_KV = 4
GROUP = N_HEADS // N_KV
ROT_DIM = HEAD_DIM // 4
ROPE_THETA = 500000.0
CMP_LEN = 32
CMP_STRIDE = 16
SEL_LEN = 64
SEL_TOPK = 16
WINDOW = 512
EPS = 1e-6
NEG = -1e30

LANES = 128
SUBLANES = 8
BF16_ROWS = 16
V_TILE = 256
Q_TILE = 256
GATE_ROWS = 16
VMEM_LIMIT = 48 * 1024 * 1024

F32 = jnp.float32
BF16 = jnp.bfloat16
NT_DIMS = (((1,), (1,)), ((), ()))


def _params(*sem):
    return pltpu.CompilerParams(dimension_semantics=sem, vmem_limit_bytes=VMEM_LIMIT)


def _dot(a, b):
    return jnp.dot(a, b, preferred_element_type=F32)


def _dot_nt(a, b):
    return lax.dot_general(a, b, NT_DIMS, preferred_element_type=F32)


def _split_bf16(x):
    hi = x.astype(BF16)
    lo = (x - hi.astype(F32)).astype(BF16)
    return hi, lo


def _rms_rows(x, g):
    ms = jnp.mean(x * x, axis=-1, keepdims=True)
    return x * lax.rsqrt(ms + EPS) * g


def _sigmoid(x):
    return 1.0 / (1.0 + jnp.exp(-x))


def _head_rms(xb, bd, gain):
    hi, lo = _split_bf16(xb * xb)
    ss = _dot(hi, bd) + _dot(lo, bd)
    return xb * lax.rsqrt(ss * (1.0 / HEAD_DIM) + EPS) * gain


def _rope(xb, c, s1, s2):
    half = ROT_DIM // 2
    return xb * c + pltpu.roll(xb, LANES - half, 1) * s1 + pltpu.roll(xb, half, 1) * s2


def _ffn_kernel(x_ref, halo_ref, g_ref, wup_ref, wdw_ref, bdw_ref, wd_ref, o_ref,
                hn_ref, a0_ref, v0_ref, a1_ref, v1_ref, *, tm, tiles_per_seq, nch):
    i = pl.program_id(0)
    hl = BF16_ROWS
    g = g_ref[...]
    hn_ref[hl:, :] = _rms_rows(x_ref[...], g).astype(BF16)
    first = (i % tiles_per_seq) == 0
    hn_ref[0:hl, :] = jnp.where(first, 0.0, _rms_rows(halo_ref[...], g)).astype(BF16)
    o_ref[...] = x_ref[...]

    tf = a0_ref.shape[1]

    def up(c, a_ref, v_ref):
        a_ref[...] = _dot(hn_ref[...], wup_ref[:, pl.ds(pl.multiple_of(c * tf, tf), tf)])
        v_ref[...] = _dot(hn_ref[hl:, :], wup_ref[:, pl.ds(pl.multiple_of((nch + c) * tf, tf), tf)])

    def down(c, a_ref, v_ref):
        w = wdw_ref[c]
        cv = (w[0:1] * a_ref[hl - 2:hl - 2 + tm, :] + w[1:2] * a_ref[hl - 1:hl - 1 + tm, :]
              + w[2:3] * a_ref[hl:hl + tm, :] + bdw_ref[c])
        act = (cv * _sigmoid(cv) * v_ref[...]).astype(BF16)
        o_ref[...] += _dot(act, wd_ref[c])

    buf0, buf1 = (a0_ref, v0_ref), (a1_ref, v1_ref)
    up(0, *buf0)

    def pair_body(p, carry):
        c = 2 * p
        up(c + 1, *buf1)
        down(c, *buf0)
        up(c + 2, *buf0)
        down(c + 1, *buf1)
        return carry

    lax.fori_loop(0, (nch - 1) // 2, pair_body, 0)
    if nch % 2 == 1:
        down(nch - 1, *buf0)
    else:
        up(nch - 1, *buf1)
        down(nch - 2, *buf0)
        down(nch - 1, *buf1)


def _conv_ffn(x, g, w_up_all, w_dw, b_dw, w_down_all, layer, seq):
    t, d = x.shape
    dff = w_down_all.shape[1]
    tm = min(512, seq)
    tf = 256
    assert seq % tm == 0 and dff % tf == 0 and w_dw.shape[0] == 3
    nch = dff // tf
    hl = BF16_ROWS
    kern = functools.partial(_ffn_kernel, tm=tm, tiles_per_seq=seq // tm, nch=nch)
    wup = w_up_all
    wdn = w_down_all.reshape(w_down_all.shape[0], nch, tf, d)
    wdw = w_dw.reshape(3, nch, tf).transpose(1, 0, 2)
    bdw = b_dw.reshape(nch, 1, tf)
    resident = lambda a: pl.BlockSpec(a.shape, lambda i: (0,) * a.ndim, pipeline_mode=pl.Buffered(1))
    of_layer = lambda a: pl.BlockSpec((None,) + a.shape[1:], lambda i: (layer,) + (0,) * (a.ndim - 1),
                                      pipeline_mode=pl.Buffered(1))
    return pl.pallas_call(
        kern,
        out_shape=jax.ShapeDtypeStruct((t, d), F32),
        grid=(t // tm,),
        in_specs=[
            pl.BlockSpec((tm, d), lambda i: (i, 0)),
            pl.BlockSpec((hl, d), lambda i: (jnp.maximum(i * (tm // hl) - 1, 0), 0)),
            pl.BlockSpec((1, d), lambda i: (0, 0)),
            of_layer(wup), resident(wdw), resident(bdw), of_layer(wdn),
        ],
        out_specs=pl.BlockSpec((tm, d), lambda i: (i, 0)),
        scratch_shapes=[
            pltpu.VMEM((tm + hl, d), BF16),
            pltpu.VMEM((tm + hl, tf), F32), pltpu.VMEM((tm, tf), F32),
            pltpu.VMEM((tm + hl, tf), F32), pltpu.VMEM((tm, tf), F32),
        ],
        compiler_params=_params("parallel"),
        name="conv_ffn",
    )(x, x, g.reshape(1, d), wup, wdw, bdw, wdn)


def _glu_kernel(x_ref, g_ref, wa_ref, wg_ref, ba_ref, bg_ref, o_ref, hn_ref):
    @pl.when(pl.program_id(1) == 0)
    def _():
        hn_ref[...] = _rms_rows(x_ref[...], g_ref[...]).astype(BF16)

    th = hn_ref.shape[0] // 2
    halves = [(_dot(hn_ref[h * th:(h + 1) * th, :], wa_ref[...]), _dot(hn_ref[h * th:(h + 1) * th, :], wg_ref[...]))
              for h in range(2)]
    for h, (a, gate) in enumerate(halves):
        o_ref[h * th:(h + 1) * th, :] = (a + ba_ref[...]) * _sigmoid(gate + bg_ref[...])


def _conformer_glu(x, g, w_pw1, b_pw1):
    t, d = x.shape
    tm = min(1024, t)
    tn = 256
    nj = d // tn
    w = w_pw1.astype(BF16)
    b = b_pw1.reshape(1, 2 * d)
    return pl.pallas_call(
        _glu_kernel,
        out_shape=jax.ShapeDtypeStruct((t, d), F32),
        grid=(t // tm, nj),
        in_specs=[
            pl.BlockSpec((tm, d), lambda i, j: (i, 0)),
            pl.BlockSpec((1, d), lambda i, j: (0, 0)),
            pl.BlockSpec((d, tn), lambda i, j: (0, j)),
            pl.BlockSpec((d, tn), lambda i, j: (0, j + nj)),
            pl.BlockSpec((1, tn), lambda i, j: (0, j)),
            pl.BlockSpec((1, tn), lambda i, j: (0, j + nj)),
        ],
        out_specs=pl.BlockSpec((tm, tn), lambda i, j: (i, j)),
        scratch_shapes=[pltpu.VMEM((tm, d), BF16)],
        compiler_params=_params("parallel", "arbitrary"),
        name="conformer_glu",
    )(x, g.reshape(1, d), w, w, b, b)


def _dwconv_kernel(u_ref, halo_ref, x_ref, wdw_ref, bdw_ref, lng_ref, lnb_ref, w2_ref, b2_ref, o_ref,
                   ext_ref, cv_ref, slab_ref, *, tm, halo, width, tiles_per_seq, row_chunk, col_chunk):
    i = pl.program_id(0)
    d = u_ref.shape[1]
    first = (i % tiles_per_seq) == 0
    ext_ref[0:halo, :] = jnp.where(first, 0.0, halo_ref[...])
    ext_ref[halo:, :] = u_ref[...]
    off = halo - (width - 1)
    for r0 in range(0, tm, row_chunk):
        for c0 in range(0, d, col_chunk):
            cs = slice(c0, c0 + col_chunk)
            acc = jnp.broadcast_to(bdw_ref[:, cs], (row_chunk, col_chunk))
            for r in range(min(SUBLANES, width)):
                taps = range(r, width, SUBLANES)
                lo = r0 + off + r
                rows = row_chunk + (len(taps) - 1) * SUBLANES
                slab_ref[r, 0:rows, :] = ext_ref[lo:lo + rows, cs]
                for q, k in enumerate(taps):
                    acc = acc + wdw_ref[k:k + 1, cs] * slab_ref[r, q * SUBLANES:q * SUBLANES + row_chunk, :]
            cv_ref[r0:r0 + row_chunk, cs] = acc
    u = cv_ref[...]
    mu = jnp.mean(u, axis=-1, keepdims=True)
    uc = u - mu
    var = jnp.mean(uc * uc, axis=-1, keepdims=True)
    y = uc * lax.rsqrt(var + EPS) * lng_ref[...] + lnb_ref[...]
    s = (y * _sigmoid(y)).astype(BF16)
    o_ref[...] = x_ref[...] + _dot(s, w2_ref[...]) + b2_ref[...]


def _conformer_conv_out(u, x, w_dw, b_dw, ln_g, ln_b, w_pw2, b_pw2, seq):
    t, d = x.shape
    width = w_dw.shape[0]
    halo = 32
    assert width - 1 <= halo
    tm = min(256, seq)
    assert seq % tm == 0 and tm % halo == 0
    row_chunk, col_chunk = 64, 256
    kern = functools.partial(_dwconv_kernel, tm=tm, halo=halo, width=width, tiles_per_seq=seq // tm,
                             row_chunk=row_chunk, col_chunk=col_chunk)
    slab_rows = row_chunk + (width - 1) // SUBLANES * SUBLANES
    vec = lambda: pl.BlockSpec((1, d), lambda i: (0, 0))
    return pl.pallas_call(
        kern,
        out_shape=jax.ShapeDtypeStruct((t, d), F32),
        grid=(t // tm,),
        in_specs=[
            pl.BlockSpec((tm, d), lambda i: (i, 0)),
            pl.BlockSpec((halo, d), lambda i: (jnp.maximum(i * (tm // halo) - 1, 0), 0)),
            pl.BlockSpec((tm, d), lambda i: (i, 0)),
            pl.BlockSpec((width, d), lambda i: (0, 0)),
            vec(), vec(), vec(),
            pl.BlockSpec((d, d), lambda i: (0, 0)),
            vec(),
        ],
        out_specs=pl.BlockSpec((tm, d), lambda i: (i, 0)),
        scratch_shapes=[pltpu.VMEM((tm + halo, d), F32), pltpu.VMEM((tm, d), F32),
                        pltpu.VMEM((SUBLANES, slab_rows, col_chunk), F32)],
        compiler_params=_params("parallel"),
        name="conformer_dwconv_out",
    )(u, u, x, w_dw, b_dw.reshape(1, d), ln_g.reshape(1, d), ln_b.reshape(1, d), w_pw2.astype(BF16),
      b_pw2.reshape(1, d))


def _in_proj_kernel(x_ref, g_ref, wn_ref, wt_ref, perm_ref, on_ref, oc_ref, oq_ref, ov_ref, *, n_qg, row_chunk):
    hn = _rms_rows(x_ref[...], g_ref[...]).astype(BF16)
    nk = on_ref.shape[1]
    on_ref[...] = _dot(hn, wn_ref[:, 0:nk])
    kcvc = _dot(hn, wn_ref[:, nk:]).astype(BF16)
    regrouped = _dot(perm_ref[...], kcvc).astype(BF16)
    per = regrouped.shape[0] // CMP_STRIDE
    for l in range(CMP_STRIDE):
        oc_ref[0, l] = regrouped[l * per:(l + 1) * per, :]
    for r0 in range(0, n_qg, row_chunk):
        r1 = min(r0 + row_chunk, n_qg)
        oq_ref[r0:r1, :] = _dot_nt(wt_ref[r0:r1, :], hn)
    vt = _dot_nt(wt_ref[n_qg:, :], hn).astype(BF16)
    for jj in range(ov_ref.shape[1]):
        for g in range(N_KV):
            ov_ref[0, jj, g] = vt[g * LANES:(g + 1) * LANES, jj * V_TILE:(jj + 1) * V_TILE]


def _in_proj(x, g, wn, wt, batch, seq):
    t, d = x.shape
    tm = min(512, seq)
    n_qg = N_HEADS * HEAD_DIM + N_KV * GATE_ROWS
    assert seq % tm == 0 and tm % V_TILE == 0 and wt.shape[0] == n_qg + N_KV * LANES
    per_seq = seq // tm
    nk = N_KV * LANES
    ncv = wn.shape[1] - nk
    per = tm // CMP_STRIDE
    rows = np.arange(tm)
    perm = np.zeros((tm, tm), np.float32)
    perm[(rows % CMP_STRIDE) * per + rows // CMP_STRIDE, rows] = 1.0
    kern = functools.partial(_in_proj_kernel, n_qg=n_qg, row_chunk=512)
    return pl.pallas_call(
        kern,
        out_shape=(jax.ShapeDtypeStruct((t, nk), F32),
                   jax.ShapeDtypeStruct((t // tm, CMP_STRIDE, per, ncv), BF16),
                   jax.ShapeDtypeStruct((n_qg, t), F32),
                   jax.ShapeDtypeStruct((batch, seq // V_TILE, N_KV, LANES, V_TILE), BF16)),
        grid=(t // tm,),
        in_specs=[
            pl.BlockSpec((tm, d), lambda i: (i, 0)),
            pl.BlockSpec((1, d), lambda i: (0, 0)),
            pl.BlockSpec(wn.shape, lambda i: (0, 0)),
            pl.BlockSpec(wt.shape, lambda i: (0, 0)),
            pl.BlockSpec((tm, tm), lambda i: (0, 0)),
        ],
        out_specs=(pl.BlockSpec((tm, nk), lambda i: (i, 0)),
                   pl.BlockSpec((1, CMP_STRIDE, per, ncv), lambda i: (i, 0, 0, 0)),
                   pl.BlockSpec((n_qg, tm), lambda i: (0, i)),
                   pl.BlockSpec((1, tm // V_TILE, N_KV, LANES, V_TILE),
                                lambda i: (i // per_seq, i % per_seq, 0, 0, 0))),
        compiler_params=_params("parallel"),
        name="nsa_in_proj",
    )(x, g.reshape(1, d), wn, wt, jnp.asarray(perm, BF16))


def _out_proj_kernel(a_ref, x_ref, w_ref, o_ref):
    o_ref[...] = x_ref[...] + _dot(a_ref[...], w_ref[...])


def _out_proj(a, x, w):
    t, d = x.shape
    tm = min(512, t)
    return pl.pallas_call(
        _out_proj_kernel,
        out_shape=jax.ShapeDtypeStruct((t, d), F32),
        grid=(t // tm,),
        in_specs=[
            pl.BlockSpec((tm, a.shape[1]), lambda i: (i, 0)),
            pl.BlockSpec((tm, d), lambda i: (i, 0)),
            pl.BlockSpec(w.shape, lambda i: (0, 0)),
        ],
        out_specs=pl.BlockSpec((tm, d), lambda i: (i, 0)),
        compiler_params=_params("parallel"),
        name="nsa_out_proj",
    )(a, x, w.astype(BF16))


def _kprep_kernel(k_ref, gain_ref, bd_ref, c_ref, s1_ref, s2_ref, kaug_ref, *, ts, seq):
    i = pl.program_id(0)
    bd = bd_ref[...]
    gain = gain_ref[...]
    c, s1, s2 = c_ref[...], s1_ref[...], s2_ref[...]
    tpos = (i * ts) % seq + lax.broadcasted_iota(jnp.int32, (ts, LANES), 0)
    lane = lax.broadcasted_iota(jnp.int32, (ts, LANES), 1)
    onehot = jnp.where(tpos // SEL_LEN == lane, 1.0, 0.0).astype(BF16)
    for g in range(N_KV):
        xb = k_ref[:, g * LANES:(g + 1) * LANES]
        xr = _rope(_head_rms(xb, bd, gain), c, s1, s2)
        kaug_ref[:, 2 * g * LANES:(2 * g + 1) * LANES] = xr.astype(BF16)
        kaug_ref[:, (2 * g + 1) * LANES:(2 * g + 2) * LANES] = onehot


def _kprep(proj, gain, bd, tabs, seq):
    t = proj.shape[0]
    ts = min(512, seq)
    assert seq % ts == 0
    nk = N_KV * LANES
    kern = functools.partial(_kprep_kernel, ts=ts, seq=seq)
    tab = lambda: pl.BlockSpec((ts, LANES), lambda i: (i % (seq // ts), 0))
    return pl.pallas_call(
        kern,
        out_shape=jax.ShapeDtypeStruct((t, 2 * nk), BF16),
        grid=(t // ts,),
        in_specs=[
            pl.BlockSpec((ts, nk), lambda i: (i, 0)),
            pl.BlockSpec((1, LANES), lambda i: (0, 0)),
            pl.BlockSpec((LANES, LANES), lambda i: (0, 0)),
            tab(), tab(), tab(),
        ],
        out_specs=pl.BlockSpec((ts, 2 * nk), lambda i: (i, 0)),
        compiler_params=_params("parallel"),
        name="nsa_key_prep",
    )(proj, gain, bd, *tabs)


def _compress_kernel(x_ref, w1k_ref, w1v_ref, ck_ref, cv_ref, w2k_ref, w2v_ref, gain_ref, o_ref, ot_ref,
                     acck_ref, accv_ref):
    l = pl.program_id(1)
    nc = acck_ref.shape[0]
    hid = w2k_ref.shape[0]
    hk = w1k_ref.shape[1]

    @pl.when(l == 0)
    def _():
        acck_ref[...] = jnp.zeros_like(acck_ref)
        accv_ref[...] = jnp.zeros_like(accv_ref)

    x = x_ref[0].reshape(nc, 2 * hk)
    acck_ref[...] += _dot(x[:, 0:hk], w1k_ref[0])
    accv_ref[...] += _dot(x[:, hk:], w1v_ref[0])

    @pl.when(l == pl.num_programs(1) - 1)
    def _():
        for g in range(N_KV):
            def hidden(acc_ref, c_ref):
                first = acc_ref[:, g * hid:(g + 1) * hid] + c_ref[0:1, :]
                second = acc_ref[:, (N_KV + g) * hid:(N_KV + g + 1) * hid] + c_ref[1:2, :]
                pre = first + pltpu.roll(second, nc - 1, 0)
                return (pre * _sigmoid(pre)).astype(BF16)

            kv = _dot(hidden(acck_ref, ck_ref), w2k_ref[...]) + _dot(hidden(accv_ref, cv_ref), w2v_ref[...])
            is_k = lax.broadcasted_iota(jnp.int32, kv.shape, 1) < HEAD_DIM
            ss = jnp.sum(jnp.where(is_k, kv * kv, 0.0), axis=-1, keepdims=True)
            kn = kv * lax.rsqrt(ss * (1.0 / HEAD_DIM) + EPS) * gain_ref[...]
            out = jnp.where(is_k, kn, kv)
            o_ref[g] = out.astype(BF16)
            ot_ref[g] = out.T.astype(BF16)


def _compress(kcvc, pe_k, pe_v, ck_w1, ck_w2, cv_w1, cv_w2, kc_g, batch, seq):
    nc = seq // CMP_STRIDE
    hid = ck_w1.shape[1]
    hk = N_KV * HEAD_DIM
    tiles, _, per, width = kcvc.shape
    assert CMP_LEN == 2 * CMP_STRIDE and width == 2 * hk and tiles * per == batch * nc
    x5 = kcvc.reshape(batch, tiles // batch, CMP_STRIDE, per, width)
    pad = lambda w, left: jnp.pad(w, ((0, 0), (HEAD_DIM, 0) if left else (0, HEAD_DIM))).astype(BF16)
    gain = jnp.concatenate([kc_g, jnp.ones((HEAD_DIM,), F32)]).reshape(1, LANES)

    def first_layer(pe, w1):
        w = w1.reshape(2, CMP_STRIDE, HEAD_DIM, hid)
        rows = lambda a, k: jnp.pad(a, ((0, 0), (k * HEAD_DIM, (N_KV - 1 - k) * HEAD_DIM), (0, 0)))
        bd = jnp.concatenate([rows(w[s], k) for s in range(2) for k in range(N_KV)], axis=2)
        const = jnp.einsum("sld,sldh->sh", pe.reshape(2, CMP_STRIDE, HEAD_DIM), w, precision=lax.Precision.HIGHEST)
        return bd.astype(BF16), const

    w1k, const_k = first_layer(pe_k, ck_w1)
    w1v, const_v = first_layer(pe_v, cv_w1)
    wide = 2 * N_KV * hid
    step_w = lambda: pl.BlockSpec((1, hk, wide), lambda b, l: (l, 0, 0))
    full = lambda shape: pl.BlockSpec(shape, lambda b, l: (0,) * len(shape))
    return pl.pallas_call(
        _compress_kernel,
        out_shape=(jax.ShapeDtypeStruct((batch * N_KV, nc, LANES), BF16),
                   jax.ShapeDtypeStruct((batch * N_KV, LANES, nc), BF16)),
        grid=(batch, CMP_STRIDE),
        in_specs=[pl.BlockSpec((1, tiles // batch, None, per, width), lambda b, l: (b, 0, l, 0, 0)),
                  step_w(), step_w(), full((2, hid)), full((2, hid)),
                  full((hid, LANES)), full((hid, LANES)), full((1, LANES))],
        out_specs=(pl.BlockSpec((N_KV, nc, LANES), lambda b, l: (b, 0, 0)),
                   pl.BlockSpec((N_KV, LANES, nc), lambda b, l: (b, 0, 0))),
        scratch_shapes=[pltpu.VMEM((nc, wide), F32), pltpu.VMEM((nc, wide), F32)],
        compiler_params=_params("parallel", "arbitrary"),
        name="nsa_compress",
    )(x5, w1k, w1v, const_k, const_v, pad(ck_w2, False), pad(cv_w2, True), gain)


def _attn_kernel(qt_ref, gate_ref, kaug_ref, vt_ref, kvc_ref, kvct_ref, cmapt_ref, qg_ref, cos_ref, sin_ref,
                 o_ref,
                 qaug_ref, qwin_ref, oc_ref, ms_ref, ls_ref, accs_ref, mw_ref, lw_ref, accw_ref,
                 sa_ref, sb_ref, wdiag_ref, wold_ref, imp_ref,
                 *, tq, tk, k_top, n_cmp_var):
    qi = pl.program_id(2)
    t0 = qi * tq
    cols = GROUP * tq
    half = ROT_DIM // 2

    gain = qg_ref[...]
    cos, sin = cos_ref[...], sin_ref[...]
    nope, rope = [], []
    for r in range(GROUP):
        x = qt_ref[r * HEAD_DIM:(r + 1) * HEAD_DIM, :]
        ss = jnp.sum(x * x, axis=0, keepdims=True)
        xn = x * lax.rsqrt(ss * (1.0 / HEAD_DIM) + EPS) * gain
        x1, x2 = xn[0:half], xn[half:ROT_DIM]
        nope.append(xn)
        rope.append(jnp.concatenate([x1 * cos - x2 * sin, x2 * cos + x1 * sin, xn[ROT_DIM:]], axis=0))
    zeros = jnp.zeros((HEAD_DIM, cols), BF16)
    q_nope = jnp.concatenate([jnp.concatenate(nope, axis=1).astype(BF16), zeros], axis=0)
    q_rope = jnp.concatenate(rope, axis=1).astype(BF16)
    qwin_ref[...] = jnp.concatenate([zeros, q_rope], axis=0)

    tpos = t0 + lax.broadcasted_iota(jnp.int32, (1, tq), 1)

    def per_head(x):
        return jnp.concatenate([x] * GROUP, axis=1)

    def fold_tile(s, keep, k0, v_rows, m_ref, l_ref, acc_ref):
        if keep is not None:
            s = jnp.where(per_head(keep), s, NEG)
        m_prev = m_ref[...]
        m_new = jnp.maximum(m_prev, jnp.max(s, axis=0, keepdims=True))
        p = jnp.exp2(s - m_new)
        alpha = jnp.exp2(m_prev - m_new)
        l_ref[...] = alpha * l_ref[...] + jnp.sum(p, axis=0, keepdims=True)
        pb = p.astype(BF16)
        v_tile0 = k0 // V_TILE
        pv = _dot(vt_ref[0, v_tile0, 0, v_rows, :], pb[0:V_TILE])
        for c in range(1, s.shape[0] // V_TILE):
            pv = pv + _dot(vt_ref[0, v_tile0 + c, 0, v_rows, :], pb[c * V_TILE:(c + 1) * V_TILE])
        acc_ref[...] = alpha * acc_ref[...] + pv
        m_ref[...] = m_new

    def back(k0, size):
        return tpos - (k0 + lax.broadcasted_iota(jnp.int32, (size, 1), 0))

    def sel_scores(k0):
        return _dot(kaug_ref[pl.ds(k0, tk), :], qaug_ref[...])

    def win_scores(k0, size):
        return _dot(kaug_ref[pl.ds(k0, size), 0:LANES], qwin_ref[...])

    for m_ref, l_ref, acc_ref in ((ms_ref, ls_ref, accs_ref), (mw_ref, lw_ref, accw_ref)):
        m_ref[...] = jnp.full_like(m_ref, NEG)
        l_ref[...] = jnp.zeros_like(l_ref)
        acc_ref[...] = jnp.zeros_like(acc_ref)

    t0a = pl.multiple_of(t0, tq)
    k_old = pl.multiple_of(jnp.maximum(t0 - WINDOW, 0), tq)
    wdiag_ref[...] = win_scores(t0a, tq)
    wold_ref[...] = win_scores(k_old, WINDOW)
    qaug_ref[0:LANES, :] = jnp.concatenate([q_rope, zeros], axis=0)
    sa_ref[...] = _dot(kaug_ref[0:tk, 0:LANES], qaug_ref[0:LANES, :])
    win_state = (slice(HEAD_DIM, LANES), mw_ref, lw_ref, accw_ref)
    fold_tile(wdiag_ref[...], back(t0a, tq) >= 0, t0a, *win_state)
    b_old = back(k_old, WINDOW)
    fold_tile(wold_ref[...], (b_old < WINDOW) & (b_old > tpos - t0), k_old, *win_state)

    nc = kvc_ref.shape[1]

    def cmp_branch(rows):
        kvc = kvc_ref[0, 0:rows, :]
        cmp_end = lax.broadcasted_iota(jnp.int32, (rows, 1), 0) * CMP_STRIDE + (CMP_LEN - 1)
        s_c = jnp.where(per_head(cmp_end <= tpos), _dot(kvc, q_nope), NEG)
        e = jnp.exp2(s_c - jnp.maximum(jnp.max(s_c, axis=0, keepdims=True), 0.1 * NEG))
        p_c = e * (1.0 / jnp.maximum(jnp.sum(e, axis=0, keepdims=True), 1e-30))
        oc_ref[...] = _dot(kvct_ref[0, HEAD_DIM:, 0:rows], p_c.astype(BF16))
        p_sum = p_c[:, 0:tq]
        for r in range(1, GROUP):
            p_sum = p_sum + p_c[:, r * tq:(r + 1) * tq]
        hi, lo = _split_bf16(p_sum)
        imp_ref[...] = _dot(cmapt_ref[:, 0:rows], hi) + _dot(cmapt_ref[:, 0:rows], lo)

    variant = (t0 + tq - 1) // (CMP_STRIDE * (nc // n_cmp_var))
    for v in range(n_cmp_var):
        pl.when(variant == v)(functools.partial(cmp_branch, (v + 1) * (nc // n_cmp_var)))
    imp = imp_ref[...]

    blk = lax.broadcasted_iota(jnp.int32, (LANES, tq), 0)
    cur = tpos // SEL_LEN
    forced = (blk == 0) | (blk == cur) | (blk == cur - 1)
    taken = -3e38
    imp = jnp.where(blk > cur, -1.0, jnp.where(forced, taken, imp))
    blk_f = blk.astype(F32)
    for _ in range(max(k_top - 3, 0)):
        mx = jnp.max(imp, axis=0, keepdims=True)
        idx = jnp.min(jnp.where(imp == mx, blk_f, float(LANES)), axis=0, keepdims=True)
        imp = jnp.where(blk_f == idx, taken, imp)
    bias = jnp.where(imp == taken, 0.0, NEG)
    qaug_ref[LANES:, :] = per_head(bias.astype(BF16))
    blocks0 = tk // SEL_LEN
    bias0 = jnp.broadcast_to(bias[0:blocks0, None, :], (blocks0, SEL_LEN, tq)).reshape(tk, tq)
    sa_ref[...] = sa_ref[...] + per_head(bias0)


    n = t0 // tk
    sel_state = (slice(0, HEAD_DIM), ms_ref, ls_ref, accs_ref)

    def pair_body(i, carry):
        k0 = pl.multiple_of(2 * i * tk, tk)
        sb_ref[...] = sel_scores(k0 + tk)
        fold_tile(sa_ref[...], None, k0, *sel_state)
        sa_ref[...] = sel_scores(k0 + 2 * tk)
        fold_tile(sb_ref[...], None, k0 + tk, *sel_state)
        return carry

    lax.fori_loop(0, n // 2, pair_body, 0)
    k_diag = pl.multiple_of(n * tk, tk)

    @pl.when(n % 2 == 1)
    def _():
        sb_ref[...] = sel_scores(k_diag)
        fold_tile(sa_ref[...], None, k_diag - tk, *sel_state)
        fold_tile(sb_ref[...], back(k_diag, tk) >= 0, k_diag, *sel_state)

    @pl.when(n % 2 == 0)
    def _():
        fold_tile(sa_ref[...], back(k_diag, tk) >= 0, k_diag, *sel_state)

    o_s = accs_ref[...] / ls_ref[...]
    o_w = accw_ref[...] / lw_ref[...]
    o_c = oc_ref[...]
    sig = _sigmoid(gate_ref[...])
    heads = []
    for r in range(GROUP):
        cs = slice(r * tq, (r + 1) * tq)
        g_c, g_s, g_w = (sig[3 * r + b:3 * r + b + 1, :] for b in range(3))
        heads.append(g_c * o_c[:, cs] + g_s * o_s[:, cs] + g_w * o_w[:, cs])
    o_ref[...] = jnp.concatenate(heads, axis=0).T.astype(o_ref.dtype)


def _attention(qgt, kaug, vt, kvc, kvct, cmapt, qg, cos, sin, batch, seq):
    t = qgt.shape[1]
    tq = min(Q_TILE, seq)
    tk = min(512, seq)
    nq = seq // tq
    nc = kvc.shape[1]
    assert seq % tk == 0 and tk % tq == 0 and WINDOW % tq == 0 and tq % V_TILE == 0 and seq >= WINDOW
    cols = GROUP * tq
    hq = GROUP * HEAD_DIM
    k_top = min(SEL_TOPK, seq // SEL_LEN)
    n_cmp_var = max(1, min(4, nc // LANES))
    assert k_top >= 3 and nc % n_cmp_var == 0
    kern = functools.partial(_attn_kernel, tq=tq, tk=tk, k_top=k_top, n_cmp_var=n_cmp_var)
    tab = lambda: pl.BlockSpec((ROT_DIM // 2, tq), lambda b, g, qi: (0, qi))
    stat = lambda: pltpu.VMEM((1, cols), F32)
    acc = lambda: pltpu.VMEM((HEAD_DIM, cols), F32)
    return pl.pallas_call(
        kern,
        out_shape=jax.ShapeDtypeStruct((t, N_HEADS * HEAD_DIM), BF16),
        grid=(batch, N_KV, nq),
        in_specs=[
            pl.BlockSpec((hq, tq), lambda b, g, qi: (g, b * nq + qi)),
            pl.BlockSpec((GATE_ROWS, tq), lambda b, g, qi: (N_HEADS * HEAD_DIM // GATE_ROWS + g, b * nq + qi)),
            pl.BlockSpec((seq, 2 * LANES), lambda b, g, qi: (b, g)),
            pl.BlockSpec((1, seq // V_TILE, 1, LANES, V_TILE), lambda b, g, qi: (b, 0, g, 0, 0)),
            pl.BlockSpec((1, nc, LANES), lambda b, g, qi: (b * N_KV + g, 0, 0)),
            pl.BlockSpec((1, LANES, nc), lambda b, g, qi: (b * N_KV + g, 0, 0)),
            pl.BlockSpec((LANES, nc), lambda b, g, qi: (0, 0)),
            pl.BlockSpec((HEAD_DIM, tq), lambda b, g, qi: (0, 0)),
            tab(), tab(),
        ],
        out_specs=pl.BlockSpec((tq, hq), lambda b, g, qi: (b * nq + qi, g)),
        scratch_shapes=[
            pltpu.VMEM((2 * LANES, cols), BF16), pltpu.VMEM((LANES, cols), BF16), acc(),
            stat(), stat(), acc(), stat(), stat(), acc(),
            pltpu.VMEM((tk, cols), F32), pltpu.VMEM((tk, cols), F32), pltpu.VMEM((tq, cols), F32), pltpu.VMEM((WINDOW, cols), F32),
            pltpu.VMEM((LANES, tq), F32),
        ],
        compiler_params=_params("parallel", "parallel", "arbitrary"),
        name="nsa_attention",
    )(qgt, qgt, kaug, vt, kvc, kvct, cmapt, qg, cos, sin)


def _rope_angles(seq):
    half = ROT_DIM // 2
    inv_freq = ROPE_THETA ** (-jnp.arange(half, dtype=F32) * (2.0 / ROT_DIM))
    ang = jnp.arange(seq, dtype=F32)[:, None] * inv_freq[None, :]
    return jnp.cos(ang), jnp.sin(ang)


def _rope_tables(cos, sin):
    seq, half = cos.shape
    zeros = jnp.zeros((seq, HEAD_DIM - ROT_DIM), F32)
    zh = jnp.zeros((seq, half), F32)
    c = jnp.concatenate([cos, cos, zeros + 1.0], axis=1)
    s1 = jnp.concatenate([-sin, zh, zeros], axis=1)
    s2 = jnp.concatenate([zh, sin, zeros], axis=1)
    reps = LANES // HEAD_DIM
    return tuple(jnp.tile(a, (1, reps)) for a in (c, s1, s2))


def _cmp_to_sel_t(nc, n_sel):
    start_c = np.arange(nc)[None, :] * CMP_STRIDE
    start_s = np.arange(LANES)[:, None] * SEL_LEN
    ov = np.minimum(start_c + CMP_LEN, start_s + SEL_LEN) - np.maximum(start_c, start_s)
    m = np.maximum(ov, 0).astype(np.float32) / CMP_LEN
    m[n_sel:, :] = 0.0
    m[:, nc - 1:] = 0.0
    return jnp.asarray(m, BF16)


def _nsa_in_weights(w_in):
    d = w_in.shape[0]
    hq, hk = N_HEADS * HEAD_DIM, N_KV * HEAD_DIM
    offs = np.cumsum([0, hq] + [hk] * 6)
    q, kc, vc, ks, vs, kw, vw = (w_in[:, offs[n]:offs[n + 1]] for n in range(7))
    gl = w_in[:, offs[7]:]
    per_group = lambda a, b: jnp.stack([a.reshape(d, N_KV, HEAD_DIM), b.reshape(d, N_KV, HEAD_DIM)],
                                       axis=2).reshape(d, N_KV * LANES)
    gates = jnp.pad(gl.reshape(d, N_KV, 3 * GROUP), ((0, 0), (0, 0), (0, GATE_ROWS - 3 * GROUP)))
    wn = jnp.concatenate([per_group(ks, kw), kc, vc], axis=1).astype(BF16)
    wt = jnp.concatenate([q, gates.reshape(d, N_KV * GATE_ROWS), per_group(vs, vw)], axis=1).T.astype(BF16)
    return wn, wt


def _nsa_layer(x, g, w_in, q_g, kc_g, ks_g, kw_g, pe_k, pe_v, ck_w1, ck_w2, cv_w1, cv_w2, w_out, batch, seq):
    nc = seq // CMP_STRIDE
    n_sel = seq // SEL_LEN
    assert n_sel <= LANES and seq % CMP_STRIDE == 0 and 3 * GROUP <= GATE_ROWS
    wn, wt = _nsa_in_weights(w_in)
    proj, kcvc, qgt, vt = _in_proj(x, g, wn, wt, batch, seq)
    cos, sin = _rope_angles(seq)
    seg = np.arange(LANES) // HEAD_DIM
    bd = jnp.asarray(seg[:, None] == seg[None, :], BF16)
    kaug = _kprep(proj, jnp.concatenate([ks_g, kw_g]).reshape(1, LANES), bd, _rope_tables(cos, sin), seq)
    kvc, kvct = _compress(kcvc, pe_k, pe_v, ck_w1, ck_w2, cv_w1, cv_w2, kc_g, batch, seq)
    tq = min(Q_TILE, seq)
    qg = jnp.broadcast_to((q_g * float(HEAD_DIM ** -0.5 * np.log2(np.e)))[:, None], (HEAD_DIM, tq))
    attn = _attention(qgt, kaug, vt, kvc, kvct, _cmp_to_sel_t(nc, n_sel), qg, cos.T, sin.T, batch, seq)
    return _out_proj(attn, x, w_out)


def kernel(x, mix_norm_g, ffn_norm_g, conv_w_pw1, conv_b_pw1, conv_w_dw, conv_b_dw, conv_ln_g, conv_ln_b, conv_w_pw2, conv_b_pw2, nsa_w_in, nsa_q_norm, nsa_kc_norm, nsa_ks_norm, nsa_kw_norm, nsa_pe_k, nsa_pe_v, nsa_ck_w1, nsa_ck_w2, nsa_cv_w1, nsa_cv_w2, nsa_w_out, ffn_w_up, ffn_w_dw, ffn_b_dw, ffn_w_down):
    batch, seq, d = x.shape
    depth = mix_norm_g.shape[0]
    n_mixers = 2
    h = x.reshape(batch * seq, d)
    w_up_all, w_down_all = ffn_w_up.astype(BF16), ffn_w_down.astype(BF16)
    for i in range(depth):
        j = i // n_mixers
        if i % n_mixers == 0:
            u = _conformer_glu(h, mix_norm_g[i], conv_w_pw1[j], conv_b_pw1[j])
            h = _conformer_conv_out(u, h, conv_w_dw[j], conv_b_dw[j], conv_ln_g[j], conv_ln_b[j],
                                    conv_w_pw2[j], conv_b_pw2[j], seq)
        else:
            h = _nsa_layer(h, mix_norm_g[i], nsa_w_in[j], nsa_q_norm[j], nsa_kc_norm[j], nsa_ks_norm[j],
                           nsa_kw_norm[j], nsa_pe_k[j], nsa_pe_v[j], nsa_ck_w1[j], nsa_ck_w2[j], nsa_cv_w1[j],
                           nsa_cv_w2[j], nsa_w_out[j], batch, seq)
        h = _conv_ffn(h, ffn_norm_g[i], w_up_all, ffn_w_dw[i], ffn_b_dw[i], w_down_all, i, seq)
    return h.reshape(batch, seq, d)
```

```python
import functools

import numpy as np
import jax
import jax.numpy as jnp
from jax import lax
from jax.experimental import pallas as pl
from jax.experimental.pallas import tpu as pltpu

N_HEADS = 16
HEAD_DIM = 64
N_KV = 4
GROUP = N_HEADS // N_KV
ROT_DIM = HEAD_DIM // 4
ROPE_THETA = 500000.0
CMP_LEN = 32
CMP_STRIDE = 16
SEL_LEN = 64
SEL_TOPK = 16
WINDOW = 512
EPS = 1e-6
NEG = -1e30

LANES = 128
SUBLANES = 8
BF16_ROWS = 16
V_TILE = 256
Q_TILE = 256
V_ROWS = HEAD_DIM + BF16_ROWS
GATE_ROWS = 16
VMEM_LIMIT = 48 * 1024 * 1024

F32 = jnp.float32
BF16 = jnp.bfloat16
NT_DIMS = (((1,), (1,)), ((), ()))


def _params(*sem):
    return pltpu.CompilerParams(dimension_semantics=sem, vmem_limit_bytes=VMEM_LIMIT)


def _dot(a, b):
    return jnp.dot(a, b, preferred_element_type=F32)


def _dot_nt(a, b):
    return lax.dot_general(a, b, NT_DIMS, preferred_element_type=F32)


def _split_bf16(x):
    hi = x.astype(BF16)
    lo = (x - hi.astype(F32)).astype(BF16)
    return hi, lo


def _rms_rows(x, g):
    ms = jnp.mean(x * x, axis=-1, keepdims=True)
    return x * lax.rsqrt(ms + EPS) * g


def _sigmoid(x):
    return 1.0 / (1.0 + jnp.exp(-x))


def _head_rms(xb, bd, gain):
    hi, lo = _split_bf16(xb * xb)
    ss = _dot(hi, bd) + _dot(lo, bd)
    return xb * lax.rsqrt(ss * (1.0 / HEAD_DIM) + EPS) * gain


def _rope(xb, c, s1, s2):
    half = ROT_DIM // 2
    return xb * c + pltpu.roll(xb, LANES - half, 1) * s1 + pltpu.roll(xb, half, 1) * s2


def _ffn_kernel(x_ref, halo_ref, g_ref, wup_ref, wdw_ref, bdw_ref, wd_ref, o_ref,
                hn_ref, a0_ref, v0_ref, a1_ref, v1_ref, *, tm, tiles_per_seq, nch):
    i = pl.program_id(0)
    hl = BF16_ROWS
    g = g_ref[...]
    hn_ref[hl:, :] = _rms_rows(x_ref[...], g).astype(BF16)
    first = (i % tiles_per_seq) == 0
    hn_ref[0:hl, :] = jnp.where(first, 0.0, _rms_rows(halo_ref[...], g)).astype(BF16)
    o_ref[...] = x_ref[...]

    tf = a0_ref.shape[1]

    def up(c, a_ref, v_ref):
        a_ref[...] = _dot(hn_ref[...], wup_ref[:, pl.ds(pl.multiple_of(c * tf, tf), tf)])
        v_ref[...] = _dot(hn_ref[hl:, :], wup_ref[:, pl.ds(pl.multiple_of((nch + c) * tf, tf), tf)])

    def down(c, a_ref, v_ref):
        w = wdw_ref[c]
        cv = (w[0:1] * a_ref[hl - 2:hl - 2 + tm, :] + w[1:2] * a_ref[hl - 1:hl - 1 + tm, :]
              + w[2:3] * a_ref[hl:hl + tm, :] + bdw_ref[c])
        act = (cv * _sigmoid(cv) * v_ref[...]).astype(BF16)
        o_ref[...] += _dot(act, wd_ref[c])

    buf0, buf1 = (a0_ref, v0_ref), (a1_ref, v1_ref)
    up(0, *buf0)

    def pair_body(p, carry):
        c = 2 * p
        up(c + 1, *buf1)
        down(c, *buf0)
        up(c + 2, *buf0)
        down(c + 1, *buf1)
        return carry

    lax.fori_loop(0, (nch - 1) // 2, pair_body, 0)
    if nch % 2 == 1:
        down(nch - 1, *buf0)
    else:
        up(nch - 1, *buf1)
        down(nch - 2, *buf0)
        down(nch - 1, *buf1)


def _conv_ffn(x, g, w_up_all, w_dw, b_dw, w_down_all, layer, seq):
    t, d = x.shape
    dff = w_down_all.shape[1]
    tm = min(512, seq)
    tf = 256
    assert seq % tm == 0 and dff % tf == 0 and w_dw.shape[0] == 3
    nch = dff // tf
    hl = BF16_ROWS
    kern = functools.partial(_ffn_kernel, tm=tm, tiles_per_seq=seq // tm, nch=nch)
    wup = w_up_all
    wdn = w_down_all.reshape(w_down_all.shape[0], nch, tf, d)
    wdw = w_dw.reshape(3, nch, tf).transpose(1, 0, 2)
    bdw = b_dw.reshape(nch, 1, tf)
    resident = lambda a: pl.BlockSpec(a.shape, lambda i: (0,) * a.ndim, pipeline_mode=pl.Buffered(1))
    of_layer = lambda a: pl.BlockSpec((None,) + a.shape[1:], lambda i: (layer,) + (0,) * (a.ndim - 1),
                                      pipeline_mode=pl.Buffered(1))
    return pl.pallas_call(
        kern,
        out_shape=jax.ShapeDtypeStruct((t, d), F32),
        grid=(t // tm,),
        in_specs=[
            pl.BlockSpec((tm, d), lambda i: (i, 0)),
            pl.BlockSpec((hl, d), lambda i: (jnp.maximum(i * (tm // hl) - 1, 0), 0)),
            pl.BlockSpec((1, d), lambda i: (0, 0)),
            of_layer(wup), resident(wdw), resident(bdw), of_layer(wdn),
        ],
        out_specs=pl.BlockSpec((tm, d), lambda i: (i, 0)),
        scratch_shapes=[
            pltpu.VMEM((tm + hl, d), BF16),
            pltpu.VMEM((tm + hl, tf), F32), pltpu.VMEM((tm, tf), F32),
            pltpu.VMEM((tm + hl, tf), F32), pltpu.VMEM((tm, tf), F32),
        ],
        compiler_params=_params("parallel"),
        name="conv_ffn",
    )(x, x, g.reshape(1, d), wup, wdw, bdw, wdn)


def _glu_kernel(x_ref, g_ref, wa_ref, wg_ref, ba_ref, bg_ref, o_ref, hn_ref):
    @pl.when(pl.program_id(1) == 0)
    def _():
        hn_ref[...] = _rms_rows(x_ref[...], g_ref[...]).astype(BF16)

    th = hn_ref.shape[0] // 2
    halves = [(_dot(hn_ref[h * th:(h + 1) * th, :], wa_ref[...]), _dot(hn_ref[h * th:(h + 1) * th, :], wg_ref[...]))
              for h in range(2)]
    for h, (a, gate) in enumerate(halves):
        o_ref[h * th:(h + 1) * th, :] = (a + ba_ref[...]) * _sigmoid(gate + bg_ref[...])


def _conformer_glu(x, g, w_pw1, b_pw1):
    t, d = x.shape
    tm = min(1024, t)
    tn = 256
    nj = d // tn
    w = w_pw1.astype(BF16)
    b = b_pw1.reshape(1, 2 * d)
    return pl.pallas_call(
        _glu_kernel,
        out_shape=jax.ShapeDtypeStruct((t, d), F32),
        grid=(t // tm, nj),
        in_specs=[
            pl.BlockSpec((tm, d), lambda i, j: (i, 0)),
            pl.BlockSpec((1, d), lambda i, j: (0, 0)),
            pl.BlockSpec((d, tn), lambda i, j: (0, j)),
            pl.BlockSpec((d, tn), lambda i, j: (0, j + nj)),
            pl.BlockSpec((1, tn), lambda i, j: (0, j)),
            pl.BlockSpec((1, tn), lambda i, j: (0, j + nj)),
        ],
        out_specs=pl.BlockSpec((tm, tn), lambda i, j: (i, j)),
        scratch_shapes=[pltpu.VMEM((tm, d), BF16)],
        compiler_params=_params("parallel", "arbitrary"),
        name="conformer_glu",
    )(x, g.reshape(1, d), w, w, b, b)


def _dwconv_kernel(u_ref, halo_ref, x_ref, wdw_ref, bdw_ref, lng_ref, lnb_ref, w2_ref, b2_ref, o_ref,
                   ext_ref, cv_ref, slab_ref, *, tm, halo, width, tiles_per_seq, row_chunk, col_chunk):
    i = pl.program_id(0)
    d = u_ref.shape[1]
    first = (i % tiles_per_seq) == 0
    ext_ref[0:halo, :] = jnp.where(first, 0.0, halo_ref[...])
    ext_ref[halo:, :] = u_ref[...]
    off = halo - (width - 1)
    for r0 in range(0, tm, row_chunk):
        for c0 in range(0, d, col_chunk):
            cs = slice(c0, c0 + col_chunk)
            acc = jnp.broadcast_to(bdw_ref[:, cs], (row_chunk, col_chunk))
            for r in range(min(SUBLANES, width)):
                taps = range(r, width, SUBLANES)
                lo = r0 + off + r
                rows = row_chunk + (len(taps) - 1) * SUBLANES
                slab_ref[r, 0:rows, :] = ext_ref[lo:lo + rows, cs]
                for q, k in enumerate(taps):
                    acc = acc + wdw_ref[k:k + 1, cs] * slab_ref[r, q * SUBLANES:q * SUBLANES + row_chunk, :]
            cv_ref[r0:r0 + row_chunk, cs] = acc
    u = cv_ref[...]
    mu = jnp.mean(u, axis=-1, keepdims=True)
    uc = u - mu
    var = jnp.mean(uc * uc, axis=-1, keepdims=True)
    y = uc * lax.rsqrt(var + EPS) * lng_ref[...] + lnb_ref[...]
    s = (y * _sigmoid(y)).astype(BF16)
    o_ref[...] = x_ref[...] + _dot(s, w2_ref[...]) + b2_ref[...]


def _conformer_conv_out(u, x, w_dw, b_dw, ln_g, ln_b, w_pw2, b_pw2, seq):
    t, d = x.shape
    width = w_dw.shape[0]
    halo = 32
    assert width - 1 <= halo
    tm = min(256, seq)
    assert seq % tm == 0 and tm % halo == 0
    row_chunk, col_chunk = 64, 256
    kern = functools.partial(_dwconv_kernel, tm=tm, halo=halo, width=width, tiles_per_seq=seq // tm,
                             row_chunk=row_chunk, col_chunk=col_chunk)
    slab_rows = row_chunk + (width - 1) // SUBLANES * SUBLANES
    vec = lambda: pl.BlockSpec((1, d), lambda i: (0, 0))
    return pl.pallas_call(
        kern,
        out_shape=jax.ShapeDtypeStruct((t, d), F32),
        grid=(t // tm,),
        in_specs=[
            pl.BlockSpec((tm, d), lambda i: (i, 0)),
            pl.BlockSpec((halo, d), lambda i: (jnp.maximum(i * (tm // halo) - 1, 0), 0)),
            pl.BlockSpec((tm, d), lambda i: (i, 0)),
            pl.BlockSpec((width, d), lambda i: (0, 0)),
            vec(), vec(), vec(),
            pl.BlockSpec((d, d), lambda i: (0, 0)),
            vec(),
        ],
        out_specs=pl.BlockSpec((tm, d), lambda i: (i, 0)),
        scratch_shapes=[pltpu.VMEM((tm + halo, d), F32), pltpu.VMEM((tm, d), F32),
                        pltpu.VMEM((SUBLANES, slab_rows, col_chunk), F32)],
        compiler_params=_params("parallel"),
        name="conformer_dwconv_out",
    )(u, u, x, w_dw, b_dw.reshape(1, d), ln_g.reshape(1, d), ln_b.reshape(1, d), w_pw2.astype(BF16),
      b_pw2.reshape(1, d))


def _in_proj_kernel(x_ref, g_ref, wn_ref, wt_ref, perm_ref, on_ref, oc_ref, oq_ref, ov_ref, *, n_qg, row_chunk):
    hn = _rms_rows(x_ref[...], g_ref[...]).astype(BF16)
    nk = on_ref.shape[1]
    on_ref[...] = _dot(hn, wn_ref[:, 0:nk])
    kcvc = _dot(hn, wn_ref[:, nk:]).astype(BF16)
    regrouped = _dot(perm_ref[...], kcvc).astype(BF16)
    per = regrouped.shape[0] // CMP_STRIDE
    for l in range(CMP_STRIDE):
        oc_ref[0, l] = regrouped[l * per:(l + 1) * per, :]
    for r0 in range(0, n_qg, row_chunk):
        r1 = min(r0 + row_chunk, n_qg)
        oq_ref[r0:r1, :] = _dot_nt(wt_ref[r0:r1, :], hn)
    vt = _dot_nt(wt_ref[n_qg:, :], hn).astype(BF16)
    ones = jnp.ones((BF16_ROWS, V_TILE), BF16)
    for jj in range(ov_ref.shape[1]):
        for g in range(N_KV):
            for half in range(2):
                r0 = g * LANES + half * HEAD_DIM
                ov_ref[0, jj, g, half * V_ROWS:half * V_ROWS + HEAD_DIM, :] = (
                    vt[r0:r0 + HEAD_DIM, jj * V_TILE:(jj + 1) * V_TILE])
                ov_ref[0, jj, g, half * V_ROWS + HEAD_DIM:(half + 1) * V_ROWS, :] = ones


def _in_proj(x, g, wn, wt, batch, seq):
    t, d = x.shape
    tm = min(512, seq)
    n_qg = N_HEADS * HEAD_DIM + N_KV * GATE_ROWS
    assert seq % tm == 0 and tm % V_TILE == 0 and wt.shape[0] == n_qg + N_KV * LANES
    per_seq = seq // tm
    nk = N_KV * LANES
    ncv = wn.shape[1] - nk
    per = tm // CMP_STRIDE
    rows = np.arange(tm)
    perm = np.zeros((tm, tm), np.float32)
    perm[(rows % CMP_STRIDE) * per + rows // CMP_STRIDE, rows] = 1.0
    kern = functools.partial(_in_proj_kernel, n_qg=n_qg, row_chunk=512)
    return pl.pallas_call(
        kern,
        out_shape=(jax.ShapeDtypeStruct((t, nk), F32),
                   jax.ShapeDtypeStruct((t // tm, CMP_STRIDE, per, ncv), BF16),
                   jax.ShapeDtypeStruct((n_qg, t), F32),
                   jax.ShapeDtypeStruct((batch, seq // V_TILE, N_KV, 2 * V_ROWS, V_TILE), BF16)),
        grid=(t // tm,),
        in_specs=[
            pl.BlockSpec((tm, d), lambda i: (i, 0)),
            pl.BlockSpec((1, d), lambda i: (0, 0)),
            pl.BlockSpec(wn.shape, lambda i: (0, 0)),
            pl.BlockSpec(wt.shape, lambda i: (0, 0)),
            pl.BlockSpec((tm, tm), lambda i: (0, 0)),
        ],
        out_specs=(pl.BlockSpec((tm, nk), lambda i: (i, 0)),
                   pl.BlockSpec((1, CMP_STRIDE, per, ncv), lambda i: (i, 0, 0, 0)),
                   pl.BlockSpec((n_qg, tm), lambda i: (0, i)),
                   pl.BlockSpec((1, tm // V_TILE, N_KV, 2 * V_ROWS, V_TILE),
                                lambda i: (i // per_seq, i % per_seq, 0, 0, 0))),
        compiler_params=_params("parallel"),
        name="nsa_in_proj",
    )(x, g.reshape(1, d), wn, wt, jnp.asarray(perm, BF16))


def _out_proj_kernel(a_ref, x_ref, w_ref, o_ref):
    o_ref[...] = x_ref[...] + _dot(a_ref[...], w_ref[...])


def _out_proj(a, x, w):
    t, d = x.shape
    tm = min(512, t)
    return pl.pallas_call(
        _out_proj_kernel,
        out_shape=jax.ShapeDtypeStruct((t, d), F32),
        grid=(t // tm,),
        in_specs=[
            pl.BlockSpec((tm, a.shape[1]), lambda i: (i, 0)),
            pl.BlockSpec((tm, d), lambda i: (i, 0)),
            pl.BlockSpec(w.shape, lambda i: (0, 0)),
        ],
        out_specs=pl.BlockSpec((tm, d), lambda i: (i, 0)),
        compiler_params=_params("parallel"),
        name="nsa_out_proj",
    )(a, x, w.astype(BF16))


def _kprep_kernel(k_ref, gain_ref, bd_ref, c_ref, s1_ref, s2_ref, kaug_ref, *, ts, seq):
    i = pl.program_id(0)
    bd = bd_ref[...]
    gain = gain_ref[...]
    c, s1, s2 = c_ref[...], s1_ref[...], s2_ref[...]
    tpos = (i * ts) % seq + lax.broadcasted_iota(jnp.int32, (ts, LANES), 0)
    lane = lax.broadcasted_iota(jnp.int32, (ts, LANES), 1)
    onehot = jnp.where(tpos // SEL_LEN == lane, 1.0, 0.0).astype(BF16)
    for g in range(N_KV):
        xb = k_ref[:, g * LANES:(g + 1) * LANES]
        xr = _rope(_head_rms(xb, bd, gain), c, s1, s2)
        kaug_ref[:, 2 * g * LANES:(2 * g + 1) * LANES] = xr.astype(BF16)
        kaug_ref[:, (2 * g + 1) * LANES:(2 * g + 2) * LANES] = onehot


def _kprep(proj, gain, bd, tabs, seq):
    t = proj.shape[0]
    ts = min(512, seq)
    assert seq % ts == 0
    nk = N_KV * LANES
    kern = functools.partial(_kprep_kernel, ts=ts, seq=seq)
    tab = lambda: pl.BlockSpec((ts, LANES), lambda i: (i % (seq // ts), 0))
    return pl.pallas_call(
        kern,
        out_shape=jax.ShapeDtypeStruct((t, 2 * nk), BF16),
        grid=(t // ts,),
        in_specs=[
            pl.BlockSpec((ts, nk), lambda i: (i, 0)),
            pl.BlockSpec((1, LANES), lambda i: (0, 0)),
            pl.BlockSpec((LANES, LANES), lambda i: (0, 0)),
            tab(), tab(), tab(),
        ],
        out_specs=pl.BlockSpec((ts, 2 * nk), lambda i: (i, 0)),
        compiler_params=_params("parallel"),
        name="nsa_key_prep",
    )(proj, gain, bd, *tabs)


def _compress_kernel(x_ref, w1k_ref, w1v_ref, ck_ref, cv_ref, w2k_ref, w2v_ref, gain_ref, o_ref, ot_ref,
                     acck_ref, accv_ref):
    l = pl.program_id(1)
    nc = acck_ref.shape[0]
    hid = w2k_ref.shape[0]
    hk = w1k_ref.shape[1]

    @pl.when(l == 0)
    def _():
        acck_ref[...] = jnp.zeros_like(acck_ref)
        accv_ref[...] = jnp.zeros_like(accv_ref)

    x = x_ref[0].reshape(nc, 2 * hk)
    acck_ref[...] += _dot(x[:, 0:hk], w1k_ref[0])
    accv_ref[...] += _dot(x[:, hk:], w1v_ref[0])

    @pl.when(l == pl.num_programs(1) - 1)
    def _():
        for g in range(N_KV):
            def hidden(acc_ref, c_ref):
                first = acc_ref[:, g * hid:(g + 1) * hid] + c_ref[0:1, :]
                second = acc_ref[:, (N_KV + g) * hid:(N_KV + g + 1) * hid] + c_ref[1:2, :]
                pre = first + pltpu.roll(second, nc - 1, 0)
                return (pre * _sigmoid(pre)).astype(BF16)

            kv = _dot(hidden(acck_ref, ck_ref), w2k_ref[...]) + _dot(hidden(accv_ref, cv_ref), w2v_ref[...])
            is_k = lax.broadcasted_iota(jnp.int32, kv.shape, 1) < HEAD_DIM
            ss = jnp.sum(jnp.where(is_k, kv * kv, 0.0), axis=-1, keepdims=True)
            kn = kv * lax.rsqrt(ss * (1.0 / HEAD_DIM) + EPS) * gain_ref[...]
            out = jnp.where(is_k, kn, kv)
            o_ref[g] = out.astype(BF16)
            ot_ref[g] = out.T.astype(BF16)


def _compress(kcvc, pe_k, pe_v, ck_w1, ck_w2, cv_w1, cv_w2, kc_g, batch, seq):
    nc = seq // CMP_STRIDE
    hid = ck_w1.shape[1]
    hk = N_KV * HEAD_DIM
    tiles, _, per, width = kcvc.shape
    assert CMP_LEN == 2 * CMP_STRIDE and width == 2 * hk and tiles * per == batch * nc
    x5 = kcvc.reshape(batch, tiles // batch, CMP_STRIDE, per, width)
    pad = lambda w, left: jnp.pad(w, ((0, 0), (HEAD_DIM, 0) if left else (0, HEAD_DIM))).astype(BF16)
    gain = jnp.concatenate([kc_g, jnp.ones((HEAD_DIM,), F32)]).reshape(1, LANES)

    def first_layer(pe, w1):
        w = w1.reshape(2, CMP_STRIDE, HEAD_DIM, hid)
        rows = lambda a, k: jnp.pad(a, ((0, 0), (k * HEAD_DIM, (N_KV - 1 - k) * HEAD_DIM), (0, 0)))
        bd = jnp.concatenate([rows(w[s], k) for s in range(2) for k in range(N_KV)], axis=2)
        const = jnp.einsum("sld,sldh->sh", pe.reshape(2, CMP_STRIDE, HEAD_DIM), w, precision=lax.Precision.HIGHEST)
        return bd.astype(BF16), const

    w1k, const_k = first_layer(pe_k, ck_w1)
    w1v, const_v = first_layer(pe_v, cv_w1)
    wide = 2 * N_KV * hid
    step_w = lambda: pl.BlockSpec((1, hk, wide), lambda b, l: (l, 0, 0))
    full = lambda shape: pl.BlockSpec(shape, lambda b, l: (0,) * len(shape))
    return pl.pallas_call(
        _compress_kernel,
        out_shape=(jax.ShapeDtypeStruct((batch * N_KV, nc, LANES), BF16),
                   jax.ShapeDtypeStruct((batch * N_KV, LANES, nc), BF16)),
        grid=(batch, CMP_STRIDE),
        in_specs=[pl.BlockSpec((1, tiles // batch, None, per, width), lambda b, l: (b, 0, l, 0, 0)),
                  step_w(), step_w(), full((2, hid)), full((2, hid)),
                  full((hid, LANES)), full((hid, LANES)), full((1, LANES))],
        out_specs=(pl.BlockSpec((N_KV, nc, LANES), lambda b, l: (b, 0, 0)),
                   pl.BlockSpec((N_KV, LANES, nc), lambda b, l: (b, 0, 0))),
        scratch_shapes=[pltpu.VMEM((nc, wide), F32), pltpu.VMEM((nc, wide), F32)],
        compiler_params=_params("parallel", "arbitrary"),
        name="nsa_compress",
    )(x5, w1k, w1v, const_k, const_v, pad(ck_w2, False), pad(cv_w2, True), gain)


def _attn_kernel(qt_ref, gate_ref, kaug_ref, vt_ref, kvc_ref, kvct_ref, cmapt_ref, qg_ref, cos_ref, sin_ref,
                 o_ref,
                 qaug_ref, qwin_ref, oc_ref, ms_ref, ls_ref, accs_ref, mw_ref, lw_ref, accw_ref,
                 sa_ref, sb_ref, wdiag_ref, wold_ref, imp_ref,
                 *, tq, tk, k_top, n_cmp_var):
    qi = pl.program_id(2)
    t0 = qi * tq
    cols = GROUP * tq
    half = ROT_DIM // 2

    gain = qg_ref[...]
    cos, sin = cos_ref[...], sin_ref[...]
    nope, rope = [], []
    for r in range(GROUP):
        x = qt_ref[r * HEAD_DIM:(r + 1) * HEAD_DIM, :]
        ss = jnp.sum(x * x, axis=0, keepdims=True)
        xn = x * lax.rsqrt(ss * (1.0 / HEAD_DIM) + EPS) * gain
        x1, x2 = xn[0:half], xn[half:ROT_DIM]
        nope.append(xn)
        rope.append(jnp.concatenate([x1 * cos - x2 * sin, x2 * cos + x1 * sin, xn[ROT_DIM:]], axis=0))
    zeros = jnp.zeros((HEAD_DIM, cols), BF16)
    q_nope = jnp.concatenate([jnp.concatenate(nope, axis=1).astype(BF16), zeros], axis=0)
    q_rope = jnp.concatenate(rope, axis=1).astype(BF16)
    qwin_ref[...] = jnp.concatenate([zeros, q_rope], axis=0)

    tpos = t0 + lax.broadcasted_iota(jnp.int32, (1, tq), 1)

    def per_head(x):
        return jnp.concatenate([x] * GROUP, axis=1)

    def fold_tile(s, keep, k0, v_rows, m_ref, l_ref, acc_ref):
        if keep is not None:
            s = jnp.where(per_head(keep), s, NEG)
        m_prev = m_ref[...]
        m_new = jnp.maximum(m_prev, jnp.max(s, axis=0, keepdims=True))
        p = jnp.exp2(s - m_new)
        alpha = jnp.exp2(m_prev - m_new)
        pb = p.astype(BF16)
        v_tile0 = k0 // V_TILE
        pv = _dot(vt_ref[0, v_tile0, 0, v_rows, :], pb[0:V_TILE])
        for c in range(1, s.shape[0] // V_TILE):
            pv = pv + _dot(vt_ref[0, v_tile0 + c, 0, v_rows, :], pb[c * V_TILE:(c + 1) * V_TILE])
        l_ref[...] = alpha * l_ref[...] + pv[HEAD_DIM:HEAD_DIM + 1]
        acc_ref[...] = alpha * acc_ref[...] + pv[0:HEAD_DIM]
        m_ref[...] = m_new

    def back(k0, size):
        return tpos - (k0 + lax.broadcasted_iota(jnp.int32, (size, 1), 0))

    def sel_scores(k0):
        return _dot(kaug_ref[pl.ds(k0, tk), :], qaug_ref[...])

    def win_scores(k0, size):
        return _dot(kaug_ref[pl.ds(k0, size), 0:LANES], qwin_ref[...])

    for m_ref, l_ref, acc_ref in ((ms_ref, ls_ref, accs_ref), (mw_ref, lw_ref, accw_ref)):
        m_ref[...] = jnp.full_like(m_ref, NEG)
        l_ref[...] = jnp.zeros_like(l_ref)
        acc_ref[...] = jnp.zeros_like(acc_ref)

    t0a = pl.multiple_of(t0, tq)
    k_old = pl.multiple_of(jnp.maximum(t0 - WINDOW, 0), tq)
    wdiag_ref[...] = win_scores(t0a, tq)
    wold_ref[...] = win_scores(k_old, WINDOW)
    qaug_ref[0:LANES, :] = jnp.concatenate([q_rope, zeros], axis=0)
    sa_ref[...] = _dot(kaug_ref[0:tk, 0:LANES], qaug_ref[0:LANES, :])
    win_state = (slice(V_ROWS, 2 * V_ROWS), mw_ref, lw_ref, accw_ref)
    fold_tile(wdiag_ref[...], back(t0a, tq) >= 0, t0a, *win_state)
    b_old = back(k_old, WINDOW)
    fold_tile(wold_ref[...], (b_old < WINDOW) & (b_old > tpos - t0), k_old, *win_state)

    nc = kvc_ref.shape[1]

    def cmp_branch(rows):
        kvc = kvc_ref[0, 0:rows, :]
        cmp_end = lax.broadcasted_iota(jnp.int32, (rows, 1), 0) * CMP_STRIDE + (CMP_LEN - 1)
        s_c = jnp.where(per_head(cmp_end <= tpos), _dot(kvc, q_nope), NEG)
        e = jnp.exp2(s_c - jnp.maximum(jnp.max(s_c, axis=0, keepdims=True), 0.1 * NEG))
        p_c = e * (1.0 / jnp.maximum(jnp.sum(e, axis=0, keepdims=True), 1e-30))
        oc_ref[...] = _dot(kvct_ref[0, HEAD_DIM:, 0:rows], p_c.astype(BF16))
        p_sum = p_c[:, 0:tq]
        for r in range(1, GROUP):
            p_sum = p_sum + p_c[:, r * tq:(r + 1) * tq]
        hi, lo = _split_bf16(p_sum)
        imp_ref[...] = _dot(cmapt_ref[:, 0:rows], hi) + _dot(cmapt_ref[:, 0:rows], lo)

    variant = (t0 + tq - 1) // (CMP_STRIDE * (nc // n_cmp_var))
    for v in range(n_cmp_var):
        pl.when(variant == v)(functools.partial(cmp_branch, (v + 1) * (nc // n_cmp_var)))
    imp = imp_ref[...]

    blk = lax.broadcasted_iota(jnp.int32, (LANES, tq), 0)
    cur = tpos // SEL_LEN
    forced = (blk == 0) | (blk == cur) | (blk == cur - 1)
    taken = -3e38
    imp = jnp.where(blk > cur, -1.0, jnp.where(forced, taken, imp))
    blk_f = blk.astype(F32)
    for _ in range(max(k_top - 3, 0)):
        mx = jnp.max(imp, axis=0, keepdims=True)
        idx = jnp.min(jnp.where(imp == mx, blk_f, float(LANES)), axis=0, keepdims=True)
        imp = jnp.where(blk_f == idx, taken, imp)
    bias = jnp.where(imp == taken, 0.0, NEG)
    qaug_ref[LANES:, :] = per_head(bias.astype(BF16))
    blocks0 = tk // SEL_LEN
    bias0 = jnp.broadcast_to(bias[0:blocks0, None, :], (blocks0, SEL_LEN, tq)).reshape(tk, tq)
    sa_ref[...] = sa_ref[...] + per_head(bias0)


    n = t0 // tk
    sel_state = (slice(0, V_ROWS), ms_ref, ls_ref, accs_ref)

    def pair_body(i, carry):
        k0 = pl.multiple_of(2 * i * tk, tk)
        sb_ref[...] = sel_scores(k0 + tk)
        fold_tile(sa_ref[...], None, k0, *sel_state)
        sa_ref[...] = sel_scores(k0 + 2 * tk)
        fold_tile(sb_ref[...], None, k0 + tk, *sel_state)
        return carry

    lax.fori_loop(0, n // 2, pair_body, 0)
    k_diag = pl.multiple_of(n * tk, tk)

    def diagonal_fold(buf_ref):
        if tk == tq:
            fold_tile(buf_ref[...], back(k_diag, tk) >= 0, k_diag, *sel_state)
            return
        pl.when(t0 % tk == 0)(lambda: fold_tile(buf_ref[0:tq, :], back(k_diag, tq) >= 0, k_diag, *sel_state))
        pl.when(t0 % tk != 0)(lambda: fold_tile(buf_ref[...], back(k_diag, tk) >= 0, k_diag, *sel_state))

    @pl.when(n % 2 == 1)
    def _():
        sb_ref[...] = sel_scores(k_diag)
        fold_tile(sa_ref[...], None, k_diag - tk, *sel_state)
        diagonal_fold(sb_ref)

    @pl.when(n % 2 == 0)
    def _():
        diagonal_fold(sa_ref)

    o_s = accs_ref[...] / ls_ref[...]
    o_w = accw_ref[...] / lw_ref[...]
    o_c = oc_ref[...]
    sig = _sigmoid(gate_ref[...])
    heads = []
    for r in range(GROUP):
        cs = slice(r * tq, (r + 1) * tq)
        g_c, g_s, g_w = (sig[3 * r + b:3 * r + b + 1, :] for b in range(3))
        heads.append(g_c * o_c[:, cs] + g_s * o_s[:, cs] + g_w * o_w[:, cs])
    o_ref[...] = jnp.concatenate(heads, axis=0).T.astype(o_ref.dtype)


def _attention(qgt, kaug, vt, kvc, kvct, cmapt, qg, cos, sin, batch, seq):
    t = qgt.shape[1]
    tq = min(Q_TILE, seq)
    tk = min(512, seq)
    nq = seq // tq
    nc = kvc.shape[1]
    assert seq % tk == 0 and tk % tq == 0 and WINDOW % tq == 0 and tq % V_TILE == 0 and seq >= WINDOW
    cols = GROUP * tq
    hq = GROUP * HEAD_DIM
    k_top = min(SEL_TOPK, seq // SEL_LEN)
    n_cmp_var = max(1, min(4, nc // LANES))
    assert k_top >= 3 and nc % n_cmp_var == 0
    kern = functools.partial(_attn_kernel, tq=tq, tk=tk, k_top=k_top, n_cmp_var=n_cmp_var)
    tab = lambda: pl.BlockSpec((ROT_DIM // 2, tq), lambda b, g, qi: (0, qi))
    stat = lambda: pltpu.VMEM((1, cols), F32)
    acc = lambda: pltpu.VMEM((HEAD_DIM, cols), F32)
    return pl.pallas_call(
        kern,
        out_shape=jax.ShapeDtypeStruct((t, N_HEADS * HEAD_DIM), BF16),
        grid=(batch, N_KV, nq),
        in_specs=[
            pl.BlockSpec((hq, tq), lambda b, g, qi: (g, b * nq + qi)),
            pl.BlockSpec((GATE_ROWS, tq), lambda b, g, qi: (N_HEADS * HEAD_DIM // GATE_ROWS + g, b * nq + qi)),
            pl.BlockSpec((seq, 2 * LANES), lambda b, g, qi: (b, g)),
            pl.BlockSpec((1, seq // V_TILE, 1, 2 * V_ROWS, V_TILE), lambda b, g, qi: (b, 0, g, 0, 0)),
            pl.BlockSpec((1, nc, LANES), lambda b, g, qi: (b * N_KV + g, 0, 0)),
            pl.BlockSpec((1, LANES, nc), lambda b, g, qi: (b * N_KV + g, 0, 0)),
            pl.BlockSpec((LANES, nc), lambda b, g, qi: (0, 0)),
            pl.BlockSpec((HEAD_DIM, tq), lambda b, g, qi: (0, 0)),
            tab(), tab(),
        ],
        out_specs=pl.BlockSpec((tq, hq), lambda b, g, qi: (b * nq + qi, g)),
        scratch_shapes=[
            pltpu.VMEM((2 * LANES, cols), BF16), pltpu.VMEM((LANES, cols), BF16), acc(),
            stat(), stat(), acc(), stat(), stat(), acc(),
            pltpu.VMEM((tk, cols), F32), pltpu.VMEM((tk, cols), F32), pltpu.VMEM((tq, cols), F32), pltpu.VMEM((WINDOW, cols), F32),
            pltpu.VMEM((LANES, tq), F32),
        ],
        compiler_params=_params("parallel", "parallel", "arbitrary"),
        name="nsa_attention",
    )(qgt, qgt, kaug, vt, kvc, kvct, cmapt, qg, cos, sin)


def _rope_angles(seq):
    half = ROT_DIM // 2
    inv_freq = ROPE_THETA ** (-jnp.arange(half, dtype=F32) * (2.0 / ROT_DIM))
    ang = jnp.arange(seq, dtype=F32)[:, None] * inv_freq[None, :]
    return jnp.cos(ang), jnp.sin(ang)


def _rope_tables(cos, sin):
    seq, half = cos.shape
    zeros = jnp.zeros((seq, HEAD_DIM - ROT_DIM), F32)
    zh = jnp.zeros((seq, half), F32)
    c = jnp.concatenate([cos, cos, zeros + 1.0], axis=1)
    s1 = jnp.concatenate([-sin, zh, zeros], axis=1)
    s2 = jnp.concatenate([zh, sin, zeros], axis=1)
    reps = LANES // HEAD_DIM
    return tuple(jnp.tile(a, (1, reps)) for a in (c, s1, s2))


def _cmp_to_sel_t(nc, n_sel):
    start_c = np.arange(nc)[None, :] * CMP_STRIDE
    start_s = np.arange(LANES)[:, None] * SEL_LEN
    ov = np.minimum(start_c + CMP_LEN, start_s + SEL_LEN) - np.maximum(start_c, start_s)
    m = np.maximum(ov, 0).astype(np.float32) / CMP_LEN
    m[n_sel:, :] = 0.0
    m[:, nc - 1:] = 0.0
    return jnp.asarray(m, BF16)


def _nsa_in_weights(w_in):
    d = w_in.shape[0]
    hq, hk = N_HEADS * HEAD_DIM, N_KV * HEAD_DIM
    offs = np.cumsum([0, hq] + [hk] * 6)
    q, kc, vc, ks, vs, kw, vw = (w_in[:, offs[n]:offs[n + 1]] for n in range(7))
    gl = w_in[:, offs[7]:]
    per_group = lambda a, b: jnp.stack([a.reshape(d, N_KV, HEAD_DIM), b.reshape(d, N_KV, HEAD_DIM)],
                                       axis=2).reshape(d, N_KV * LANES)
    gates = jnp.pad(gl.reshape(d, N_KV, 3 * GROUP), ((0, 0), (0, 0), (0, GATE_ROWS - 3 * GROUP)))
    wn = jnp.concatenate([per_group(ks, kw), kc, vc], axis=1).astype(BF16)
    wt = jnp.concatenate([q, gates.reshape(d, N_KV * GATE_ROWS), per_group(vs, vw)], axis=1).T.astype(BF16)
    return wn, wt


def _nsa_layer(x, g, w_in, q_g, kc_g, ks_g, kw_g, pe_k, pe_v, ck_w1, ck_w2, cv_w1, cv_w2, w_out, batch, seq):
    nc = seq // CMP_STRIDE
    n_sel = seq // SEL_LEN
    assert n_sel <= LANES and seq % CMP_STRIDE == 0 and 3 * GROUP <= GATE_ROWS
    wn, wt = _nsa_in_weights(w_in)
    proj, kcvc, qgt, vt = _in_proj(x, g, wn, wt, batch, seq)
    cos, sin = _rope_angles(seq)
    seg = np.arange(LANES) // HEAD_DIM
    bd = jnp.asarray(seg[:, None] == seg[None, :], BF16)
    kaug = _kprep(proj, jnp.concatenate([ks_g, kw_g]).reshape(1, LANES), bd, _rope_tables(cos, sin), seq)
    kvc, kvct = _compress(kcvc, pe_k, pe_v, ck_w1, ck_w2, cv_w1, cv_w2, kc_g, batch, seq)
    tq = min(Q_TILE, seq)
    qg = jnp.broadcast_to((q_g * float(HEAD_DIM ** -0.5 * np.log2(np.e)))[:, None], (HEAD_DIM, tq))
    attn = _attention(qgt, kaug, vt, kvc, kvct, _cmp_to_sel_t(nc, n_sel), qg, cos.T, sin.T, batch, seq)
    return _out_proj(attn, x, w_out)


def kernel(x, mix_norm_g, ffn_norm_g, conv_w_pw1, conv_b_pw1, conv_w_dw, conv_b_dw, conv_ln_g, conv_ln_b, conv_w_pw2, conv_b_pw2, nsa_w_in, nsa_q_norm, nsa_kc_norm, nsa_ks_norm, nsa_kw_norm, nsa_pe_k, nsa_pe_v, nsa_ck_w1, nsa_ck_w2, nsa_cv_w1, nsa_cv_w2, nsa_w_out, ffn_w_up, ffn_w_dw, ffn_b_dw, ffn_w_down):
    batch, seq, d = x.shape
    depth = mix_norm_g.shape[0]
    n_mixers = 2
    h = x.reshape(batch * seq, d)
    w_up_all, w_down_all = ffn_w_up.astype(BF16), ffn_w_down.astype(BF16)
    for i in range(depth):
        j = i // n_mixers
        if i % n_mixers == 0:
            u = _conformer_glu(h, mix_norm_g[i], conv_w_pw1[j], conv_b_pw1[j])
            h = _conformer_conv_out(u, h, conv_w_dw[j], conv_b_dw[j], conv_ln_g[j], conv_ln_b[j],
                                    conv_w_pw2[j], conv_b_pw2[j], seq)
        else:
            h = _nsa_layer(h, mix_norm_g[i], nsa_w_in[j], nsa_q_norm[j], nsa_kc_norm[j], nsa_ks_norm[j],
                           nsa_kw_norm[j], nsa_pe_k[j], nsa_pe_v[j], nsa_ck_w1[j], nsa_ck_w2[j], nsa_cv_w1[j],
                           nsa_cv_w2[j], nsa_w_out[j], batch, seq)
        h = _conv_ffn(h, ffn_norm_g[i], w_up_all, ffn_w_dw[i], ffn_b_dw[i], w_down_all, i, seq)
    return h.reshape(batch, seq, d)
```

```python
import functools

import numpy as np
import jax
import jax.numpy as jnp
from jax import lax
from jax.experimental import pallas as pl
from jax.experimental.pallas import tpu as pltpu

N_HEADS = 16
HEAD_DIM = 64
N_KV = 4
GROUP = N_HEADS // N_KV
ROT_DIM = HEAD_DIM // 4
ROPE_THETA = 500000.0
CMP_LEN = 32
CMP_STRIDE = 16
SEL_LEN = 64
SEL_TOPK = 16
WINDOW = 512
EPS = 1e-6
NEG = -1e30

LANES = 128
SUBLANES = 8
BF16_ROWS = 16
V_TILE = 256
Q_TILE = 256
V_ROWS = HEAD_DIM + BF16_ROWS
GATE_ROWS = 16
VMEM_LIMIT = 48 * 1024 * 1024

F32 = jnp.float32
BF16 = jnp.bfloat16
NT_DIMS = (((1,), (1,)), ((), ()))


def _params(*sem):
    return pltpu.CompilerParams(dimension_semantics=sem, vmem_limit_bytes=VMEM_LIMIT)


def _dot(a, b):
    return jnp.dot(a, b, preferred_element_type=F32)


def _dot_nt(a, b):
    return lax.dot_general(a, b, NT_DIMS, preferred_element_type=F32)


def _split_bf16(x):
    hi = x.astype(BF16)
    lo = (x - hi.astype(F32)).astype(BF16)
    return hi, lo


def _rms_rows(x, g):
    ms = jnp.mean(x * x, axis=-1, keepdims=True)
    return x * lax.rsqrt(ms + EPS) * g


def _sigmoid(x):
    return 1.0 / (1.0 + jnp.exp(-x))


def _head_rms(xb, bd, gain):
    hi, lo = _split_bf16(xb * xb)
    ss = _dot(hi, bd) + _dot(lo, bd)
    return xb * lax.rsqrt(ss * (1.0 / HEAD_DIM) + EPS) * gain


def _rope(xb, c, s1, s2):
    half = ROT_DIM // 2
    return xb * c + pltpu.roll(xb, LANES - half, 1) * s1 + pltpu.roll(xb, half, 1) * s2


def _ffn_kernel(x_ref, halo_ref, g_ref, wup_ref, wdw_ref, bdw_ref, wd_ref, o_ref,
                hn_ref, a0_ref, v0_ref, a1_ref, v1_ref, act_ref, *, tm, tiles_per_seq, nch):
    i = pl.program_id(0)
    hl = BF16_ROWS
    g = g_ref[...]
    hn_ref[hl:, :] = _rms_rows(x_ref[...], g).astype(BF16)
    first = (i % tiles_per_seq) == 0
    hn_ref[0:hl, :] = jnp.where(first, 0.0, _rms_rows(halo_ref[...], g)).astype(BF16)
    o_ref[...] = x_ref[...]

    tf = a0_ref.shape[1]

    def up(c, a_ref, v_ref):
        a_ref[...] = _dot(hn_ref[...], wup_ref[:, pl.ds(pl.multiple_of(c * tf, tf), tf)])
        v_ref[...] = _dot(hn_ref[hl:, :], wup_ref[:, pl.ds(pl.multiple_of((nch + c) * tf, tf), tf)])

    def gate(c, a_ref, v_ref):
        w = wdw_ref[c]
        cv = (w[0:1] * a_ref[hl - 2:hl - 2 + tm, :] + w[1:2] * a_ref[hl - 1:hl - 1 + tm, :]
              + w[2:3] * a_ref[hl:hl + tm, :] + bdw_ref[c])
        return (cv * _sigmoid(cv) * v_ref[...]).astype(BF16)

    def down(c, n_chunks):
        rows = pl.ds(pl.multiple_of(c * tf, tf), n_chunks * tf)
        o_ref[...] += _dot(act_ref[:, 0:n_chunks * tf], wd_ref[rows, :])

    buf0, buf1 = (a0_ref, v0_ref), (a1_ref, v1_ref)
    up(0, *buf0)

    def pair_body(p, carry):
        c = 2 * p
        up(c + 1, *buf1)
        act_ref[:, 0:tf] = gate(c, *buf0)
        up(c + 2, *buf0)
        act_ref[:, tf:] = gate(c + 1, *buf1)
        down(c, 2)
        return carry

    lax.fori_loop(0, (nch - 1) // 2, pair_body, 0)
    if nch % 2 == 1:
        act_ref[:, 0:tf] = gate(nch - 1, *buf0)
        down(nch - 1, 1)
    else:
        up(nch - 1, *buf1)
        act_ref[:, 0:tf] = gate(nch - 2, *buf0)
        act_ref[:, tf:] = gate(nch - 1, *buf1)
        down(nch - 2, 2)


def _conv_ffn(x, g, w_up_all, w_dw, b_dw, w_down_all, layer, seq):
    t, d = x.shape
    dff = w_down_all.shape[1]
    tm = min(512, seq)
    tf = 256
    assert seq % tm == 0 and dff % tf == 0 and w_dw.shape[0] == 3
    nch = dff // tf
    hl = BF16_ROWS
    kern = functools.partial(_ffn_kernel, tm=tm, tiles_per_seq=seq // tm, nch=nch)
    wup = w_up_all
    wdn = w_down_all
    wdw = w_dw.reshape(3, nch, tf).transpose(1, 0, 2)
    bdw = b_dw.reshape(nch, 1, tf)
    resident = lambda a: pl.BlockSpec(a.shape, lambda i: (0,) * a.ndim, pipeline_mode=pl.Buffered(1))
    of_layer = lambda a: pl.BlockSpec((None,) + a.shape[1:], lambda i: (layer,) + (0,) * (a.ndim - 1),
                                      pipeline_mode=pl.Buffered(1))
    return pl.pallas_call(
        kern,
        out_shape=jax.ShapeDtypeStruct((t, d), F32),
        grid=(t // tm,),
        in_specs=[
            pl.BlockSpec((tm, d), lambda i: (i, 0)),
            pl.BlockSpec((hl, d), lambda i: (jnp.maximum(i * (tm // hl) - 1, 0), 0)),
            pl.BlockSpec((1, d), lambda i: (0, 0)),
            of_layer(wup), resident(wdw), resident(bdw), of_layer(wdn),
        ],
        out_specs=pl.BlockSpec((tm, d), lambda i: (i, 0)),
        scratch_shapes=[
            pltpu.VMEM((tm + hl, d), BF16),
            pltpu.VMEM((tm + hl, tf), F32), pltpu.VMEM((tm, tf), F32),
            pltpu.VMEM((tm + hl, tf), F32), pltpu.VMEM((tm, tf), F32),
            pltpu.VMEM((tm, 2 * tf), BF16),
        ],
        compiler_params=_params("parallel"),
        name="conv_ffn",
    )(x, x, g.reshape(1, d), wup, wdw, bdw, wdn)


def _glu_kernel(x_ref, g_ref, wa_ref, wg_ref, ba_ref, bg_ref, o_ref, hn_ref):
    @pl.when(pl.program_id(1) == 0)
    def _():
        hn_ref[...] = _rms_rows(x_ref[...], g_ref[...]).astype(BF16)

    th = hn_ref.shape[0] // 2
    halves = [(_dot(hn_ref[h * th:(h + 1) * th, :], wa_ref[...]), _dot(hn_ref[h * th:(h + 1) * th, :], wg_ref[...]))
              for h in range(2)]
    for h, (a, gate) in enumerate(halves):
        o_ref[h * th:(h + 1) * th, :] = (a + ba_ref[...]) * _sigmoid(gate + bg_ref[...])


def _conformer_glu(x, g, w_pw1, b_pw1):
    t, d = x.shape
    tm = min(1024, t)
    tn = 256
    nj = d // tn
    w = w_pw1.astype(BF16)
    b = b_pw1.reshape(1, 2 * d)
    return pl.pallas_call(
        _glu_kernel,
        out_shape=jax.ShapeDtypeStruct((t, d), F32),
        grid=(t // tm, nj),
        in_specs=[
            pl.BlockSpec((tm, d), lambda i, j: (i, 0)),
            pl.BlockSpec((1, d), lambda i, j: (0, 0)),
            pl.BlockSpec((d, tn), lambda i, j: (0, j)),
            pl.BlockSpec((d, tn), lambda i, j: (0, j + nj)),
            pl.BlockSpec((1, tn), lambda i, j: (0, j)),
            pl.BlockSpec((1, tn), lambda i, j: (0, j + nj)),
        ],
        out_specs=pl.BlockSpec((tm, tn), lambda i, j: (i, j)),
        scratch_shapes=[pltpu.VMEM((tm, d), BF16)],
        compiler_params=_params("parallel", "arbitrary"),
        name="conformer_glu",
    )(x, g.reshape(1, d), w, w, b, b)


def _dwconv_kernel(u_ref, halo_ref, x_ref, wdw_ref, bdw_ref, lng_ref, lnb_ref, w2_ref, b2_ref, o_ref,
                   ext_ref, cv_ref, slab_ref, *, tm, halo, width, tiles_per_seq, row_chunk, col_chunk):
    i = pl.program_id(0)
    d = u_ref.shape[1]
    first = (i % tiles_per_seq) == 0
    ext_ref[0:halo, :] = jnp.where(first, 0.0, halo_ref[...])
    ext_ref[halo:, :] = u_ref[...]
    off = halo - (width - 1)
    for r0 in range(0, tm, row_chunk):
        for c0 in range(0, d, col_chunk):
            cs = slice(c0, c0 + col_chunk)
            acc = jnp.broadcast_to(bdw_ref[:, cs], (row_chunk, col_chunk))
            for r in range(min(SUBLANES, width)):
                taps = range(r, width, SUBLANES)
                lo = r0 + off + r
                rows = row_chunk + (len(taps) - 1) * SUBLANES
                slab_ref[r, 0:rows, :] = ext_ref[lo:lo + rows, cs]
                for q, k in enumerate(taps):
                    acc = acc + wdw_ref[k:k + 1, cs] * slab_ref[r, q * SUBLANES:q * SUBLANES + row_chunk, :]
            cv_ref[r0:r0 + row_chunk, cs] = acc
    u = cv_ref[...]
    mu = jnp.mean(u, axis=-1, keepdims=True)
    uc = u - mu
    var = jnp.mean(uc * uc, axis=-1, keepdims=True)
    y = uc * lax.rsqrt(var + EPS) * lng_ref[...] + lnb_ref[...]
    s = (y * _sigmoid(y)).astype(BF16)
    o_ref[...] = x_ref[...] + _dot(s, w2_ref[...]) + b2_ref[...]


def _conformer_conv_out(u, x, w_dw, b_dw, ln_g, ln_b, w_pw2, b_pw2, seq):
    t, d = x.shape
    width = w_dw.shape[0]
    halo = 32
    assert width - 1 <= halo
    tm = min(256, seq)
    assert seq % tm == 0 and tm % halo == 0
    row_chunk, col_chunk = 64, 256
    kern = functools.partial(_dwconv_kernel, tm=tm, halo=halo, width=width, tiles_per_seq=seq // tm,
                             row_chunk=row_chunk, col_chunk=col_chunk)
    slab_rows = row_chunk + (width - 1) // SUBLANES * SUBLANES
    vec = lambda: pl.BlockSpec((1, d), lambda i: (0, 0))
    return pl.pallas_call(
        kern,
        out_shape=jax.ShapeDtypeStruct((t, d), F32),
        grid=(t // tm,),
        in_specs=[
            pl.BlockSpec((tm, d), lambda i: (i, 0)),
            pl.BlockSpec((halo, d), lambda i: (jnp.maximum(i * (tm // halo) - 1, 0), 0)),
            pl.BlockSpec((tm, d), lambda i: (i, 0)),
            pl.BlockSpec((width, d), lambda i: (0, 0)),
            vec(), vec(), vec(),
            pl.BlockSpec((d, d), lambda i: (0, 0)),
            vec(),
        ],
        out_specs=pl.BlockSpec((tm, d), lambda i: (i, 0)),
        scratch_shapes=[pltpu.VMEM((tm + halo, d), F32), pltpu.VMEM((tm, d), F32),
                        pltpu.VMEM((SUBLANES, slab_rows, col_chunk), F32)],
        compiler_params=_params("parallel"),
        name="conformer_dwconv_out",
    )(u, u, x, w_dw, b_dw.reshape(1, d), ln_g.reshape(1, d), ln_b.reshape(1, d), w_pw2.astype(BF16),
      b_pw2.reshape(1, d))


def _in_proj_kernel(x_ref, g_ref, wn_ref, wt_ref, perm_ref, on_ref, oc_ref, oq_ref, ov_ref, *, n_qg, row_chunk):
    hn = _rms_rows(x_ref[...], g_ref[...]).astype(BF16)
    nk = on_ref.shape[1]
    on_ref[...] = _dot(hn, wn_ref[:, 0:nk])
    kcvc = _dot(hn, wn_ref[:, nk:]).astype(BF16)
    regrouped = _dot(perm_ref[...], kcvc).astype(BF16)
    per = regrouped.shape[0] // CMP_STRIDE
    for l in range(CMP_STRIDE):
        oc_ref[0, l] = regrouped[l * per:(l + 1) * per, :]
    for r0 in range(0, n_qg, row_chunk):
        r1 = min(r0 + row_chunk, n_qg)
        oq_ref[r0:r1, :] = _dot_nt(wt_ref[r0:r1, :], hn)
    vt = _dot_nt(wt_ref[n_qg:, :], hn).astype(BF16)
    ones = jnp.ones((BF16_ROWS, V_TILE), BF16)
    for jj in range(ov_ref.shape[1]):
        for g in range(N_KV):
            for half in range(2):
                r0 = g * LANES + half * HEAD_DIM
                ov_ref[0, jj, g, half * V_ROWS:half * V_ROWS + HEAD_DIM, :] = (
                    vt[r0:r0 + HEAD_DIM, jj * V_TILE:(jj + 1) * V_TILE])
                ov_ref[0, jj, g, half * V_ROWS + HEAD_DIM:(half + 1) * V_ROWS, :] = ones


def _in_proj(x, g, wn, wt, batch, seq):
    t, d = x.shape
    tm = min(512, seq)
    n_qg = N_HEADS * HEAD_DIM + N_KV * GATE_ROWS
    assert seq % tm == 0 and tm % V_TILE == 0 and wt.shape[0] == n_qg + N_KV * LANES
    per_seq = seq // tm
    nk = N_KV * LANES
    ncv = wn.shape[1] - nk
    per = tm // CMP_STRIDE
    rows = np.arange(tm)
    perm = np.zeros((tm, tm), np.float32)
    perm[(rows % CMP_STRIDE) * per + rows // CMP_STRIDE, rows] = 1.0
    kern = functools.partial(_in_proj_kernel, n_qg=n_qg, row_chunk=512)
    return pl.pallas_call(
        kern,
        out_shape=(jax.ShapeDtypeStruct((t, nk), F32),
                   jax.ShapeDtypeStruct((t // tm, CMP_STRIDE, per, ncv), BF16),
                   jax.ShapeDtypeStruct((n_qg, t), F32),
                   jax.ShapeDtypeStruct((batch, seq // V_TILE, N_KV, 2 * V_ROWS, V_TILE), BF16)),
        grid=(t // tm,),
        in_specs=[
            pl.BlockSpec((tm, d), lambda i: (i, 0)),
            pl.BlockSpec((1, d), lambda i: (0, 0)),
            pl.BlockSpec(wn.shape, lambda i: (0, 0)),
            pl.BlockSpec(wt.shape, lambda i: (0, 0)),
            pl.BlockSpec((tm, tm), lambda i: (0, 0)),
        ],
        out_specs=(pl.BlockSpec((tm, nk), lambda i: (i, 0)),
                   pl.BlockSpec((1, CMP_STRIDE, per, ncv), lambda i: (i, 0, 0, 0)),
                   pl.BlockSpec((n_qg, tm), lambda i: (0, i)),
                   pl.BlockSpec((1, tm // V_TILE, N_KV, 2 * V_ROWS, V_TILE),
                                lambda i: (i // per_seq, i % per_seq, 0, 0, 0))),
        compiler_params=_params("parallel"),
        name="nsa_in_proj",
    )(x, g.reshape(1, d), wn, wt, jnp.asarray(perm, BF16))


def _out_proj_kernel(a_ref, x_ref, w_ref, o_ref):
    o_ref[...] = x_ref[...] + _dot(a_ref[...], w_ref[...])


def _out_proj(a, x, w):
    t, d = x.shape
    tm = min(512, t)
    return pl.pallas_call(
        _out_proj_kernel,
        out_shape=jax.ShapeDtypeStruct((t, d), F32),
        grid=(t // tm,),
        in_specs=[
            pl.BlockSpec((tm, a.shape[1]), lambda i: (i, 0)),
            pl.BlockSpec((tm, d), lambda i: (i, 0)),
            pl.BlockSpec(w.shape, lambda i: (0, 0)),
        ],
        out_specs=pl.BlockSpec((tm, d), lambda i: (i, 0)),
        compiler_params=_params("parallel"),
        name="nsa_out_proj",
    )(a, x, w.astype(BF16))


def _kprep_kernel(k_ref, gain_ref, bd_ref, c_ref, s1_ref, s2_ref, kaug_ref, *, ts, seq):
    i = pl.program_id(0)
    bd = bd_ref[...]
    gain = gain_ref[...]
    c, s1, s2 = c_ref[...], s1_ref[...], s2_ref[...]
    tpos = (i * ts) % seq + lax.broadcasted_iota(jnp.int32, (ts, LANES), 0)
    lane = lax.broadcasted_iota(jnp.int32, (ts, LANES), 1)
    onehot = jnp.where(tpos // SEL_LEN == lane, 1.0, 0.0).astype(BF16)
    for g in range(N_KV):
        xb = k_ref[:, g * LANES:(g + 1) * LANES]
        xr = _rope(_head_rms(xb, bd, gain), c, s1, s2)
        kaug_ref[:, 2 * g * LANES:(2 * g + 1) * LANES] = xr.astype(BF16)
        kaug_ref[:, (2 * g + 1) * LANES:(2 * g + 2) * LANES] = onehot


def _kprep(proj, gain, bd, tabs, seq):
    t = proj.shape[0]
    ts = min(512, seq)
    assert seq % ts == 0
    nk = N_KV * LANES
    kern = functools.partial(_kprep_kernel, ts=ts, seq=seq)
    tab = lambda: pl.BlockSpec((ts, LANES), lambda i: (i % (seq // ts), 0))
    return pl.pallas_call(
        kern,
        out_shape=jax.ShapeDtypeStruct((t, 2 * nk), BF16),
        grid=(t // ts,),
        in_specs=[
            pl.BlockSpec((ts, nk), lambda i: (i, 0)),
            pl.BlockSpec((1, LANES), lambda i: (0, 0)),
            pl.BlockSpec((LANES, LANES), lambda i: (0, 0)),
            tab(), tab(), tab(),
        ],
        out_specs=pl.BlockSpec((ts, 2 * nk), lambda i: (i, 0)),
        compiler_params=_params("parallel"),
        name="nsa_key_prep",
    )(proj, gain, bd, *tabs)


def _compress_kernel(x_ref, w1k_ref, w1v_ref, ck_ref, cv_ref, w2k_ref, w2v_ref, gain_ref, o_ref, ot_ref,
                     acck_ref, accv_ref):
    l = pl.program_id(1)
    nc = acck_ref.shape[0]
    hid = w2k_ref.shape[0]
    hk = w1k_ref.shape[1]

    @pl.when(l == 0)
    def _():
        acck_ref[...] = jnp.zeros_like(acck_ref)
        accv_ref[...] = jnp.zeros_like(accv_ref)

    x = x_ref[0].reshape(nc, 2 * hk)
    acck_ref[...] += _dot(x[:, 0:hk], w1k_ref[0])
    accv_ref[...] += _dot(x[:, hk:], w1v_ref[0])

    @pl.when(l == pl.num_programs(1) - 1)
    def _():
        for g in range(N_KV):
            def hidden(acc_ref, c_ref):
                first = acc_ref[:, g * hid:(g + 1) * hid] + c_ref[0:1, :]
                second = acc_ref[:, (N_KV + g) * hid:(N_KV + g + 1) * hid] + c_ref[1:2, :]
                pre = first + pltpu.roll(second, nc - 1, 0)
                return (pre * _sigmoid(pre)).astype(BF16)

            kv = _dot(hidden(acck_ref, ck_ref), w2k_ref[...]) + _dot(hidden(accv_ref, cv_ref), w2v_ref[...])
            is_k = lax.broadcasted_iota(jnp.int32, kv.shape, 1) < HEAD_DIM
            ss = jnp.sum(jnp.where(is_k, kv * kv, 0.0), axis=-1, keepdims=True)
            kn = kv * lax.rsqrt(ss * (1.0 / HEAD_DIM) + EPS) * gain_ref[...]
            out = jnp.where(is_k, kn, kv)
            o_ref[g] = out.astype(BF16)
            ot_ref[g] = out.T.astype(BF16)


def _compress(kcvc, pe_k, pe_v, ck_w1, ck_w2, cv_w1, cv_w2, kc_g, batch, seq):
    nc = seq // CMP_STRIDE
    hid = ck_w1.shape[1]
    hk = N_KV * HEAD_DIM
    tiles, _, per, width = kcvc.shape
    assert CMP_LEN == 2 * CMP_STRIDE and width == 2 * hk and tiles * per == batch * nc
    x5 = kcvc.reshape(batch, tiles // batch, CMP_STRIDE, per, width)
    pad = lambda w, left: jnp.pad(w, ((0, 0), (HEAD_DIM, 0) if left else (0, HEAD_DIM))).astype(BF16)
    gain = jnp.concatenate([kc_g, jnp.ones((HEAD_DIM,), F32)]).reshape(1, LANES)

    def first_layer(pe, w1):
        w = w1.reshape(2, CMP_STRIDE, HEAD_DIM, hid)
        rows = lambda a, k: jnp.pad(a, ((0, 0), (k * HEAD_DIM, (N_KV - 1 - k) * HEAD_DIM), (0, 0)))
        bd = jnp.concatenate([rows(w[s], k) for s in range(2) for k in range(N_KV)], axis=2)
        const = jnp.einsum("sld,sldh->sh", pe.reshape(2, CMP_STRIDE, HEAD_DIM), w, precision=lax.Precision.HIGHEST)
        return bd.astype(BF16), const

    w1k, const_k = first_layer(pe_k, ck_w1)
    w1v, const_v = first_layer(pe_v, cv_w1)
    wide = 2 * N_KV * hid
    step_w = lambda: pl.BlockSpec((1, hk, wide), lambda b, l: (l, 0, 0))
    full = lambda shape: pl.BlockSpec(shape, lambda b, l: (0,) * len(shape))
    return pl.pallas_call(
        _compress_kernel,
        out_shape=(jax.ShapeDtypeStruct((batch * N_KV, nc, LANES), BF16),
                   jax.ShapeDtypeStruct((batch * N_KV, LANES, nc), BF16)),
        grid=(batch, CMP_STRIDE),
        in_specs=[pl.BlockSpec((1, tiles // batch, None, per, width), lambda b, l: (b, 0, l, 0, 0)),
                  step_w(), step_w(), full((2, hid)), full((2, hid)),
                  full((hid, LANES)), full((hid, LANES)), full((1, LANES))],
        out_specs=(pl.BlockSpec((N_KV, nc, LANES), lambda b, l: (b, 0, 0)),
                   pl.BlockSpec((N_KV, LANES, nc), lambda b, l: (b, 0, 0))),
        scratch_shapes=[pltpu.VMEM((nc, wide), F32), pltpu.VMEM((nc, wide), F32)],
        compiler_params=_params("parallel", "arbitrary"),
        name="nsa_compress",
    )(x5, w1k, w1v, const_k, const_v, pad(ck_w2, False), pad(cv_w2, True), gain)


def _attn_kernel(qt_ref, gate_ref, kaug_ref, vt_ref, kvc_ref, kvct_ref, cmapt_ref, qg_ref, cos_ref, sin_ref,
                 o_ref,
                 qaug_ref, qwin_ref, oc_ref, ms_ref, ls_ref, accs_ref, mw_ref, lw_ref, accw_ref,
                 sa_ref, sb_ref, wdiag_ref, wold_ref, imp_ref,
                 *, tq, tk, k_top, n_cmp_var):
    qi = pl.program_id(2)
    t0 = qi * tq
    cols = GROUP * tq
    half = ROT_DIM // 2

    gain = qg_ref[...]
    cos, sin = cos_ref[...], sin_ref[...]
    nope, rope = [], []
    for r in range(GROUP):
        x = qt_ref[r * HEAD_DIM:(r + 1) * HEAD_DIM, :]
        ss = jnp.sum(x * x, axis=0, keepdims=True)
        xn = x * lax.rsqrt(ss * (1.0 / HEAD_DIM) + EPS) * gain
        x1, x2 = xn[0:half], xn[half:ROT_DIM]
        nope.append(xn)
        rope.append(jnp.concatenate([x1 * cos - x2 * sin, x2 * cos + x1 * sin, xn[ROT_DIM:]], axis=0))
    zeros = jnp.zeros((HEAD_DIM, cols), BF16)
    q_nope = jnp.concatenate([jnp.concatenate(nope, axis=1).astype(BF16), zeros], axis=0)
    q_rope = jnp.concatenate(rope, axis=1).astype(BF16)
    qwin_ref[...] = jnp.concatenate([zeros, q_rope], axis=0)

    tpos = t0 + lax.broadcasted_iota(jnp.int32, (1, tq), 1)

    def per_head(x):
        return jnp.concatenate([x] * GROUP, axis=1)

    def fold_tile(s, keep, k0, v_rows, m_ref, l_ref, acc_ref):
        if keep is not None:
            s = jnp.where(per_head(keep), s, NEG)
        m_prev = m_ref[...]
        m_new = jnp.maximum(m_prev, jnp.max(s, axis=0, keepdims=True))
        p = jnp.exp2(s - m_new)
        alpha = jnp.exp2(m_prev - m_new)
        pb = p.astype(BF16)
        v_tile0 = k0 // V_TILE
        pv = _dot(vt_ref[0, v_tile0, 0, v_rows, :], pb[0:V_TILE])
        for c in range(1, s.shape[0] // V_TILE):
            pv = pv + _dot(vt_ref[0, v_tile0 + c, 0, v_rows, :], pb[c * V_TILE:(c + 1) * V_TILE])
        l_ref[...] = alpha * l_ref[...] + pv[HEAD_DIM:HEAD_DIM + 1]
        acc_ref[...] = alpha * acc_ref[...] + pv[0:HEAD_DIM]
        m_ref[...] = m_new

    def back(k0, size):
        return tpos - (k0 + lax.broadcasted_iota(jnp.int32, (size, 1), 0))

    def sel_scores(k0):
        return _dot(kaug_ref[pl.ds(k0, tk), :], qaug_ref[...])

    def win_scores(k0, size):
        return _dot(kaug_ref[pl.ds(k0, size), 0:LANES], qwin_ref[...])

    for m_ref, l_ref, acc_ref in ((ms_ref, ls_ref, accs_ref), (mw_ref, lw_ref, accw_ref)):
        m_ref[...] = jnp.full_like(m_ref, NEG)
        l_ref[...] = jnp.zeros_like(l_ref)
        acc_ref[...] = jnp.zeros_like(acc_ref)

    t0a = pl.multiple_of(t0, tq)
    k_old = pl.multiple_of(jnp.maximum(t0 - WINDOW, 0), tq)
    wdiag_ref[...] = win_scores(t0a, tq)
    wold_ref[...] = win_scores(k_old, WINDOW)
    qaug_ref[0:LANES, :] = jnp.concatenate([q_rope, zeros], axis=0)
    sa_ref[...] = _dot(kaug_ref[0:tk, 0:LANES], qaug_ref[0:LANES, :])
    win_state = (slice(V_ROWS, 2 * V_ROWS), mw_ref, lw_ref, accw_ref)
    fold_tile(wdiag_ref[...], back(t0a, tq) >= 0, t0a, *win_state)
    b_old = back(k_old, WINDOW)
    fold_tile(wold_ref[...], (b_old < WINDOW) & (b_old > tpos - t0), k_old, *win_state)

    nc = kvc_ref.shape[1]

    def cmp_branch(rows):
        kvc = kvc_ref[0, 0:rows, :]
        cmp_end = lax.broadcasted_iota(jnp.int32, (rows, 1), 0) * CMP_STRIDE + (CMP_LEN - 1)
        s_c = jnp.where(per_head(cmp_end <= tpos), _dot(kvc, q_nope), NEG)
        e = jnp.exp2(s_c - jnp.maximum(jnp.max(s_c, axis=0, keepdims=True), 0.1 * NEG))
        p_c = e * (1.0 / jnp.maximum(jnp.sum(e, axis=0, keepdims=True), 1e-30))
        oc_ref[...] = _dot(kvct_ref[0, HEAD_DIM:, 0:rows], p_c.astype(BF16))
        p_sum = p_c[:, 0:tq]
        for r in range(1, GROUP):
            p_sum = p_sum + p_c[:, r * tq:(r + 1) * tq]
        hi, lo = _split_bf16(p_sum)
        imp_ref[...] = _dot(cmapt_ref[:, 0:rows], hi) + _dot(cmapt_ref[:, 0:rows], lo)

    variant = (t0 + tq - 1) // (CMP_STRIDE * (nc // n_cmp_var))
    for v in range(n_cmp_var):
        pl.when(variant == v)(functools.partial(cmp_branch, (v + 1) * (nc // n_cmp_var)))
    imp = imp_ref[...]

    blk = lax.broadcasted_iota(jnp.int32, (LANES, tq), 0)
    cur = tpos // SEL_LEN
    forced = (blk == 0) | (blk == cur) | (blk == cur - 1)
    taken = -3e38
    imp_ref[...] = jnp.where(blk > cur, -1.0, jnp.where(forced, taken, imp))

    @pl.when(t0 + tq > k_top * SEL_LEN)
    def _():
        imp = imp_ref[...]
        blk_f = blk.astype(F32)
        for _ in range(max(k_top - 3, 0)):
            mx = jnp.max(imp, axis=0, keepdims=True)
            idx = jnp.min(jnp.where(imp == mx, blk_f, float(LANES)), axis=0, keepdims=True)
            imp = jnp.where(blk_f == idx, taken, imp)
        imp_ref[...] = imp

    @pl.when(t0 + tq <= k_top * SEL_LEN)
    def _():
        imp_ref[...] = jnp.where(blk <= cur, taken, imp_ref[...])

    bias = jnp.where(imp_ref[...] == taken, 0.0, NEG)
    qaug_ref[LANES:, :] = per_head(bias.astype(BF16))
    blocks0 = tk // SEL_LEN
    bias0 = jnp.broadcast_to(bias[0:blocks0, None, :], (blocks0, SEL_LEN, tq)).reshape(tk, tq)
    sa_ref[...] = sa_ref[...] + per_head(bias0)


    n = t0 // tk
    sel_state = (slice(0, V_ROWS), ms_ref, ls_ref, accs_ref)

    def pair_body(i, carry):
        k0 = pl.multiple_of(2 * i * tk, tk)
        sb_ref[...] = sel_scores(k0 + tk)
        fold_tile(sa_ref[...], None, k0, *sel_state)
        sa_ref[...] = sel_scores(k0 + 2 * tk)
        fold_tile(sb_ref[...], None, k0 + tk, *sel_state)
        return carry

    lax.fori_loop(0, n // 2, pair_body, 0)
    k_diag = pl.multiple_of(n * tk, tk)

    def diagonal_fold(buf_ref):
        if tk == tq:
            fold_tile(buf_ref[...], back(k_diag, tk) >= 0, k_diag, *sel_state)
            return
        pl.when(t0 % tk == 0)(lambda: fold_tile(buf_ref[0:tq, :], back(k_diag, tq) >= 0, k_diag, *sel_state))
        pl.when(t0 % tk != 0)(lambda: fold_tile(buf_ref[...], back(k_diag, tk) >= 0, k_diag, *sel_state))

    @pl.when(n % 2 == 1)
    def _():
        sb_ref[...] = sel_scores(k_diag)
        fold_tile(sa_ref[...], None, k_diag - tk, *sel_state)
        diagonal_fold(sb_ref)

    @pl.when(n % 2 == 0)
    def _():
        diagonal_fold(sa_ref)

    o_s = accs_ref[...] / ls_ref[...]
    o_w = accw_ref[...] / lw_ref[...]
    o_c = oc_ref[...]
    sig = _sigmoid(gate_ref[...])
    heads = []
    for r in range(GROUP):
        cs = slice(r * tq, (r + 1) * tq)
        g_c, g_s, g_w = (sig[3 * r + b:3 * r + b + 1, :] for b in range(3))
        heads.append(g_c * o_c[:, cs] + g_s * o_s[:, cs] + g_w * o_w[:, cs])
    o_ref[...] = jnp.concatenate(heads, axis=0).T.astype(o_ref.dtype)


def _attention(qgt, kaug, vt, kvc, kvct, cmapt, qg, cos, sin, batch, seq):
    t = qgt.shape[1]
    tq = min(Q_TILE, seq)
    tk = min(512, seq)
    nq = seq // tq
    nc = kvc.shape[1]
    assert seq % tk == 0 and tk % tq == 0 and WINDOW % tq == 0 and tq % V_TILE == 0 and seq >= WINDOW
    cols = GROUP * tq
    hq = GROUP * HEAD_DIM
    k_top = min(SEL_TOPK, seq // SEL_LEN)
    n_cmp_var = max(1, min(4, nc // LANES))
    assert k_top >= 3 and nc % n_cmp_var == 0
    kern = functools.partial(_attn_kernel, tq=tq, tk=tk, k_top=k_top, n_cmp_var=n_cmp_var)
    tab = lambda: pl.BlockSpec((ROT_DIM // 2, tq), lambda b, g, qi: (0, qi))
    stat = lambda: pltpu.VMEM((1, cols), F32)
    acc = lambda: pltpu.VMEM((HEAD_DIM, cols), F32)
    return pl.pallas_call(
        kern,
        out_shape=jax.ShapeDtypeStruct((t, N_HEADS * HEAD_DIM), BF16),
        grid=(batch, N_KV, nq),
        in_specs=[
            pl.BlockSpec((hq, tq), lambda b, g, qi: (g, b * nq + qi)),
            pl.BlockSpec((GATE_ROWS, tq), lambda b, g, qi: (N_HEADS * HEAD_DIM // GATE_ROWS + g, b * nq + qi)),
            pl.BlockSpec((seq, 2 * LANES), lambda b, g, qi: (b, g)),
            pl.BlockSpec((1, seq // V_TILE, 1, 2 * V_ROWS, V_TILE), lambda b, g, qi: (b, 0, g, 0, 0)),
            pl.BlockSpec((1, nc, LANES), lambda b, g, qi: (b * N_KV + g, 0, 0)),
            pl.BlockSpec((1, LANES, nc), lambda b, g, qi: (b * N_KV + g, 0, 0)),
            pl.BlockSpec((LANES, nc), lambda b, g, qi: (0, 0)),
            pl.BlockSpec((HEAD_DIM, tq), lambda b, g, qi: (0, 0)),
            tab(), tab(),
        ],
        out_specs=pl.BlockSpec((tq, hq), lambda b, g, qi: (b * nq + qi, g)),
        scratch_shapes=[
            pltpu.VMEM((2 * LANES, cols), BF16), pltpu.VMEM((LANES, cols), BF16), acc(),
            stat(), stat(), acc(), stat(), stat(), acc(),
            pltpu.VMEM((tk, cols), F32), pltpu.VMEM((tk, cols), F32), pltpu.VMEM((tq, cols), F32), pltpu.VMEM((WINDOW, cols), F32),
            pltpu.VMEM((LANES, tq), F32),
        ],
        compiler_params=_params("parallel", "parallel", "arbitrary"),
        name="nsa_attention",
    )(qgt, qgt, kaug, vt, kvc, kvct, cmapt, qg, cos, sin)


def _rope_angles(seq):
    half = ROT_DIM // 2
    inv_freq = ROPE_THETA ** (-jnp.arange(half, dtype=F32) * (2.0 / ROT_DIM))
    ang = jnp.arange(seq, dtype=F32)[:, None] * inv_freq[None, :]
    return jnp.cos(ang), jnp.sin(ang)


def _rope_tables(cos, sin):
    seq, half = cos.shape
    zeros = jnp.zeros((seq, HEAD_DIM - ROT_DIM), F32)
    zh = jnp.zeros((seq, half), F32)
    c = jnp.concatenate([cos, cos, zeros + 1.0], axis=1)
    s1 = jnp.concatenate([-sin, zh, zeros], axis=1)
    s2 = jnp.concatenate([zh, sin, zeros], axis=1)
    reps = LANES // HEAD_DIM
    return tuple(jnp.tile(a, (1, reps)) for a in (c, s1, s2))


def _cmp_to_sel_t(nc, n_sel):
    start_c = np.arange(nc)[None, :] * CMP_STRIDE
    start_s = np.arange(LANES)[:, None] * SEL_LEN
    ov = np.minimum(start_c + CMP_LEN, start_s + SEL_LEN) - np.maximum(start_c, start_s)
    m = np.maximum(ov, 0).astype(np.float32) / CMP_LEN
    m[n_sel:, :] = 0.0
    m[:, nc - 1:] = 0.0
    return jnp.asarray(m, BF16)


def _nsa_in_weights(w_in):
    d = w_in.shape[0]
    hq, hk = N_HEADS * HEAD_DIM, N_KV * HEAD_DIM
    offs = np.cumsum([0, hq] + [hk] * 6)
    q, kc, vc, ks, vs, kw, vw = (w_in[:, offs[n]:offs[n + 1]] for n in range(7))
    gl = w_in[:, offs[7]:]
    per_group = lambda a, b: jnp.stack([a.reshape(d, N_KV, HEAD_DIM), b.reshape(d, N_KV, HEAD_DIM)],
                                       axis=2).reshape(d, N_KV * LANES)
    gates = jnp.pad(gl.reshape(d, N_KV, 3 * GROUP), ((0, 0), (0, 0), (0, GATE_ROWS - 3 * GROUP)))
    wn = jnp.concatenate([per_group(ks, kw), kc, vc], axis=1).astype(BF16)
    wt = jnp.concatenate([q, gates.reshape(d, N_KV * GATE_ROWS), per_group(vs, vw)], axis=1).T.astype(BF16)
    return wn, wt


def _nsa_layer(x, g, w_in, q_g, kc_g, ks_g, kw_g, pe_k, pe_v, ck_w1, ck_w2, cv_w1, cv_w2, w_out, batch, seq):
    nc = seq // CMP_STRIDE
    n_sel = seq // SEL_LEN
    assert n_sel <= LANES and seq % CMP_STRIDE == 0 and 3 * GROUP <= GATE_ROWS
    wn, wt = _nsa_in_weights(w_in)
    proj, kcvc, qgt, vt = _in_proj(x, g, wn, wt, batch, seq)
    cos, sin = _rope_angles(seq)
    seg = np.arange(LANES) // HEAD_DIM
    bd = jnp.asarray(seg[:, None] == seg[None, :], BF16)
    kaug = _kprep(proj, jnp.concatenate([ks_g, kw_g]).reshape(1, LANES), bd, _rope_tables(cos, sin), seq)
    kvc, kvct = _compress(kcvc, pe_k, pe_v, ck_w1, ck_w2, cv_w1, cv_w2, kc_g, batch, seq)
    tq = min(Q_TILE, seq)
    qg = jnp.broadcast_to((q_g * float(HEAD_DIM ** -0.5 * np.log2(np.e)))[:, None], (HEAD_DIM, tq))
    attn = _attention(qgt, kaug, vt, kvc, kvct, _cmp_to_sel_t(nc, n_sel), qg, cos.T, sin.T, batch, seq)
    return _out_proj(attn, x, w_out)


def kernel(x, mix_norm_g, ffn_norm_g, conv_w_pw1, conv_b_pw1, conv_w_dw, conv_b_dw, conv_ln_g, conv_ln_b, conv_w_pw2, conv_b_pw2, nsa_w_in, nsa_q_norm, nsa_kc_norm, nsa_ks_norm, nsa_kw_norm, nsa_pe_k, nsa_pe_v, nsa_ck_w1, nsa_ck_w2, nsa_cv_w1, nsa_cv_w2, nsa_w_out, ffn_w_up, ffn_w_dw, ffn_b_dw, ffn_w_down):
    batch, seq, d = x.shape
    depth = mix_norm_g.shape[0]
    n_mixers = 2
    h = x.reshape(batch * seq, d)
    w_up_all, w_down_all = ffn_w_up.astype(BF16), ffn_w_down.astype(BF16)
    for i in range(depth):
        j = i // n_mixers
        if i % n_mixers == 0:
            u = _conformer_glu(h, mix_norm_g[i], conv_w_pw1[j], conv_b_pw1[j])
            h = _conformer_conv_out(u, h, conv_w_dw[j], conv_b_dw[j], conv_ln_g[j], conv_ln_b[j],
                                    conv_w_pw2[j], conv_b_pw2[j], seq)
        else:
            h = _nsa_layer(h, mix_norm_g[i], nsa_w_in[j], nsa_q_norm[j], nsa_kc_norm[j], nsa_ks_norm[j],
                           nsa_kw_norm[j], nsa_pe_k[j], nsa_pe_v[j], nsa_ck_w1[j], nsa_ck_w2[j], nsa_cv_w1[j],
                           nsa_cv_w2[j], nsa_w_out[j], batch, seq)
        h = _conv_ffn(h, ffn_norm_g[i], w_up_all, ffn_w_dw[i], ffn_b_dw[i], w_down_all, i, seq)
    return h.reshape(batch, seq, d)
```

```python
import functools

import numpy as np
import jax
import jax.numpy as jnp
from jax import lax
from jax.experimental import pallas as pl
from jax.experimental.pallas import tpu as pltpu

N_HEADS = 16
HEAD_DIM = 64
N_KV = 4
GROUP = N_HEADS // N_KV
ROT_DIM = HEAD_DIM // 4
ROPE_THETA = 500000.0
CMP_LEN = 32
CMP_STRIDE = 16
SEL_LEN = 64
SEL_TOPK = 16
WINDOW = 512
EPS = 1e-6
NEG = -1e30

LANES = 128
SUBLANES = 8
BF16_ROWS = 16
V_TILE = 256
Q_TILE = 512
V_ROWS = HEAD_DIM + BF16_ROWS
GATE_ROWS = 16
VMEM_LIMIT = 48 * 1024 * 1024

F32 = jnp.float32
BF16 = jnp.bfloat16
NT_DIMS = (((1,), (1,)), ((), ()))


def _params(*sem):
    return pltpu.CompilerParams(dimension_semantics=sem, vmem_limit_bytes=VMEM_LIMIT)


def _dot(a, b):
    return jnp.dot(a, b, preferred_element_type=F32)


def _dot_nt(a, b):
    return lax.dot_general(a, b, NT_DIMS, preferred_element_type=F32)


def _split_bf16(x):
    hi = x.astype(BF16)
    lo = (x - hi.astype(F32)).astype(BF16)
    return hi, lo


def _rms_rows(x, g):
    ms = jnp.mean(x * x, axis=-1, keepdims=True)
    return x * lax.rsqrt(ms + EPS) * g


def _sigmoid(x):
    return 1.0 / (1.0 + jnp.exp(-x))


def _head_rms(xb, bd, gain):
    hi, lo = _split_bf16(xb * xb)
    ss = _dot(hi, bd) + _dot(lo, bd)
    return xb * lax.rsqrt(ss * (1.0 / HEAD_DIM) + EPS) * gain


def _rope(xb, c, s1, s2):
    half = ROT_DIM // 2
    return xb * c + pltpu.roll(xb, LANES - half, 1) * s1 + pltpu.roll(xb, half, 1) * s2


def _ffn_kernel(x_ref, halo_ref, g_ref, wup_ref, wdw_ref, bdw_ref, wd_ref, o_ref,
                hn_ref, a0_ref, v0_ref, a1_ref, v1_ref, *, tm, tiles_per_seq, nch):
    i = pl.program_id(0)
    hl = BF16_ROWS
    g = g_ref[...]
    hn_ref[hl:, :] = _rms_rows(x_ref[...], g).astype(BF16)
    first = (i % tiles_per_seq) == 0
    hn_ref[0:hl, :] = jnp.where(first, 0.0, _rms_rows(halo_ref[...], g)).astype(BF16)
    o_ref[...] = x_ref[...]

    tf = a0_ref.shape[1]

    def up(c, a_ref, v_ref):
        a_ref[...] = _dot(hn_ref[...], wup_ref[:, pl.ds(pl.multiple_of(c * tf, tf), tf)])
        v_ref[...] = _dot(hn_ref[hl:, :], wup_ref[:, pl.ds(pl.multiple_of((nch + c) * tf, tf), tf)])

    def down(c, a_ref, v_ref):
        w = wdw_ref[c]
        cv = (w[0:1] * a_ref[hl - 2:hl - 2 + tm, :] + w[1:2] * a_ref[hl - 1:hl - 1 + tm, :]
              + w[2:3] * a_ref[hl:hl + tm, :] + bdw_ref[c])
        act = (cv * _sigmoid(cv) * v_ref[...]).astype(BF16)
        o_ref[...] += _dot(act, wd_ref[c])

    buf0, buf1 = (a0_ref, v0_ref), (a1_ref, v1_ref)
    up(0, *buf0)

    def pair_body(p, carry):
        c = 2 * p
        up(c + 1, *buf1)
        down(c, *buf0)
        up(c + 2, *buf0)
        down(c + 1, *buf1)
        return carry

    lax.fori_loop(0, (nch - 1) // 2, pair_body, 0)
    if nch % 2 == 1:
        down(nch - 1, *buf0)
    else:
        up(nch - 1, *buf1)
        down(nch - 2, *buf0)
        down(nch - 1, *buf1)


def _conv_ffn(x, g, w_up_all, w_dw, b_dw, w_down_all, layer, seq):
    t, d = x.shape
    dff = w_down_all.shape[1]
    tm = min(512, seq)
    tf = 256
    assert seq % tm == 0 and dff % tf == 0 and w_dw.shape[0] == 3
    nch = dff // tf
    hl = BF16_ROWS
    kern = functools.partial(_ffn_kernel, tm=tm, tiles_per_seq=seq // tm, nch=nch)
    wup = w_up_all
    wdn = w_down_all.reshape(w_down_all.shape[0], nch, tf, d)
    wdw = w_dw.reshape(3, nch, tf).transpose(1, 0, 2)
    bdw = b_dw.reshape(nch, 1, tf)
    resident = lambda a: pl.BlockSpec(a.shape, lambda i: (0,) * a.ndim, pipeline_mode=pl.Buffered(1))
    of_layer = lambda a: pl.BlockSpec((None,) + a.shape[1:], lambda i: (layer,) + (0,) * (a.ndim - 1),
                                      pipeline_mode=pl.Buffered(1))
    return pl.pallas_call(
        kern,
        out_shape=jax.ShapeDtypeStruct((t, d), F32),
        grid=(t // tm,),
        in_specs=[
            pl.BlockSpec((tm, d), lambda i: (i, 0)),
            pl.BlockSpec((hl, d), lambda i: (jnp.maximum(i * (tm // hl) - 1, 0), 0)),
            pl.BlockSpec((1, d), lambda i: (0, 0)),
            of_layer(wup), resident(wdw), resident(bdw), of_layer(wdn),
        ],
        out_specs=pl.BlockSpec((tm, d), lambda i: (i, 0)),
        scratch_shapes=[
            pltpu.VMEM((tm + hl, d), BF16),
            pltpu.VMEM((tm + hl, tf), F32), pltpu.VMEM((tm, tf), F32),
            pltpu.VMEM((tm + hl, tf), F32), pltpu.VMEM((tm, tf), F32),
        ],
        compiler_params=_params("parallel"),
        name="conv_ffn",
    )(x, x, g.reshape(1, d), wup, wdw, bdw, wdn)


def _glu_kernel(x_ref, g_ref, wa_ref, wg_ref, ba_ref, bg_ref, o_ref, hn_ref):
    @pl.when(pl.program_id(1) == 0)
    def _():
        hn_ref[...] = _rms_rows(x_ref[...], g_ref[...]).astype(BF16)

    th = hn_ref.shape[0] // 2
    halves = [(_dot(hn_ref[h * th:(h + 1) * th, :], wa_ref[...]), _dot(hn_ref[h * th:(h + 1) * th, :], wg_ref[...]))
              for h in range(2)]
    for h, (a, gate) in enumerate(halves):
        o_ref[h * th:(h + 1) * th, :] = (a + ba_ref[...]) * _sigmoid(gate + bg_ref[...])


def _conformer_glu(x, g, w_pw1, b_pw1):
    t, d = x.shape
    tm = min(1024, t)
    tn = 256
    nj = d // tn
    w = w_pw1.astype(BF16)
    b = b_pw1.reshape(1, 2 * d)
    return pl.pallas_call(
        _glu_kernel,
        out_shape=jax.ShapeDtypeStruct((t, d), F32),
        grid=(t // tm, nj),
        in_specs=[
            pl.BlockSpec((tm, d), lambda i, j: (i, 0)),
            pl.BlockSpec((1, d), lambda i, j: (0, 0)),
            pl.BlockSpec((d, tn), lambda i, j: (0, j)),
            pl.BlockSpec((d, tn), lambda i, j: (0, j + nj)),
            pl.BlockSpec((1, tn), lambda i, j: (0, j)),
            pl.BlockSpec((1, tn), lambda i, j: (0, j + nj)),
        ],
        out_specs=pl.BlockSpec((tm, tn), lambda i, j: (i, j)),
        scratch_shapes=[pltpu.VMEM((tm, d), BF16)],
        compiler_params=_params("parallel", "arbitrary"),
        name="conformer_glu",
    )(x, g.reshape(1, d), w, w, b, b)


def _dwconv_kernel(u_ref, halo_ref, x_ref, wdw_ref, bdw_ref, lng_ref, lnb_ref, w2_ref, b2_ref, o_ref,
                   ext_ref, cv_ref, slab_ref, *, tm, halo, width, tiles_per_seq, row_chunk, col_chunk):
    i = pl.program_id(0)
    d = u_ref.shape[1]
    first = (i % tiles_per_seq) == 0
    ext_ref[0:halo, :] = jnp.where(first, 0.0, halo_ref[...])
    ext_ref[halo:, :] = u_ref[...]
    off = halo - (width - 1)
    for r0 in range(0, tm, row_chunk):
        for c0 in range(0, d, col_chunk):
            cs = slice(c0, c0 + col_chunk)
            acc = jnp.broadcast_to(bdw_ref[:, cs], (row_chunk, col_chunk))
            for r in range(min(SUBLANES, width)):
                taps = range(r, width, SUBLANES)
                lo = r0 + off + r
                rows = row_chunk + (len(taps) - 1) * SUBLANES
                slab_ref[r, 0:rows, :] = ext_ref[lo:lo + rows, cs]
                for q, k in enumerate(taps):
                    acc = acc + wdw_ref[k:k + 1, cs] * slab_ref[r, q * SUBLANES:q * SUBLANES + row_chunk, :]
            cv_ref[r0:r0 + row_chunk, cs] = acc
    u = cv_ref[...]
    mu = jnp.mean(u, axis=-1, keepdims=True)
    uc = u - mu
    var = jnp.mean(uc * uc, axis=-1, keepdims=True)
    y = uc * lax.rsqrt(var + EPS) * lng_ref[...] + lnb_ref[...]
    s = (y * _sigmoid(y)).astype(BF16)
    o_ref[...] = x_ref[...] + _dot(s, w2_ref[...]) + b2_ref[...]


def _conformer_conv_out(u, x, w_dw, b_dw, ln_g, ln_b, w_pw2, b_pw2, seq):
    t, d = x.shape
    width = w_dw.shape[0]
    halo = 32
    assert width - 1 <= halo
    tm = min(256, seq)
    assert seq % tm == 0 and tm % halo == 0
    row_chunk, col_chunk = 64, 256
    kern = functools.partial(_dwconv_kernel, tm=tm, halo=halo, width=width, tiles_per_seq=seq // tm,
                             row_chunk=row_chunk, col_chunk=col_chunk)
    slab_rows = row_chunk + (width - 1) // SUBLANES * SUBLANES
    vec = lambda: pl.BlockSpec((1, d), lambda i: (0, 0))
    return pl.pallas_call(
        kern,
        out_shape=jax.ShapeDtypeStruct((t, d), F32),
        grid=(t // tm,),
        in_specs=[
            pl.BlockSpec((tm, d), lambda i: (i, 0)),
            pl.BlockSpec((halo, d), lambda i: (jnp.maximum(i * (tm // halo) - 1, 0), 0)),
            pl.BlockSpec((tm, d), lambda i: (i, 0)),
            pl.BlockSpec((width, d), lambda i: (0, 0)),
            vec(), vec(), vec(),
            pl.BlockSpec((d, d), lambda i: (0, 0)),
            vec(),
        ],
        out_specs=pl.BlockSpec((tm, d), lambda i: (i, 0)),
        scratch_shapes=[pltpu.VMEM((tm + halo, d), F32), pltpu.VMEM((tm, d), F32),
                        pltpu.VMEM((SUBLANES, slab_rows, col_chunk), F32)],
        compiler_params=_params("parallel"),
        name="conformer_dwconv_out",
    )(u, u, x, w_dw, b_dw.reshape(1, d), ln_g.reshape(1, d), ln_b.reshape(1, d), w_pw2.astype(BF16),
      b_pw2.reshape(1, d))


def _in_proj_kernel(x_ref, g_ref, wn_ref, wt_ref, perm_ref, on_ref, oc_ref, oq_ref, ov_ref, *, n_qg, row_chunk):
    hn = _rms_rows(x_ref[...], g_ref[...]).astype(BF16)
    nk = on_ref.shape[1]
    on_ref[...] = _dot(hn, wn_ref[:, 0:nk])
    kcvc = _dot(hn, wn_ref[:, nk:]).astype(BF16)
    regrouped = _dot(perm_ref[...], kcvc).astype(BF16)
    per = regrouped.shape[0] // CMP_STRIDE
    for l in range(CMP_STRIDE):
        oc_ref[0, l] = regrouped[l * per:(l + 1) * per, :]
    for r0 in range(0, n_qg, row_chunk):
        r1 = min(r0 + row_chunk, n_qg)
        oq_ref[r0:r1, :] = _dot_nt(wt_ref[r0:r1, :], hn)
    vt = _dot_nt(wt_ref[n_qg:, :], hn).astype(BF16)
    ones = jnp.ones((BF16_ROWS, V_TILE), BF16)
    for jj in range(ov_ref.shape[1]):
        for g in range(N_KV):
            for half in range(2):
                r0 = g * LANES + half * HEAD_DIM
                ov_ref[0, jj, g, half * V_ROWS:half * V_ROWS + HEAD_DIM, :] = (
                    vt[r0:r0 + HEAD_DIM, jj * V_TILE:(jj + 1) * V_TILE])
                ov_ref[0, jj, g, half * V_ROWS + HEAD_DIM:(half + 1) * V_ROWS, :] = ones


def _in_proj(x, g, wn, wt, batch, seq):
    t, d = x.shape
    tm = min(512, seq)
    n_qg = N_HEADS * HEAD_DIM + N_KV * GATE_ROWS
    assert seq % tm == 0 and tm % V_TILE == 0 and wt.shape[0] == n_qg + N_KV * LANES
    per_seq = seq // tm
    nk = N_KV * LANES
    ncv = wn.shape[1] - nk
    per = tm // CMP_STRIDE
    rows = np.arange(tm)
    perm = np.zeros((tm, tm), np.float32)
    perm[(rows % CMP_STRIDE) * per + rows // CMP_STRIDE, rows] = 1.0
    kern = functools.partial(_in_proj_kernel, n_qg=n_qg, row_chunk=512)
    return pl.pallas_call(
        kern,
        out_shape=(jax.ShapeDtypeStruct((t, nk), F32),
                   jax.ShapeDtypeStruct((t // tm, CMP_STRIDE, per, ncv), BF16),
                   jax.ShapeDtypeStruct((n_qg, t), F32),
                   jax.ShapeDtypeStruct((batch, seq // V_TILE, N_KV, 2 * V_ROWS, V_TILE), BF16)),
        grid=(t // tm,),
        in_specs=[
            pl.BlockSpec((tm, d), lambda i: (i, 0)),
            pl.BlockSpec((1, d), lambda i: (0, 0)),
            pl.BlockSpec(wn.shape, lambda i: (0, 0)),
            pl.BlockSpec(wt.shape, lambda i: (0, 0)),
            pl.BlockSpec((tm, tm), lambda i: (0, 0)),
        ],
        out_specs=(pl.BlockSpec((tm, nk), lambda i: (i, 0)),
                   pl.BlockSpec((1, CMP_STRIDE, per, ncv), lambda i: (i, 0, 0, 0)),
                   pl.BlockSpec((n_qg, tm), lambda i: (0, i)),
                   pl.BlockSpec((1, tm // V_TILE, N_KV, 2 * V_ROWS, V_TILE),
                                lambda i: (i // per_seq, i % per_seq, 0, 0, 0))),
        compiler_params=_params("parallel"),
        name="nsa_in_proj",
    )(x, g.reshape(1, d), wn, wt, jnp.asarray(perm, BF16))


def _out_proj_kernel(a_ref, x_ref, w_ref, o_ref):
    o_ref[...] = x_ref[...] + _dot(a_ref[...], w_ref[...])


def _out_proj(a, x, w):
    t, d = x.shape
    tm = min(512, t)
    return pl.pallas_call(
        _out_proj_kernel,
        out_shape=jax.ShapeDtypeStruct((t, d), F32),
        grid=(t // tm,),
        in_specs=[
            pl.BlockSpec((tm, a.shape[1]), lambda i: (i, 0)),
            pl.BlockSpec((tm, d), lambda i: (i, 0)),
            pl.BlockSpec(w.shape, lambda i: (0, 0)),
        ],
        out_specs=pl.BlockSpec((tm, d), lambda i: (i, 0)),
        compiler_params=_params("parallel"),
        name="nsa_out_proj",
    )(a, x, w.astype(BF16))


def _kprep_kernel(k_ref, gain_ref, bd_ref, c_ref, s1_ref, s2_ref, kaug_ref, *, ts, seq):
    i = pl.program_id(0)
    bd = bd_ref[...]
    gain = gain_ref[...]
    c, s1, s2 = c_ref[...], s1_ref[...], s2_ref[...]
    tpos = (i * ts) % seq + lax.broadcasted_iota(jnp.int32, (ts, LANES), 0)
    lane = lax.broadcasted_iota(jnp.int32, (ts, LANES), 1)
    onehot = jnp.where(tpos // SEL_LEN == lane, 1.0, 0.0).astype(BF16)
    for g in range(N_KV):
        xb = k_ref[:, g * LANES:(g + 1) * LANES]
        xr = _rope(_head_rms(xb, bd, gain), c, s1, s2)
        kaug_ref[:, 2 * g * LANES:(2 * g + 1) * LANES] = xr.astype(BF16)
        kaug_ref[:, (2 * g + 1) * LANES:(2 * g + 2) * LANES] = onehot


def _kprep(proj, gain, bd, tabs, seq):
    t = proj.shape[0]
    ts = min(512, seq)
    assert seq % ts == 0
    nk = N_KV * LANES
    kern = functools.partial(_kprep_kernel, ts=ts, seq=seq)
    tab = lambda: pl.BlockSpec((ts, LANES), lambda i: (i % (seq // ts), 0))
    return pl.pallas_call(
        kern,
        out_shape=jax.ShapeDtypeStruct((t, 2 * nk), BF16),
        grid=(t // ts,),
        in_specs=[
            pl.BlockSpec((ts, nk), lambda i: (i, 0)),
            pl.BlockSpec((1, LANES), lambda i: (0, 0)),
            pl.BlockSpec((LANES, LANES), lambda i: (0, 0)),
            tab(), tab(), tab(),
        ],
        out_specs=pl.BlockSpec((ts, 2 * nk), lambda i: (i, 0)),
        compiler_params=_params("parallel"),
        name="nsa_key_prep",
    )(proj, gain, bd, *tabs)


def _compress_kernel(x_ref, w1k_ref, w1v_ref, ck_ref, cv_ref, w2k_ref, w2v_ref, gain_ref, o_ref, ot_ref,
                     acck_ref, accv_ref):
    l = pl.program_id(1)
    nc = acck_ref.shape[0]
    hid = w2k_ref.shape[0]
    hk = w1k_ref.shape[1]

    @pl.when(l == 0)
    def _():
        acck_ref[...] = jnp.zeros_like(acck_ref)
        accv_ref[...] = jnp.zeros_like(accv_ref)

    x = x_ref[0].reshape(nc, 2 * hk)
    acck_ref[...] += _dot(x[:, 0:hk], w1k_ref[0])
    accv_ref[...] += _dot(x[:, hk:], w1v_ref[0])

    @pl.when(l == pl.num_programs(1) - 1)
    def _():
        for g in range(N_KV):
            def hidden(acc_ref, c_ref):
                first = acc_ref[:, g * hid:(g + 1) * hid] + c_ref[0:1, :]
                second = acc_ref[:, (N_KV + g) * hid:(N_KV + g + 1) * hid] + c_ref[1:2, :]
                pre = first + pltpu.roll(second, nc - 1, 0)
                return (pre * _sigmoid(pre)).astype(BF16)

            kv = _dot(hidden(acck_ref, ck_ref), w2k_ref[...]) + _dot(hidden(accv_ref, cv_ref), w2v_ref[...])
            is_k = lax.broadcasted_iota(jnp.int32, kv.shape, 1) < HEAD_DIM
            ss = jnp.sum(jnp.where(is_k, kv * kv, 0.0), axis=-1, keepdims=True)
            kn = kv * lax.rsqrt(ss * (1.0 / HEAD_DIM) + EPS) * gain_ref[...]
            out = jnp.where(is_k, kn, kv)
            o_ref[g] = out.astype(BF16)
            ot_ref[g] = out.T.astype(BF16)


def _compress(kcvc, pe_k, pe_v, ck_w1, ck_w2, cv_w1, cv_w2, kc_g, batch, seq):
    nc = seq // CMP_STRIDE
    hid = ck_w1.shape[1]
    hk = N_KV * HEAD_DIM
    tiles, _, per, width = kcvc.shape
    assert CMP_LEN == 2 * CMP_STRIDE and width == 2 * hk and tiles * per == batch * nc
    x5 = kcvc.reshape(batch, tiles // batch, CMP_STRIDE, per, width)
    pad = lambda w, left: jnp.pad(w, ((0, 0), (HEAD_DIM, 0) if left else (0, HEAD_DIM))).astype(BF16)
    gain = jnp.concatenate([kc_g, jnp.ones((HEAD_DIM,), F32)]).reshape(1, LANES)

    def first_layer(pe, w1):
        w = w1.reshape(2, CMP_STRIDE, HEAD_DIM, hid)
        rows = lambda a, k: jnp.pad(a, ((0, 0), (k * HEAD_DIM, (N_KV - 1 - k) * HEAD_DIM), (0, 0)))
        bd = jnp.concatenate([rows(w[s], k) for s in range(2) for k in range(N_KV)], axis=2)
        const = jnp.einsum("sld,sldh->sh", pe.reshape(2, CMP_STRIDE, HEAD_DIM), w, precision=lax.Precision.HIGHEST)
        return bd.astype(BF16), const

    w1k, const_k = first_layer(pe_k, ck_w1)
    w1v, const_v = first_layer(pe_v, cv_w1)
    wide = 2 * N_KV * hid
    step_w = lambda: pl.BlockSpec((1, hk, wide), lambda b, l: (l, 0, 0))
    full = lambda shape: pl.BlockSpec(shape, lambda b, l: (0,) * len(shape))
    return pl.pallas_call(
        _compress_kernel,
        out_shape=(jax.ShapeDtypeStruct((batch * N_KV, nc, LANES), BF16),
                   jax.ShapeDtypeStruct((batch * N_KV, LANES, nc), BF16)),
        grid=(batch, CMP_STRIDE),
        in_specs=[pl.BlockSpec((1, tiles // batch, None, per, width), lambda b, l: (b, 0, l, 0, 0)),
                  step_w(), step_w(), full((2, hid)), full((2, hid)),
                  full((hid, LANES)), full((hid, LANES)), full((1, LANES))],
        out_specs=(pl.BlockSpec((N_KV, nc, LANES), lambda b, l: (b, 0, 0)),
                   pl.BlockSpec((N_KV, LANES, nc), lambda b, l: (b, 0, 0))),
        scratch_shapes=[pltpu.VMEM((nc, wide), F32), pltpu.VMEM((nc, wide), F32)],
        compiler_params=_params("parallel", "arbitrary"),
        name="nsa_compress",
    )(x5, w1k, w1v, const_k, const_v, pad(ck_w2, False), pad(cv_w2, True), gain)


def _attn_kernel(qt_ref, gate_ref, kaug_ref, vt_ref, kvc_ref, kvct_ref, cmapt_ref, qg_ref, cos_ref, sin_ref,
                 o_ref,
                 qaug_ref, qwin_ref, oc_ref, ms_ref, ls_ref, accs_ref, mw_ref, lw_ref, accw_ref,
                 sa_ref, sb_ref, wdiag_ref, wold_ref, imp_ref,
                 *, tq, tk, k_top, n_cmp_var):
    qi = pl.program_id(2)
    t0 = qi * tq
    cols = GROUP * tq
    half = ROT_DIM // 2

    gain = qg_ref[...]
    cos, sin = cos_ref[...], sin_ref[...]
    nope, rope = [], []
    for r in range(GROUP):
        x = qt_ref[r * HEAD_DIM:(r + 1) * HEAD_DIM, :]
        ss = jnp.sum(x * x, axis=0, keepdims=True)
        xn = x * lax.rsqrt(ss * (1.0 / HEAD_DIM) + EPS) * gain
        x1, x2 = xn[0:half], xn[half:ROT_DIM]
        nope.append(xn)
        rope.append(jnp.concatenate([x1 * cos - x2 * sin, x2 * cos + x1 * sin, xn[ROT_DIM:]], axis=0))
    zeros = jnp.zeros((HEAD_DIM, cols), BF16)
    q_nope = jnp.concatenate([jnp.concatenate(nope, axis=1).astype(BF16), zeros], axis=0)
    q_rope = jnp.concatenate(rope, axis=1).astype(BF16)
    qwin_ref[...] = jnp.concatenate([zeros, q_rope], axis=0)

    tpos = t0 + lax.broadcasted_iota(jnp.int32, (1, tq), 1)

    def per_head(x):
        return jnp.concatenate([x] * GROUP, axis=1)

    def fold_tile(s, keep, k0, v_rows, m_ref, l_ref, acc_ref):
        if keep is not None:
            s = jnp.where(per_head(keep), s, NEG)
        m_prev = m_ref[...]
        m_new = jnp.maximum(m_prev, jnp.max(s, axis=0, keepdims=True))
        p = jnp.exp2(s - m_new)
        alpha = jnp.exp2(m_prev - m_new)
        pb = p.astype(BF16)
        v_tile0 = k0 // V_TILE
        pv = _dot(vt_ref[0, v_tile0, 0, v_rows, :], pb[0:V_TILE])
        for c in range(1, s.shape[0] // V_TILE):
            pv = pv + _dot(vt_ref[0, v_tile0 + c, 0, v_rows, :], pb[c * V_TILE:(c + 1) * V_TILE])
        l_ref[...] = alpha * l_ref[...] + pv[HEAD_DIM:HEAD_DIM + 1]
        acc_ref[...] = alpha * acc_ref[...] + pv[0:HEAD_DIM]
        m_ref[...] = m_new

    def back(k0, size):
        return tpos - (k0 + lax.broadcasted_iota(jnp.int32, (size, 1), 0))

    def sel_scores(k0):
        return _dot(kaug_ref[pl.ds(k0, tk), :], qaug_ref[...])

    def win_scores(k0, size):
        return _dot(kaug_ref[pl.ds(k0, size), 0:LANES], qwin_ref[...])

    for m_ref, l_ref, acc_ref in ((ms_ref, ls_ref, accs_ref), (mw_ref, lw_ref, accw_ref)):
        m_ref[...] = jnp.full_like(m_ref, NEG)
        l_ref[...] = jnp.zeros_like(l_ref)
        acc_ref[...] = jnp.zeros_like(acc_ref)

    t0a = pl.multiple_of(t0, tq)
    k_old = pl.multiple_of(jnp.maximum(t0 - WINDOW, 0), tq)
    wdiag_ref[...] = win_scores(t0a, tq)
    wold_ref[...] = win_scores(k_old, WINDOW)
    qaug_ref[0:LANES, :] = jnp.concatenate([q_rope, zeros], axis=0)
    sa_ref[...] = _dot(kaug_ref[0:tk, 0:LANES], qaug_ref[0:LANES, :])
    win_state = (slice(V_ROWS, 2 * V_ROWS), mw_ref, lw_ref, accw_ref)
    fold_tile(wdiag_ref[...], back(t0a, tq) >= 0, t0a, *win_state)
    b_old = back(k_old, WINDOW)
    fold_tile(wold_ref[...], (b_old < WINDOW) & (b_old > tpos - t0), k_old, *win_state)

    nc = kvc_ref.shape[1]

    def cmp_branch(rows):
        kvc = kvc_ref[0, 0:rows, :]
        cmp_end = lax.broadcasted_iota(jnp.int32, (rows, 1), 0) * CMP_STRIDE + (CMP_LEN - 1)
        s_c = jnp.where(per_head(cmp_end <= tpos), _dot(kvc, q_nope), NEG)
        e = jnp.exp2(s_c - jnp.maximum(jnp.max(s_c, axis=0, keepdims=True), 0.1 * NEG))
        p_c = e * (1.0 / jnp.maximum(jnp.sum(e, axis=0, keepdims=True), 1e-30))
        oc_ref[...] = _dot(kvct_ref[0, HEAD_DIM:, 0:rows], p_c.astype(BF16))
        p_sum = p_c[:, 0:tq]
        for r in range(1, GROUP):
            p_sum = p_sum + p_c[:, r * tq:(r + 1) * tq]
        hi, lo = _split_bf16(p_sum)
        imp_ref[...] = _dot(cmapt_ref[:, 0:rows], hi) + _dot(cmapt_ref[:, 0:rows], lo)

    variant = (t0 + tq - 1) // (CMP_STRIDE * (nc // n_cmp_var))
    for v in range(n_cmp_var):
        pl.when(variant == v)(functools.partial(cmp_branch, (v + 1) * (nc // n_cmp_var)))
    imp = imp_ref[...]

    blk = lax.broadcasted_iota(jnp.int32, (LANES, tq), 0)
    cur = tpos // SEL_LEN
    forced = (blk == 0) | (blk == cur) | (blk == cur - 1)
    taken = -3e38
    imp = jnp.where(blk > cur, -1.0, jnp.where(forced, taken, imp))
    blk_f = blk.astype(F32)
    for _ in range(max(k_top - 3, 0)):
        mx = jnp.max(imp, axis=0, keepdims=True)
        idx = jnp.min(jnp.where(imp == mx, blk_f, float(LANES)), axis=0, keepdims=True)
        imp = jnp.where(blk_f == idx, taken, imp)
    bias = jnp.where(imp == taken, 0.0, NEG)
    qaug_ref[LANES:, :] = per_head(bias.astype(BF16))
    blocks0 = tk // SEL_LEN
    bias0 = jnp.broadcast_to(bias[0:blocks0, None, :], (blocks0, SEL_LEN, tq)).reshape(tk, tq)
    sa_ref[...] = sa_ref[...] + per_head(bias0)


    n = t0 // tk
    sel_state = (slice(0, V_ROWS), ms_ref, ls_ref, accs_ref)

    def pair_body(i, carry):
        k0 = pl.multiple_of(2 * i * tk, tk)
        sb_ref[...] = sel_scores(k0 + tk)
        fold_tile(sa_ref[...], None, k0, *sel_state)
        sa_ref[...] = sel_scores(k0 + 2 * tk)
        fold_tile(sb_ref[...], None, k0 + tk, *sel_state)
        return carry

    lax.fori_loop(0, n // 2, pair_body, 0)
    k_diag = pl.multiple_of(n * tk, tk)

    def diagonal_fold(buf_ref):
        if tk == tq:
            fold_tile(buf_ref[...], back(k_diag, tk) >= 0, k_diag, *sel_state)
            return
        pl.when(t0 % tk == 0)(lambda: fold_tile(buf_ref[0:tq, :], back(k_diag, tq) >= 0, k_diag, *sel_state))
        pl.when(t0 % tk != 0)(lambda: fold_tile(buf_ref[...], back(k_diag, tk) >= 0, k_diag, *sel_state))

    @pl.when(n % 2 == 1)
    def _():
        sb_ref[...] = sel_scores(k_diag)
        fold_tile(sa_ref[...], None, k_diag - tk, *sel_state)
        diagonal_fold(sb_ref)

    @pl.when(n % 2 == 0)
    def _():
        diagonal_fold(sa_ref)

    o_s = accs_ref[...] / ls_ref[...]
    o_w = accw_ref[...] / lw_ref[...]
    o_c = oc_ref[...]
    sig = _sigmoid(gate_ref[...])
    heads = []
    for r in range(GROUP):
        cs = slice(r * tq, (r + 1) * tq)
        g_c, g_s, g_w = (sig[3 * r + b:3 * r + b + 1, :] for b in range(3))
        heads.append(g_c * o_c[:, cs] + g_s * o_s[:, cs] + g_w * o_w[:, cs])
    o_ref[...] = jnp.concatenate(heads, axis=0).T.astype(o_ref.dtype)


def _attention(qgt, kaug, vt, kvc, kvct, cmapt, qg, cos, sin, batch, seq):
    t = qgt.shape[1]
    tq = min(Q_TILE, seq)
    tk = min(512, seq)
    nq = seq // tq
    nc = kvc.shape[1]
    assert seq % tk == 0 and tk % tq == 0 and WINDOW % tq == 0 and tq % V_TILE == 0 and seq >= WINDOW
    cols = GROUP * tq
    hq = GROUP * HEAD_DIM
    k_top = min(SEL_TOPK, seq // SEL_LEN)
    n_cmp_var = max(1, min(4, nc // LANES))
    assert k_top >= 3 and nc % n_cmp_var == 0
    kern = functools.partial(_attn_kernel, tq=tq, tk=tk, k_top=k_top, n_cmp_var=n_cmp_var)
    tab = lambda: pl.BlockSpec((ROT_DIM // 2, tq), lambda b, g, qi: (0, qi))
    stat = lambda: pltpu.VMEM((1, cols), F32)
    acc = lambda: pltpu.VMEM((HEAD_DIM, cols), F32)
    return pl.pallas_call(
        kern,
        out_shape=jax.ShapeDtypeStruct((t, N_HEADS * HEAD_DIM), BF16),
        grid=(batch, N_KV, nq),
        in_specs=[
            pl.BlockSpec((hq, tq), lambda b, g, qi: (g, b * nq + qi)),
            pl.BlockSpec((GATE_ROWS, tq), lambda b, g, qi: (N_HEADS * HEAD_DIM // GATE_ROWS + g, b * nq + qi)),
            pl.BlockSpec((seq, 2 * LANES), lambda b, g, qi: (b, g)),
            pl.BlockSpec((1, seq // V_TILE, 1, 2 * V_ROWS, V_TILE), lambda b, g, qi: (b, 0, g, 0, 0)),
            pl.BlockSpec((1, nc, LANES), lambda b, g, qi: (b * N_KV + g, 0, 0)),
            pl.BlockSpec((1, LANES, nc), lambda b, g, qi: (b * N_KV + g, 0, 0)),
            pl.BlockSpec((LANES, nc), lambda b, g, qi: (0, 0)),
            pl.BlockSpec((HEAD_DIM, tq), lambda b, g, qi: (0, 0)),
            tab(), tab(),
        ],
        out_specs=pl.BlockSpec((tq, hq), lambda b, g, qi: (b * nq + qi, g)),
        scratch_shapes=[
            pltpu.VMEM((2 * LANES, cols), BF16), pltpu.VMEM((LANES, cols), BF16), acc(),
            stat(), stat(), acc(), stat(), stat(), acc(),
            pltpu.VMEM((tk, cols), F32), pltpu.VMEM((tk, cols), F32), pltpu.VMEM((tq, cols), F32), pltpu.VMEM((WINDOW, cols), F32),
            pltpu.VMEM((LANES, tq), F32),
        ],
        compiler_params=_params("parallel", "parallel", "arbitrary"),
        name="nsa_attention",
    )(qgt, qgt, kaug, vt, kvc, kvct, cmapt, qg, cos, sin)


def _rope_angles(seq):
    half = ROT_DIM // 2
    inv_freq = ROPE_THETA ** (-jnp.arange(half, dtype=F32) * (2.0 / ROT_DIM))
    ang = jnp.arange(seq, dtype=F32)[:, None] * inv_freq[None, :]
    return jnp.cos(ang), jnp.sin(ang)


def _rope_tables(cos, sin):
    seq, half = cos.shape
    zeros = jnp.zeros((seq, HEAD_DIM - ROT_DIM), F32)
    zh = jnp.zeros((seq, half), F32)
    c = jnp.concatenate([cos, cos, zeros + 1.0], axis=1)
    s1 = jnp.concatenate([-sin, zh, zeros], axis=1)
    s2 = jnp.concatenate([zh, sin, zeros], axis=1)
    reps = LANES // HEAD_DIM
    return tuple(jnp.tile(a, (1, reps)) for a in (c, s1, s2))


def _cmp_to_sel_t(nc, n_sel):
    start_c = np.arange(nc)[None, :] * CMP_STRIDE
    start_s = np.arange(LANES)[:, None] * SEL_LEN
    ov = np.minimum(start_c + CMP_LEN, start_s + SEL_LEN) - np.maximum(start_c, start_s)
    m = np.maximum(ov, 0).astype(np.float32) / CMP_LEN
    m[n_sel:, :] = 0.0
    m[:, nc - 1:] = 0.0
    return jnp.asarray(m, BF16)


def _nsa_in_weights(w_in):
    d = w_in.shape[0]
    hq, hk = N_HEADS * HEAD_DIM, N_KV * HEAD_DIM
    offs = np.cumsum([0, hq] + [hk] * 6)
    q, kc, vc, ks, vs, kw, vw = (w_in[:, offs[n]:offs[n + 1]] for n in range(7))
    gl = w_in[:, offs[7]:]
    per_group = lambda a, b: jnp.stack([a.reshape(d, N_KV, HEAD_DIM), b.reshape(d, N_KV, HEAD_DIM)],
                                       axis=2).reshape(d, N_KV * LANES)
    gates = jnp.pad(gl.reshape(d, N_KV, 3 * GROUP), ((0, 0), (0, 0), (0, GATE_ROWS - 3 * GROUP)))
    wn = jnp.concatenate([per_group(ks, kw), kc, vc], axis=1).astype(BF16)
    wt = jnp.concatenate([q, gates.reshape(d, N_KV * GATE_ROWS), per_group(vs, vw)], axis=1).T.astype(BF16)
    return wn, wt


def _nsa_layer(x, g, w_in, q_g, kc_g, ks_g, kw_g, pe_k, pe_v, ck_w1, ck_w2, cv_w1, cv_w2, w_out, batch, seq):
    nc = seq // CMP_STRIDE
    n_sel = seq // SEL_LEN
    assert n_sel <= LANES and seq % CMP_STRIDE == 0 and 3 * GROUP <= GATE_ROWS
    wn, wt = _nsa_in_weights(w_in)
    proj, kcvc, qgt, vt = _in_proj(x, g, wn, wt, batch, seq)
    cos, sin = _rope_angles(seq)
    seg = np.arange(LANES) // HEAD_DIM
    bd = jnp.asarray(seg[:, None] == seg[None, :], BF16)
    kaug = _kprep(proj, jnp.concatenate([ks_g, kw_g]).reshape(1, LANES), bd, _rope_tables(cos, sin), seq)
    kvc, kvct = _compress(kcvc, pe_k, pe_v, ck_w1, ck_w2, cv_w1, cv_w2, kc_g, batch, seq)
    tq = min(Q_TILE, seq)
    qg = jnp.broadcast_to((q_g * float(HEAD_DIM ** -0.5 * np.log2(np.e)))[:, None], (HEAD_DIM, tq))
    attn = _attention(qgt, kaug, vt, kvc, kvct, _cmp_to_sel_t(nc, n_sel), qg, cos.T, sin.T, batch, seq)
    return _out_proj(attn, x, w_out)


def kernel(x, mix_norm_g, ffn_norm_g, conv_w_pw1, conv_b_pw1, conv_w_dw, conv_b_dw, conv_ln_g, conv_ln_b, conv_w_pw2, conv_b_pw2, nsa_w_in, nsa_q_norm, nsa_kc_norm, nsa_ks_norm, nsa_kw_norm, nsa_pe_k, nsa_pe_v, nsa_ck_w1, nsa_ck_w2, nsa_cv_w1, nsa_cv_w2, nsa_w_out, ffn_w_up, ffn_w_dw, ffn_b_dw, ffn_w_down):
    batch, seq, d = x.shape
    depth = mix_norm_g.shape[0]
    n_mixers = 2
    h = x.reshape(batch * seq, d)
    w_up_all, w_down_all = ffn_w_up.astype(BF16), ffn_w_down.astype(BF16)
    for i in range(depth):
        j = i // n_mixers
        if i % n_mixers == 0:
            u = _conformer_glu(h, mix_norm_g[i], conv_w_pw1[j], conv_b_pw1[j])
            h = _conformer_conv_out(u, h, conv_w_dw[j], conv_b_dw[j], conv_ln_g[j], conv_ln_b[j],
                                    conv_w_pw2[j], conv_b_pw2[j], seq)
        else:
            h = _nsa_layer(h, mix_norm_g[i], nsa_w_in[j], nsa_q_norm[j], nsa_kc_norm[j], nsa_ks_norm[j],
                           nsa_kw_norm[j], nsa_pe_k[j], nsa_pe_v[j], nsa_ck_w1[j], nsa_ck_w2[j], nsa_cv_w1[j],
                           nsa_cv_w2[j], nsa_w_out[j], batch, seq)
        h = _conv_ffn(h, ffn_norm_g[i], w_up_all, ffn_w_dw[i], ffn_b_dw[i], w_down_all, i, seq)
    return h.reshape(batch, seq, d)
```

```python
import functools

import numpy as np
import jax
import jax.numpy as jnp
from jax import lax
from jax.experimental import pallas as pl
from jax.experimental.pallas import tpu as pltpu

N_HEADS = 16
HEAD_DIM = 64
N_KV = 4
GROUP = N_HEADS // N_KV
ROT_DIM = HEAD_DIM // 4
ROPE_THETA = 500000.0
CMP_LEN = 32
CMP_STRIDE = 16
SEL_LEN = 64
SEL_TOPK = 16
WINDOW = 512
EPS = 1e-6
NEG = -1e30

LANES = 128
SUBLANES = 8
BF16_ROWS = 16
V_TILE = 256
Q_TILE = 512
V_ROWS = HEAD_DIM + BF16_ROWS
GATE_ROWS = 16
VMEM_LIMIT = 48 * 1024 * 1024

F32 = jnp.float32
BF16 = jnp.bfloat16
NT_DIMS = (((1,), (1,)), ((), ()))


def _params(*sem):
    return pltpu.CompilerParams(dimension_semantics=sem, vmem_limit_bytes=VMEM_LIMIT)


def _dot(a, b):
    return jnp.dot(a, b, preferred_element_type=F32)


def _dot_nt(a, b):
    return lax.dot_general(a, b, NT_DIMS, preferred_element_type=F32)


def _split_bf16(x):
    hi = x.astype(BF16)
    lo = (x - hi.astype(F32)).astype(BF16)
    return hi, lo


def _rms_rows(x, g):
    ms = jnp.mean(x * x, axis=-1, keepdims=True)
    return x * lax.rsqrt(ms + EPS) * g


def _sigmoid(x):
    return 1.0 / (1.0 + jnp.exp(-x))


def _head_rms(xb, bd, gain):
    hi, lo = _split_bf16(xb * xb)
    ss = _dot(hi, bd) + _dot(lo, bd)
    return xb * lax.rsqrt(ss * (1.0 / HEAD_DIM) + EPS) * gain


def _rope(xb, c, s1, s2):
    half = ROT_DIM // 2
    return xb * c + pltpu.roll(xb, LANES - half, 1) * s1 + pltpu.roll(xb, half, 1) * s2


def _ffn_kernel(x_ref, halo_ref, g_ref, wup_ref, wdw_ref, bdw_ref, wd_ref, o_ref,
                hn_ref, a0_ref, v0_ref, a1_ref, v1_ref, *, tm, tiles_per_seq, nch):
    i = pl.program_id(0)
    hl = BF16_ROWS
    g = g_ref[...]
    hn_ref[hl:, :] = _rms_rows(x_ref[...], g).astype(BF16)
    first = (i % tiles_per_seq) == 0
    hn_ref[0:hl, :] = jnp.where(first, 0.0, _rms_rows(halo_ref[...], g)).astype(BF16)
    o_ref[...] = x_ref[...]

    tf = a0_ref.shape[1]

    def up(c, a_ref, v_ref):
        a_ref[...] = _dot(hn_ref[...], wup_ref[:, pl.ds(pl.multiple_of(c * tf, tf), tf)])
        v_ref[...] = _dot(hn_ref[hl:, :], wup_ref[:, pl.ds(pl.multiple_of((nch + c) * tf, tf), tf)])

    def down(c, a_ref, v_ref):
        w = wdw_ref[c]
        cv = (w[0:1] * a_ref[hl - 2:hl - 2 + tm, :] + w[1:2] * a_ref[hl - 1:hl - 1 + tm, :]
              + w[2:3] * a_ref[hl:hl + tm, :] + bdw_ref[c])
        act = (cv * _sigmoid(cv) * v_ref[...]).astype(BF16)
        o_ref[...] += _dot(act, wd_ref[c])

    buf0, buf1 = (a0_ref, v0_ref), (a1_ref, v1_ref)
    up(0, *buf0)

    def pair_body(p, carry):
        c = 2 * p
        up(c + 1, *buf1)
        down(c, *buf0)
        up(c + 2, *buf0)
        down(c + 1, *buf1)
        return carry

    lax.fori_loop(0, (nch - 1) // 2, pair_body, 0)
    if nch % 2 == 1:
        down(nch - 1, *buf0)
    else:
        up(nch - 1, *buf1)
        down(nch - 2, *buf0)
        down(nch - 1, *buf1)


def _conv_ffn(x, g, w_up_all, w_dw, b_dw, w_down_all, layer, seq):
    t, d = x.shape
    dff = w_down_all.shape[1]
    tm = min(512, seq)
    tf = 256
    assert seq % tm == 0 and dff % tf == 0 and w_dw.shape[0] == 3
    nch = dff // tf
    hl = BF16_ROWS
    kern = functools.partial(_ffn_kernel, tm=tm, tiles_per_seq=seq // tm, nch=nch)
    wup = w_up_all
    wdn = w_down_all.reshape(w_down_all.shape[0], nch, tf, d)
    wdw = w_dw.reshape(3, nch, tf).transpose(1, 0, 2)
    bdw = b_dw.reshape(nch, 1, tf)
    resident = lambda a: pl.BlockSpec(a.shape, lambda i: (0,) * a.ndim, pipeline_mode=pl.Buffered(1))
    of_layer = lambda a: pl.BlockSpec((None,) + a.shape[1:], lambda i: (layer,) + (0,) * (a.ndim - 1),
                                      pipeline_mode=pl.Buffered(1))
    return pl.pallas_call(
        kern,
        out_shape=jax.ShapeDtypeStruct((t, d), F32),
        grid=(t // tm,),
        in_specs=[
            pl.BlockSpec((tm, d), lambda i: (i, 0)),
            pl.BlockSpec((hl, d), lambda i: (jnp.maximum(i * (tm // hl) - 1, 0), 0)),
            pl.BlockSpec((1, d), lambda i: (0, 0)),
            of_layer(wup), resident(wdw), resident(bdw), of_layer(wdn),
        ],
        out_specs=pl.BlockSpec((tm, d), lambda i: (i, 0)),
        scratch_shapes=[
            pltpu.VMEM((tm + hl, d), BF16),
            pltpu.VMEM((tm + hl, tf), F32), pltpu.VMEM((tm, tf), F32),
            pltpu.VMEM((tm + hl, tf), F32), pltpu.VMEM((tm, tf), F32),
        ],
        compiler_params=_params("parallel"),
        name="conv_ffn",
    )(x, x, g.reshape(1, d), wup, wdw, bdw, wdn)


def _glu_kernel(x_ref, g_ref, wa_ref, wg_ref, ba_ref, bg_ref, o_ref, hn_ref):
    @pl.when(pl.program_id(1) == 0)
    def _():
        hn_ref[...] = _rms_rows(x_ref[...], g_ref[...]).astype(BF16)

    th = hn_ref.shape[0] // 2
    halves = [(_dot(hn_ref[h * th:(h + 1) * th, :], wa_ref[...]), _dot(hn_ref[h * th:(h + 1) * th, :], wg_ref[...]))
              for h in range(2)]
    for h, (a, gate) in enumerate(halves):
        o_ref[h * th:(h + 1) * th, :] = (a + ba_ref[...]) * _sigmoid(gate + bg_ref[...])


def _conformer_glu(x, g, w_pw1, b_pw1):
    t, d = x.shape
    tm = min(1024, t)
    tn = 256
    nj = d // tn
    w = w_pw1.astype(BF16)
    b = b_pw1.reshape(1, 2 * d)
    return pl.pallas_call(
        _glu_kernel,
        out_shape=jax.ShapeDtypeStruct((t, d), F32),
        grid=(t // tm, nj),
        in_specs=[
            pl.BlockSpec((tm, d), lambda i, j: (i, 0)),
            pl.BlockSpec((1, d), lambda i, j: (0, 0)),
            pl.BlockSpec((d, tn), lambda i, j: (0, j)),
            pl.BlockSpec((d, tn), lambda i, j: (0, j + nj)),
            pl.BlockSpec((1, tn), lambda i, j: (0, j)),
            pl.BlockSpec((1, tn), lambda i, j: (0, j + nj)),
        ],
        out_specs=pl.BlockSpec((tm, tn), lambda i, j: (i, j)),
        scratch_shapes=[pltpu.VMEM((tm, d), BF16)],
        compiler_params=_params("parallel", "arbitrary"),
        name="conformer_glu",
    )(x, g.reshape(1, d), w, w, b, b)


def _dwconv_kernel(u_ref, halo_ref, x_ref, wdw_ref, bdw_ref, lng_ref, lnb_ref, w2_ref, b2_ref, o_ref,
                   ext_ref, cv_ref, slab_ref, *, tm, halo, width, tiles_per_seq, row_chunk, col_chunk):
    i = pl.program_id(0)
    d = u_ref.shape[1]
    first = (i % tiles_per_seq) == 0
    ext_ref[0:halo, :] = jnp.where(first, 0.0, halo_ref[...])
    ext_ref[halo:, :] = u_ref[...]
    off = halo - (width - 1)
    for r0 in range(0, tm, row_chunk):
        for c0 in range(0, d, col_chunk):
            cs = slice(c0, c0 + col_chunk)
            acc = jnp.broadcast_to(bdw_ref[:, cs], (row_chunk, col_chunk))
            for r in range(min(SUBLANES, width)):
                taps = range(r, width, SUBLANES)
                lo = r0 + off + r
                rows = row_chunk + (len(taps) - 1) * SUBLANES
                slab_ref[r, 0:rows, :] = ext_ref[lo:lo + rows, cs]
                for q, k in enumerate(taps):
                    acc = acc + wdw_ref[k:k + 1, cs] * slab_ref[r, q * SUBLANES:q * SUBLANES + row_chunk, :]
            cv_ref[r0:r0 + row_chunk, cs] = acc
    u = cv_ref[...]
    mu = jnp.mean(u, axis=-1, keepdims=True)
    uc = u - mu
    var = jnp.mean(uc * uc, axis=-1, keepdims=True)
    y = uc * lax.rsqrt(var + EPS) * lng_ref[...] + lnb_ref[...]
    s = (y * _sigmoid(y)).astype(BF16)
    o_ref[...] = x_ref[...] + _dot(s, w2_ref[...]) + b2_ref[...]


def _conformer_conv_out(u, x, w_dw, b_dw, ln_g, ln_b, w_pw2, b_pw2, seq):
    t, d = x.shape
    width = w_dw.shape[0]
    halo = 32
    assert width - 1 <= halo
    tm = min(256, seq)
    assert seq % tm == 0 and tm % halo == 0
    row_chunk, col_chunk = 64, 256
    kern = functools.partial(_dwconv_kernel, tm=tm, halo=halo, width=width, tiles_per_seq=seq // tm,
                             row_chunk=row_chunk, col_chunk=col_chunk)
    slab_rows = row_chunk + (width - 1) // SUBLANES * SUBLANES
    vec = lambda: pl.BlockSpec((1, d), lambda i: (0, 0))
    return pl.pallas_call(
        kern,
        out_shape=jax.ShapeDtypeStruct((t, d), F32),
        grid=(t // tm,),
        in_specs=[
            pl.BlockSpec((tm, d), lambda i: (i, 0)),
            pl.BlockSpec((halo, d), lambda i: (jnp.maximum(i * (tm // halo) - 1, 0), 0)),
            pl.BlockSpec((tm, d), lambda i: (i, 0)),
            pl.BlockSpec((width, d), lambda i: (0, 0)),
            vec(), vec(), vec(),
            pl.BlockSpec((d, d), lambda i: (0, 0)),
            vec(),
        ],
        out_specs=pl.BlockSpec((tm, d), lambda i: (i, 0)),
        scratch_shapes=[pltpu.VMEM((tm + halo, d), F32), pltpu.VMEM((tm, d), F32),
                        pltpu.VMEM((SUBLANES, slab_rows, col_chunk), F32)],
        compiler_params=_params("parallel"),
        name="conformer_dwconv_out",
    )(u, u, x, w_dw, b_dw.reshape(1, d), ln_g.reshape(1, d), ln_b.reshape(1, d), w_pw2.astype(BF16),
      b_pw2.reshape(1, d))


def _in_proj_kernel(x_ref, g_ref, wn_ref, wt_ref, perm_ref, on_ref, oc_ref, oq_ref, ov_ref, *, n_qg, row_chunk):
    hn = _rms_rows(x_ref[...], g_ref[...]).astype(BF16)
    nk = on_ref.shape[1]
    on_ref[...] = _dot(hn, wn_ref[:, 0:nk])
    kcvc = _dot(hn, wn_ref[:, nk:]).astype(BF16)
    regrouped = _dot(perm_ref[...], kcvc).astype(BF16)
    per = regrouped.shape[0] // CMP_STRIDE
    for l in range(CMP_STRIDE):
        oc_ref[0, l] = regrouped[l * per:(l + 1) * per, :]
    for r0 in range(0, n_qg, row_chunk):
        r1 = min(r0 + row_chunk, n_qg)
        oq_ref[r0:r1, :] = _dot_nt(wt_ref[r0:r1, :], hn)
    vt = _dot_nt(wt_ref[n_qg:, :], hn).astype(BF16)
    ones = jnp.ones((BF16_ROWS, V_TILE), BF16)
    for jj in range(ov_ref.shape[1]):
        for g in range(N_KV):
            for half in range(2):
                r0 = g * LANES + half * HEAD_DIM
                ov_ref[0, jj, g, half * V_ROWS:half * V_ROWS + HEAD_DIM, :] = (
                    vt[r0:r0 + HEAD_DIM, jj * V_TILE:(jj + 1) * V_TILE])
                ov_ref[0, jj, g, half * V_ROWS + HEAD_DIM:(half + 1) * V_ROWS, :] = ones


def _in_proj(x, g, wn, wt, batch, seq):
    t, d = x.shape
    tm = min(512, seq)
    n_qg = N_HEADS * HEAD_DIM + N_KV * GATE_ROWS
    assert seq % tm == 0 and tm % V_TILE == 0 and wt.shape[0] == n_qg + N_KV * LANES
    per_seq = seq // tm
    nk = N_KV * LANES
    ncv = wn.shape[1] - nk
    per = tm // CMP_STRIDE
    rows = np.arange(tm)
    perm = np.zeros((tm, tm), np.float32)
    perm[(rows % CMP_STRIDE) * per + rows // CMP_STRIDE, rows] = 1.0
    kern = functools.partial(_in_proj_kernel, n_qg=n_qg, row_chunk=512)
    return pl.pallas_call(
        kern,
        out_shape=(jax.ShapeDtypeStruct((t, nk), F32),
                   jax.ShapeDtypeStruct((t // tm, CMP_STRIDE, per, ncv), BF16),
                   jax.ShapeDtypeStruct((n_qg, t), F32),
                   jax.ShapeDtypeStruct((batch, seq // V_TILE, N_KV, 2 * V_ROWS, V_TILE), BF16)),
        grid=(t // tm,),
        in_specs=[
            pl.BlockSpec((tm, d), lambda i: (i, 0)),
            pl.BlockSpec((1, d), lambda i: (0, 0)),
            pl.BlockSpec(wn.shape, lambda i: (0, 0)),
            pl.BlockSpec(wt.shape, lambda i: (0, 0)),
            pl.BlockSpec((tm, tm), lambda i: (0, 0)),
        ],
        out_specs=(pl.BlockSpec((tm, nk), lambda i: (i, 0)),
                   pl.BlockSpec((1, CMP_STRIDE, per, ncv), lambda i: (i, 0, 0, 0)),
                   pl.BlockSpec((n_qg, tm), lambda i: (0, i)),
                   pl.BlockSpec((1, tm // V_TILE, N_KV, 2 * V_ROWS, V_TILE),
                                lambda i: (i // per_seq, i % per_seq, 0, 0, 0))),
        compiler_params=_params("parallel"),
        name="nsa_in_proj",
    )(x, g.reshape(1, d), wn, wt, jnp.asarray(perm, BF16))


def _out_proj_kernel(a_ref, x_ref, w_ref, o_ref):
    o_ref[...] = x_ref[...] + _dot(a_ref[...], w_ref[...])


def _out_proj(a, x, w):
    t, d = x.shape
    tm = min(512, t)
    return pl.pallas_call(
        _out_proj_kernel,
        out_shape=jax.ShapeDtypeStruct((t, d), F32),
        grid=(t // tm,),
        in_specs=[
            pl.BlockSpec((tm, a.shape[1]), lambda i: (i, 0)),
            pl.BlockSpec((tm, d), lambda i: (i, 0)),
            pl.BlockSpec(w.shape, lambda i: (0, 0)),
        ],
        out_specs=pl.BlockSpec((tm, d), lambda i: (i, 0)),
        compiler_params=_params("parallel"),
        name="nsa_out_proj",
    )(a, x, w.astype(BF16))


def _kprep_kernel(k_ref, gain_ref, bd_ref, c_ref, s1_ref, s2_ref, kaug_ref, *, ts, seq):
    i = pl.program_id(0)
    bd = bd_ref[...]
    gain = gain_ref[...]
    c, s1, s2 = c_ref[...], s1_ref[...], s2_ref[...]
    tpos = (i * ts) % seq + lax.broadcasted_iota(jnp.int32, (ts, LANES), 0)
    lane = lax.broadcasted_iota(jnp.int32, (ts, LANES), 1)
    onehot = jnp.where(tpos // SEL_LEN == lane, 1.0, 0.0).astype(BF16)
    for g in range(N_KV):
        xb = k_ref[:, g * LANES:(g + 1) * LANES]
        xr = _rope(_head_rms(xb, bd, gain), c, s1, s2)
        kaug_ref[:, 2 * g * LANES:(2 * g + 1) * LANES] = xr.astype(BF16)
        kaug_ref[:, (2 * g + 1) * LANES:(2 * g + 2) * LANES] = onehot


def _kprep(proj, gain, bd, tabs, seq):
    t = proj.shape[0]
    ts = min(512, seq)
    assert seq % ts == 0
    nk = N_KV * LANES
    kern = functools.partial(_kprep_kernel, ts=ts, seq=seq)
    tab = lambda: pl.BlockSpec((ts, LANES), lambda i: (i % (seq // ts), 0))
    return pl.pallas_call(
        kern,
        out_shape=jax.ShapeDtypeStruct((t, 2 * nk), BF16),
        grid=(t // ts,),
        in_specs=[
            pl.BlockSpec((ts, nk), lambda i: (i, 0)),
            pl.BlockSpec((1, LANES), lambda i: (0, 0)),
            pl.BlockSpec((LANES, LANES), lambda i: (0, 0)),
            tab(), tab(), tab(),
        ],
        out_specs=pl.BlockSpec((ts, 2 * nk), lambda i: (i, 0)),
        compiler_params=_params("parallel"),
        name="nsa_key_prep",
    )(proj, gain, bd, *tabs)


def _compress_kernel(x_ref, w1k_ref, w1v_ref, ck_ref, cv_ref, w2k_ref, w2v_ref, gain_ref, o_ref, ot_ref,
                     acck_ref, accv_ref):
    l = pl.program_id(1)
    nc = acck_ref.shape[0]
    hid = w2k_ref.shape[0]
    hk = w1k_ref.shape[1]

    @pl.when(l == 0)
    def _():
        acck_ref[...] = jnp.zeros_like(acck_ref)
        accv_ref[...] = jnp.zeros_like(accv_ref)

    x = x_ref[0].reshape(nc, 2 * hk)
    acck_ref[...] += _dot(x[:, 0:hk], w1k_ref[0])
    accv_ref[...] += _dot(x[:, hk:], w1v_ref[0])

    @pl.when(l == pl.num_programs(1) - 1)
    def _():
        for g in range(N_KV):
            def hidden(acc_ref, c_ref):
                first = acc_ref[:, g * hid:(g + 1) * hid] + c_ref[0:1, :]
                second = acc_ref[:, (N_KV + g) * hid:(N_KV + g + 1) * hid] + c_ref[1:2, :]
                pre = first + pltpu.roll(second, nc - 1, 0)
                return (pre * _sigmoid(pre)).astype(BF16)

            kv = _dot(hidden(acck_ref, ck_ref), w2k_ref[...]) + _dot(hidden(accv_ref, cv_ref), w2v_ref[...])
            is_k = lax.broadcasted_iota(jnp.int32, kv.shape, 1) < HEAD_DIM
            ss = jnp.sum(jnp.where(is_k, kv * kv, 0.0), axis=-1, keepdims=True)
            kn = kv * lax.rsqrt(ss * (1.0 / HEAD_DIM) + EPS) * gain_ref[...]
            out = jnp.where(is_k, kn, kv)
            o_ref[g] = out.astype(BF16)
            ot_ref[g] = out.T.astype(BF16)


def _compress(kcvc, pe_k, pe_v, ck_w1, ck_w2, cv_w1, cv_w2, kc_g, batch, seq):
    nc = seq // CMP_STRIDE
    hid = ck_w1.shape[1]
    hk = N_KV * HEAD_DIM
    tiles, _, per, width = kcvc.shape
    assert CMP_LEN == 2 * CMP_STRIDE and width == 2 * hk and tiles * per == batch * nc
    x5 = kcvc.reshape(batch, tiles // batch, CMP_STRIDE, per, width)
    pad = lambda w, left: jnp.pad(w, ((0, 0), (HEAD_DIM, 0) if left else (0, HEAD_DIM))).astype(BF16)
    gain = jnp.concatenate([kc_g, jnp.ones((HEAD_DIM,), F32)]).reshape(1, LANES)

    def first_layer(pe, w1):
        w = w1.reshape(2, CMP_STRIDE, HEAD_DIM, hid)
        rows = lambda a, k: jnp.pad(a, ((0, 0), (k * HEAD_DIM, (N_KV - 1 - k) * HEAD_DIM), (0, 0)))
        bd = jnp.concatenate([rows(w[s], k) for s in range(2) for k in range(N_KV)], axis=2)
        const = jnp.einsum("sld,sldh->sh", pe.reshape(2, CMP_STRIDE, HEAD_DIM), w, precision=lax.Precision.HIGHEST)
        return bd.astype(BF16), const

    w1k, const_k = first_layer(pe_k, ck_w1)
    w1v, const_v = first_layer(pe_v, cv_w1)
    wide = 2 * N_KV * hid
    step_w = lambda: pl.BlockSpec((1, hk, wide), lambda b, l: (l, 0, 0))
    full = lambda shape: pl.BlockSpec(shape, lambda b, l: (0,) * len(shape))
    return pl.pallas_call(
        _compress_kernel,
        out_shape=(jax.ShapeDtypeStruct((batch * N_KV, nc, LANES), BF16),
                   jax.ShapeDtypeStruct((batch * N_KV, LANES, nc), BF16)),
        grid=(batch, CMP_STRIDE),
        in_specs=[pl.BlockSpec((1, tiles // batch, None, per, width), lambda b, l: (b, 0, l, 0, 0)),
                  step_w(), step_w(), full((2, hid)), full((2, hid)),
                  full((hid, LANES)), full((hid, LANES)), full((1, LANES))],
        out_specs=(pl.BlockSpec((N_KV, nc, LANES), lambda b, l: (b, 0, 0)),
                   pl.BlockSpec((N_KV, LANES, nc), lambda b, l: (b, 0, 0))),
        scratch_shapes=[pltpu.VMEM((nc, wide), F32), pltpu.VMEM((nc, wide), F32)],
        compiler_params=_params("parallel", "arbitrary"),
        name="nsa_compress",
    )(x5, w1k, w1v, const_k, const_v, pad(ck_w2, False), pad(cv_w2, True), gain)


def _attn_kernel(qt_ref, gate_ref, kaug_ref, vt_ref, kvc_ref, kvct_ref, cmapt_ref, qg_ref, cos_ref, sin_ref,
                 o_ref,
                 qaug_ref, qwin_ref, oc_ref, ms_ref, ls_ref, accs_ref, mw_ref, lw_ref, accw_ref,
                 sa_ref, sb_ref, wdiag_ref, wold_ref, imp_ref,
                 *, tq, tk, k_top, n_cmp_var):
    qi = pl.program_id(2)
    t0 = qi * tq
    cols = GROUP * tq
    half = ROT_DIM // 2

    gain = qg_ref[...]
    cos, sin = cos_ref[...], sin_ref[...]
    nope, rope = [], []
    for r in range(GROUP):
        x = qt_ref[r * HEAD_DIM:(r + 1) * HEAD_DIM, :]
        ss = jnp.sum(x * x, axis=0, keepdims=True)
        xn = x * lax.rsqrt(ss * (1.0 / HEAD_DIM) + EPS) * gain
        x1, x2 = xn[0:half], xn[half:ROT_DIM]
        nope.append(xn)
        rope.append(jnp.concatenate([x1 * cos - x2 * sin, x2 * cos + x1 * sin, xn[ROT_DIM:]], axis=0))
    zeros = jnp.zeros((HEAD_DIM, cols), BF16)
    q_nope = jnp.concatenate([jnp.concatenate(nope, axis=1).astype(BF16), zeros], axis=0)
    q_rope = jnp.concatenate(rope, axis=1).astype(BF16)
    qwin_ref[...] = jnp.concatenate([zeros, q_rope], axis=0)

    tpos = t0 + lax.broadcasted_iota(jnp.int32, (1, tq), 1)

    def per_head(x):
        return jnp.concatenate([x] * GROUP, axis=1)

    def fold_tile(s, keep, k0, v_rows, m_ref, l_ref, acc_ref):
        if keep is not None:
            s = jnp.where(per_head(keep), s, NEG)
        m_prev = m_ref[...]
        m_new = jnp.maximum(m_prev, jnp.max(s, axis=0, keepdims=True))
        p = jnp.exp2(s - m_new)
        alpha = jnp.exp2(m_prev - m_new)
        pb = p.astype(BF16)
        v_tile0 = k0 // V_TILE
        pv = _dot(vt_ref[0, v_tile0, 0, v_rows, :], pb[0:V_TILE])
        for c in range(1, s.shape[0] // V_TILE):
            pv = pv + _dot(vt_ref[0, v_tile0 + c, 0, v_rows, :], pb[c * V_TILE:(c + 1) * V_TILE])
        l_ref[...] = alpha * l_ref[...] + pv[HEAD_DIM:HEAD_DIM + 1]
        acc_ref[...] = alpha * acc_ref[...] + pv[0:HEAD_DIM]
        m_ref[...] = m_new

    def back(k0, size):
        return tpos - (k0 + lax.broadcasted_iota(jnp.int32, (size, 1), 0))

    def sel_scores(k0):
        return _dot(kaug_ref[pl.ds(k0, tk), :], qaug_ref[...])

    def win_scores(k0, size):
        return _dot(kaug_ref[pl.ds(k0, size), 0:LANES], qwin_ref[...])

    for m_ref, l_ref, acc_ref in ((ms_ref, ls_ref, accs_ref), (mw_ref, lw_ref, accw_ref)):
        m_ref[...] = jnp.full_like(m_ref, NEG)
        l_ref[...] = jnp.zeros_like(l_ref)
        acc_ref[...] = jnp.zeros_like(acc_ref)

    t0a = pl.multiple_of(t0, tq)
    k_old = pl.multiple_of(jnp.maximum(t0 - WINDOW, 0), tq)
    wdiag_ref[...] = win_scores(t0a, tq)
    wold_ref[...] = win_scores(k_old, WINDOW)
    qaug_ref[0:LANES, :] = jnp.concatenate([q_rope, zeros], axis=0)
    sa_ref[...] = _dot(kaug_ref[0:tk, 0:LANES], qaug_ref[0:LANES, :])
    win_state = (slice(V_ROWS, 2 * V_ROWS), mw_ref, lw_ref, accw_ref)
    fold_tile(wdiag_ref[...], back(t0a, tq) >= 0, t0a, *win_state)
    b_old = back(k_old, WINDOW)
    fold_tile(wold_ref[...], (b_old < WINDOW) & (b_old > tpos - t0), k_old, *win_state)

    nc = kvc_ref.shape[1]

    def cmp_branch(rows):
        kvc = kvc_ref[0, 0:rows, :]
        cmp_end = lax.broadcasted_iota(jnp.int32, (rows, 1), 0) * CMP_STRIDE + (CMP_LEN - 1)
        s_c = jnp.where(per_head(cmp_end <= tpos), _dot(kvc, q_nope), NEG)
        e = jnp.exp2(s_c - jnp.maximum(jnp.max(s_c, axis=0, keepdims=True), 0.1 * NEG))
        p_c = e * (1.0 / jnp.maximum(jnp.sum(e, axis=0, keepdims=True), 1e-30))
        oc_ref[...] = _dot(kvct_ref[0, HEAD_DIM:, 0:rows], p_c.astype(BF16))
        p_sum = p_c[:, 0:tq]
        for r in range(1, GROUP):
            p_sum = p_sum + p_c[:, r * tq:(r + 1) * tq]
        hi, lo = _split_bf16(p_sum)
        imp_ref[...] = _dot(cmapt_ref[:, 0:rows], hi) + _dot(cmapt_ref[:, 0:rows], lo)

    variant = (t0 + tq - 1) // (CMP_STRIDE * (nc // n_cmp_var))
    for v in range(n_cmp_var):
        pl.when(variant == v)(functools.partial(cmp_branch, (v + 1) * (nc // n_cmp_var)))
    imp = imp_ref[...]

    blk = lax.broadcasted_iota(jnp.int32, (LANES, tq), 0)
    cur = tpos // SEL_LEN
    forced = (blk == 0) | (blk == cur) | (blk == cur - 1)
    taken = -3e38
    imp = jnp.where(blk > cur, -1.0, jnp.where(forced, taken, imp))
    blk_f = blk.astype(F32)
    for _ in range(max(k_top - 3, 0)):
        mx = jnp.max(imp, axis=0, keepdims=True)
        idx = jnp.min(jnp.where(imp == mx, blk_f, float(LANES)), axis=0, keepdims=True)
        imp = jnp.where(blk_f == idx, taken, imp)
    bias = jnp.where(imp == taken, 0.0, NEG)
    qaug_ref[LANES:, :] = per_head(bias.astype(BF16))
    blocks0 = tk // SEL_LEN
    bias0 = jnp.broadcast_to(bias[0:blocks0, None, :], (blocks0, SEL_LEN, tq)).reshape(tk, tq)
    sa_ref[...] = sa_ref[...] + per_head(bias0)


    n = t0 // tk
    sel_state = (slice(0, V_ROWS), ms_ref, ls_ref, accs_ref)

    def tile_pair(k0):
        sb_ref[...] = sel_scores(k0 + tk)
        fold_tile(sa_ref[...], None, k0, *sel_state)
        sa_ref[...] = sel_scores(k0 + 2 * tk)
        fold_tile(sb_ref[...], None, k0 + tk, *sel_state)

    def quad_body(i, carry):
        k0 = pl.multiple_of(4 * i * tk, tk)
        tile_pair(k0)
        tile_pair(k0 + 2 * tk)
        return carry

    n_pairs = n // 2
    lax.fori_loop(0, n_pairs // 2, quad_body, 0)

    @pl.when(n_pairs % 2 == 1)
    def _():
        tile_pair(pl.multiple_of((n_pairs - 1) * 2 * tk, tk))

    k_diag = pl.multiple_of(n * tk, tk)

    def diagonal_fold(buf_ref):
        if tk == tq:
            fold_tile(buf_ref[...], back(k_diag, tk) >= 0, k_diag, *sel_state)
            return
        pl.when(t0 % tk == 0)(lambda: fold_tile(buf_ref[0:tq, :], back(k_diag, tq) >= 0, k_diag, *sel_state))
        pl.when(t0 % tk != 0)(lambda: fold_tile(buf_ref[...], back(k_diag, tk) >= 0, k_diag, *sel_state))

    @pl.when(n % 2 == 1)
    def _():
        sb_ref[...] = sel_scores(k_diag)
        fold_tile(sa_ref[...], None, k_diag - tk, *sel_state)
        diagonal_fold(sb_ref)

    @pl.when(n % 2 == 0)
    def _():
        diagonal_fold(sa_ref)

    o_s = accs_ref[...] / ls_ref[...]
    o_w = accw_ref[...] / lw_ref[...]
    o_c = oc_ref[...]
    sig = _sigmoid(gate_ref[...])
    heads = []
    for r in range(GROUP):
        cs = slice(r * tq, (r + 1) * tq)
        g_c, g_s, g_w = (sig[3 * r + b:3 * r + b + 1, :] for b in range(3))
        heads.append(g_c * o_c[:, cs] + g_s * o_s[:, cs] + g_w * o_w[:, cs])
    o_ref[...] = jnp.concatenate(heads, axis=0).T.astype(o_ref.dtype)


def _attention(qgt, kaug, vt, kvc, kvct, cmapt, qg, cos, sin, batch, seq):
    t = qgt.shape[1]
    tq = min(Q_TILE, seq)
    tk = min(512, seq)
    nq = seq // tq
    nc = kvc.shape[1]
    assert seq % tk == 0 and tk % tq == 0 and WINDOW % tq == 0 and tq % V_TILE == 0 and seq >= WINDOW
    cols = GROUP * tq
    hq = GROUP * HEAD_DIM
    k_top = min(SEL_TOPK, seq // SEL_LEN)
    n_cmp_var = max(1, min(4, nc // LANES))
    assert k_top >= 3 and nc % n_cmp_var == 0
    kern = functools.partial(_attn_kernel, tq=tq, tk=tk, k_top=k_top, n_cmp_var=n_cmp_var)
    tab = lambda: pl.BlockSpec((ROT_DIM // 2, tq), lambda b, g, qi: (0, qi))
    stat = lambda: pltpu.VMEM((1, cols), F32)
    acc = lambda: pltpu.VMEM((HEAD_DIM, cols), F32)
    return pl.pallas_call(
        kern,
        out_shape=jax.ShapeDtypeStruct((t, N_HEADS * HEAD_DIM), BF16),
        grid=(batch, N_KV, nq),
        in_specs=[
            pl.BlockSpec((hq, tq), lambda b, g, qi: (g, b * nq + qi)),
            pl.BlockSpec((GATE_ROWS, tq), lambda b, g, qi: (N_HEADS * HEAD_DIM // GATE_ROWS + g, b * nq + qi)),
            pl.BlockSpec((seq, 2 * LANES), lambda b, g, qi: (b, g)),
            pl.BlockSpec((1, seq // V_TILE, 1, 2 * V_ROWS, V_TILE), lambda b, g, qi: (b, 0, g, 0, 0)),
            pl.BlockSpec((1, nc, LANES), lambda b, g, qi: (b * N_KV + g, 0, 0)),
            pl.BlockSpec((1, LANES, nc), lambda b, g, qi: (b * N_KV + g, 0, 0)),
            pl.BlockSpec((LANES, nc), lambda b, g, qi: (0, 0)),
            pl.BlockSpec((HEAD_DIM, tq), lambda b, g, qi: (0, 0)),
            tab(), tab(),
        ],
        out_specs=pl.BlockSpec((tq, hq), lambda b, g, qi: (b * nq + qi, g)),
        scratch_shapes=[
            pltpu.VMEM((2 * LANES, cols), BF16), pltpu.VMEM((LANES, cols), BF16), acc(),
            stat(), stat(), acc(), stat(), stat(), acc(),
            pltpu.VMEM((tk, cols), F32), pltpu.VMEM((tk, cols), F32), pltpu.VMEM((tq, cols), F32), pltpu.VMEM((WINDOW, cols), F32),
            pltpu.VMEM((LANES, tq), F32),
        ],
        compiler_params=_params("parallel", "parallel", "arbitrary"),
        name="nsa_attention",
    )(qgt, qgt, kaug, vt, kvc, kvct, cmapt, qg, cos, sin)


def _rope_angles(seq):
    half = ROT_DIM // 2
    inv_freq = ROPE_THETA ** (-jnp.arange(half, dtype=F32) * (2.0 / ROT_DIM))
    ang = jnp.arange(seq, dtype=F32)[:, None] * inv_freq[None, :]
    return jnp.cos(ang), jnp.sin(ang)


def _rope_tables(cos, sin):
    seq, half = cos.shape
    zeros = jnp.zeros((seq, HEAD_DIM - ROT_DIM), F32)
    zh = jnp.zeros((seq, half), F32)
    c = jnp.concatenate([cos, cos, zeros + 1.0], axis=1)
    s1 = jnp.concatenate([-sin, zh, zeros], axis=1)
    s2 = jnp.concatenate([zh, sin, zeros], axis=1)
    reps = LANES // HEAD_DIM
    return tuple(jnp.tile(a, (1, reps)) for a in (c, s1, s2))


def _cmp_to_sel_t(nc, n_sel):
    start_c = np.arange(nc)[None, :] * CMP_STRIDE
    start_s = np.arange(LANES)[:, None] * SEL_LEN
    ov = np.minimum(start_c + CMP_LEN, start_s + SEL_LEN) - np.maximum(start_c, start_s)
    m = np.maximum(ov, 0).astype(np.float32) / CMP_LEN
    m[n_sel:, :] = 0.0
    m[:, nc - 1:] = 0.0
    return jnp.asarray(m, BF16)


def _nsa_in_weights(w_in):
    d = w_in.shape[0]
    hq, hk = N_HEADS * HEAD_DIM, N_KV * HEAD_DIM
    offs = np.cumsum([0, hq] + [hk] * 6)
    q, kc, vc, ks, vs, kw, vw = (w_in[:, offs[n]:offs[n + 1]] for n in range(7))
    gl = w_in[:, offs[7]:]
    per_group = lambda a, b: jnp.stack([a.reshape(d, N_KV, HEAD_DIM), b.reshape(d, N_KV, HEAD_DIM)],
                                       axis=2).reshape(d, N_KV * LANES)
    gates = jnp.pad(gl.reshape(d, N_KV, 3 * GROUP), ((0, 0), (0, 0), (0, GATE_ROWS - 3 * GROUP)))
    wn = jnp.concatenate([per_group(ks, kw), kc, vc], axis=1).astype(BF16)
    wt = jnp.concatenate([q, gates.reshape(d, N_KV * GATE_ROWS), per_group(vs, vw)], axis=1).T.astype(BF16)
    return wn, wt


def _nsa_layer(x, g, w_in, q_g, kc_g, ks_g, kw_g, pe_k, pe_v, ck_w1, ck_w2, cv_w1, cv_w2, w_out, batch, seq):
    nc = seq // CMP_STRIDE
    n_sel = seq // SEL_LEN
    assert n_sel <= LANES and seq % CMP_STRIDE == 0 and 3 * GROUP <= GATE_ROWS
    wn, wt = _nsa_in_weights(w_in)
    proj, kcvc, qgt, vt = _in_proj(x, g, wn, wt, batch, seq)
    cos, sin = _rope_angles(seq)
    seg = np.arange(LANES) // HEAD_DIM
    bd = jnp.asarray(seg[:, None] == seg[None, :], BF16)
    kaug = _kprep(proj, jnp.concatenate([ks_g, kw_g]).reshape(1, LANES), bd, _rope_tables(cos, sin), seq)
    kvc, kvct = _compress(kcvc, pe_k, pe_v, ck_w1, ck_w2, cv_w1, cv_w2, kc_g, batch, seq)
    tq = min(Q_TILE, seq)
    qg = jnp.broadcast_to((q_g * float(HEAD_DIM ** -0.5 * np.log2(np.e)))[:, None], (HEAD_DIM, tq))
    attn = _attention(qgt, kaug, vt, kvc, kvct, _cmp_to_sel_t(nc, n_sel), qg, cos.T, sin.T, batch, seq)
    return _out_proj(attn, x, w_out)


def kernel(x, mix_norm_g, ffn_norm_g, conv_w_pw1, conv_b_pw1, conv_w_dw, conv_b_dw, conv_ln_g, conv_ln_b, conv_w_pw2, conv_b_pw2, nsa_w_in, nsa_q_norm, nsa_kc_norm, nsa_ks_norm, nsa_kw_norm, nsa_pe_k, nsa_pe_v, nsa_ck_w1, nsa_ck_w2, nsa_cv_w1, nsa_cv_w2, nsa_w_out, ffn_w_up, ffn_w_dw, ffn_b_dw, ffn_w_down):
    batch, seq, d = x.shape
    depth = mix_norm_g.shape[0]
    n_mixers = 2
    h = x.reshape(batch * seq, d)
    w_up_all, w_down_all = ffn_w_up.astype(BF16), ffn_w_down.astype(BF16)
    for i in range(depth):
        j = i // n_mixers
        if i % n_mixers == 0:
            u = _conformer_glu(h, mix_norm_g[i], conv_w_pw1[j], conv_b_pw1[j])
            h = _conformer_conv_out(u, h, conv_w_dw[j], conv_b_dw[j], conv_ln_g[j], conv_ln_b[j],
                                    conv_w_pw2[j], conv_b_pw2[j], seq)
        else:
            h = _nsa_layer(h, mix_norm_g[i], nsa_w_in[j], nsa_q_norm[j], nsa_kc_norm[j], nsa_ks_norm[j],
                           nsa_kw_norm[j], nsa_pe_k[j], nsa_pe_v[j], nsa_ck_w1[j], nsa_ck_w2[j], nsa_cv_w1[j],
                           nsa_cv_w2[j], nsa_w_out[j], batch, seq)
        h = _conv_ffn(h, ffn_norm_g[i], w_up_all, ffn_w_dw[i], ffn_b_dw[i], w_down_all, i, seq)
    return h.reshape(batch, seq, d)
```

```python
import functools

import numpy as np
import jax
import jax.numpy as jnp
from jax import lax
from jax.experimental import pallas as pl
from jax.experimental.pallas import tpu as pltpu

N_HEADS = 16
HEAD_DIM = 64
N_KV = 4
GROUP = N_HEADS // N_KV
ROT_DIM = HEAD_DIM // 4
ROPE_THETA = 500000.0
CMP_LEN = 32
CMP_STRIDE = 16
SEL_LEN = 64
SEL_TOPK = 16
WINDOW = 512
EPS = 1e-6
NEG = -1e30

LANES = 128
SUBLANES = 8
BF16_ROWS = 16
V_TILE = 256
Q_TILE = 512
V_ROWS = HEAD_DIM + BF16_ROWS
GATE_ROWS = 16
VMEM_LIMIT = 48 * 1024 * 1024

F32 = jnp.float32
BF16 = jnp.bfloat16
NT_DIMS = (((1,), (1,)), ((), ()))


def _params(*sem):
    return pltpu.CompilerParams(dimension_semantics=sem, vmem_limit_bytes=VMEM_LIMIT)


def _dot(a, b):
    return jnp.dot(a, b, preferred_element_type=F32)


def _dot_nt(a, b):
    return lax.dot_general(a, b, NT_DIMS, preferred_element_type=F32)


def _split_bf16(x):
    hi = x.astype(BF16)
    lo = (x - hi.astype(F32)).astype(BF16)
    return hi, lo


def _rms_rows(x, g):
    ms = jnp.mean(x * x, axis=-1, keepdims=True)
    return x * lax.rsqrt(ms + EPS) * g


def _sigmoid(x):
    return 1.0 / (1.0 + jnp.exp(-x))


def _head_rms(xb, bd, gain):
    hi, lo = _split_bf16(xb * xb)
    ss = _dot(hi, bd) + _dot(lo, bd)
    return xb * lax.rsqrt(ss * (1.0 / HEAD_DIM) + EPS) * gain


def _rope(xb, c, s1, s2):
    half = ROT_DIM // 2
    return xb * c + pltpu.roll(xb, LANES - half, 1) * s1 + pltpu.roll(xb, half, 1) * s2


def _ffn_kernel(x_ref, halo_ref, g_ref, wup_ref, wdw_ref, bdw_ref, wd_ref, o_ref,
                hn_ref, a0_ref, v0_ref, a1_ref, v1_ref, *, tm, tiles_per_seq, nch):
    i = pl.program_id(0)
    hl = BF16_ROWS
    g = g_ref[...]
    hn_ref[hl:, :] = _rms_rows(x_ref[...], g).astype(BF16)
    first = (i % tiles_per_seq) == 0
    hn_ref[0:hl, :] = jnp.where(first, 0.0, _rms_rows(halo_ref[...], g)).astype(BF16)
    o_ref[...] = x_ref[...]

    tf = a0_ref.shape[1]

    def cols(c):
        start = c * tf
        return pl.ds(start if isinstance(start, int) else pl.multiple_of(start, tf), tf)

    def up(c, a_ref, v_ref):
        a_ref[...] = _dot(hn_ref[...], wup_ref[:, cols(c)])
        v_ref[...] = _dot(hn_ref[hl:, :], wup_ref[:, cols(nch + c)])

    def down(c, a_ref, v_ref):
        w = wdw_ref[c]
        cv = (w[0:1] * a_ref[hl - 2:hl - 2 + tm, :] + w[1:2] * a_ref[hl - 1:hl - 1 + tm, :]
              + w[2:3] * a_ref[hl:hl + tm, :] + bdw_ref[c])
        act = (cv * _sigmoid(cv) * v_ref[...]).astype(BF16)
        o_ref[...] += _dot(act, wd_ref[c])

    buf0, buf1 = (a0_ref, v0_ref), (a1_ref, v1_ref)
    up(0, *buf0)

    def chunk_pair(c):
        up(c + 1, *buf1)
        down(c, *buf0)
        up(c + 2, *buf0)
        down(c + 1, *buf1)

    def quad_body(q, carry):
        chunk_pair(4 * q)
        chunk_pair(4 * q + 2)
        return carry

    n_pairs = (nch - 1) // 2
    lax.fori_loop(0, n_pairs // 2, quad_body, 0)
    if n_pairs % 2 == 1:
        chunk_pair(2 * (n_pairs - 1))
    if nch % 2 == 1:
        down(nch - 1, *buf0)
    else:
        up(nch - 1, *buf1)
        down(nch - 2, *buf0)
        down(nch - 1, *buf1)


def _conv_ffn(x, g, w_up_all, w_dw, b_dw, w_down_all, layer, seq):
    t, d = x.shape
    dff = w_down_all.shape[1]
    tm = min(512, seq)
    tf = 256
    assert seq % tm == 0 and dff % tf == 0 and w_dw.shape[0] == 3
    nch = dff // tf
    hl = BF16_ROWS
    kern = functools.partial(_ffn_kernel, tm=tm, tiles_per_seq=seq // tm, nch=nch)
    wup = w_up_all
    wdn = w_down_all.reshape(w_down_all.shape[0], nch, tf, d)
    wdw = w_dw.reshape(3, nch, tf).transpose(1, 0, 2)
    bdw = b_dw.reshape(nch, 1, tf)
    resident = lambda a: pl.BlockSpec(a.shape, lambda i: (0,) * a.ndim, pipeline_mode=pl.Buffered(1))
    of_layer = lambda a: pl.BlockSpec((None,) + a.shape[1:], lambda i: (layer,) + (0,) * (a.ndim - 1),
                                      pipeline_mode=pl.Buffered(1))
    return pl.pallas_call(
        kern,
        out_shape=jax.ShapeDtypeStruct((t, d), F32),
        grid=(t // tm,),
        in_specs=[
            pl.BlockSpec((tm, d), lambda i: (i, 0)),
            pl.BlockSpec((hl, d), lambda i: (jnp.maximum(i * (tm // hl) - 1, 0), 0)),
            pl.BlockSpec((1, d), lambda i: (0, 0)),
            of_layer(wup), resident(wdw), resident(bdw), of_layer(wdn),
        ],
        out_specs=pl.BlockSpec((tm, d), lambda i: (i, 0)),
        scratch_shapes=[
            pltpu.VMEM((tm + hl, d), BF16),
            pltpu.VMEM((tm + hl, tf), F32), pltpu.VMEM((tm, tf), F32),
            pltpu.VMEM((tm + hl, tf), F32), pltpu.VMEM((tm, tf), F32),
        ],
        compiler_params=_params("parallel"),
        name="conv_ffn",
    )(x, x, g.reshape(1, d), wup, wdw, bdw, wdn)


def _glu_kernel(x_ref, g_ref, wa_ref, wg_ref, ba_ref, bg_ref, o_ref, hn_ref):
    @pl.when(pl.program_id(1) == 0)
    def _():
        hn_ref[...] = _rms_rows(x_ref[...], g_ref[...]).astype(BF16)

    th = hn_ref.shape[0] // 2
    halves = [(_dot(hn_ref[h * th:(h + 1) * th, :], wa_ref[...]), _dot(hn_ref[h * th:(h + 1) * th, :], wg_ref[...]))
              for h in range(2)]
    for h, (a, gate) in enumerate(halves):
        o_ref[h * th:(h + 1) * th, :] = (a + ba_ref[...]) * _sigmoid(gate + bg_ref[...])


def _conformer_glu(x, g, w_pw1, b_pw1):
    t, d = x.shape
    tm = min(1024, t)
    tn = 256
    nj = d // tn
    w = w_pw1.astype(BF16)
    b = b_pw1.reshape(1, 2 * d)
    return pl.pallas_call(
        _glu_kernel,
        out_shape=jax.ShapeDtypeStruct((t, d), F32),
        grid=(t // tm, nj),
        in_specs=[
            pl.BlockSpec((tm, d), lambda i, j: (i, 0)),
            pl.BlockSpec((1, d), lambda i, j: (0, 0)),
            pl.BlockSpec((d, tn), lambda i, j: (0, j)),
            pl.BlockSpec((d, tn), lambda i, j: (0, j + nj)),
            pl.BlockSpec((1, tn), lambda i, j: (0, j)),
            pl.BlockSpec((1, tn), lambda i, j: (0, j + nj)),
        ],
        out_specs=pl.BlockSpec((tm, tn), lambda i, j: (i, j)),
        scratch_shapes=[pltpu.VMEM((tm, d), BF16)],
        compiler_params=_params("parallel", "arbitrary"),
        name="conformer_glu",
    )(x, g.reshape(1, d), w, w, b, b)


def _dwconv_kernel(u_ref, halo_ref, x_ref, wdw_ref, bdw_ref, lng_ref, lnb_ref, w2_ref, b2_ref, o_ref,
                   ext_ref, cv_ref, slab_ref, *, tm, halo, width, tiles_per_seq, row_chunk, col_chunk):
    i = pl.program_id(0)
    d = u_ref.shape[1]
    first = (i % tiles_per_seq) == 0
    ext_ref[0:halo, :] = jnp.where(first, 0.0, halo_ref[...])
    ext_ref[halo:, :] = u_ref[...]
    off = halo - (width - 1)
    for r0 in range(0, tm, row_chunk):
        for c0 in range(0, d, col_chunk):
            cs = slice(c0, c0 + col_chunk)
            acc = jnp.broadcast_to(bdw_ref[:, cs], (row_chunk, col_chunk))
            for r in range(min(SUBLANES, width)):
                taps = range(r, width, SUBLANES)
                lo = r0 + off + r
                rows = row_chunk + (len(taps) - 1) * SUBLANES
                slab_ref[r, 0:rows, :] = ext_ref[lo:lo + rows, cs]
                for q, k in enumerate(taps):
                    acc = acc + wdw_ref[k:k + 1, cs] * slab_ref[r, q * SUBLANES:q * SUBLANES + row_chunk, :]
            cv_ref[r0:r0 + row_chunk, cs] = acc
    u = cv_ref[...]
    mu = jnp.mean(u, axis=-1, keepdims=True)
    uc = u - mu
    var = jnp.mean(uc * uc, axis=-1, keepdims=True)
    y = uc * lax.rsqrt(var + EPS) * lng_ref[...] + lnb_ref[...]
    s = (y * _sigmoid(y)).astype(BF16)
    o_ref[...] = x_ref[...] + _dot(s, w2_ref[...]) + b2_ref[...]


def _conformer_conv_out(u, x, w_dw, b_dw, ln_g, ln_b, w_pw2, b_pw2, seq):
    t, d = x.shape
    width = w_dw.shape[0]
    halo = 32
    assert width - 1 <= halo
    tm = min(256, seq)
    assert seq % tm == 0 and tm % halo == 0
    row_chunk, col_chunk = 64, 256
    kern = functools.partial(_dwconv_kernel, tm=tm, halo=halo, width=width, tiles_per_seq=seq // tm,
                             row_chunk=row_chunk, col_chunk=col_chunk)
    slab_rows = row_chunk + (width - 1) // SUBLANES * SUBLANES
    vec = lambda: pl.BlockSpec((1, d), lambda i: (0, 0))
    return pl.pallas_call(
        kern,
        out_shape=jax.ShapeDtypeStruct((t, d), F32),
        grid=(t // tm,),
        in_specs=[
            pl.BlockSpec((tm, d), lambda i: (i, 0)),
            pl.BlockSpec((halo, d), lambda i: (jnp.maximum(i * (tm // halo) - 1, 0), 0)),
            pl.BlockSpec((tm, d), lambda i: (i, 0)),
            pl.BlockSpec((width, d), lambda i: (0, 0)),
            vec(), vec(), vec(),
            pl.BlockSpec((d, d), lambda i: (0, 0)),
            vec(),
        ],
        out_specs=pl.BlockSpec((tm, d), lambda i: (i, 0)),
        scratch_shapes=[pltpu.VMEM((tm + halo, d), F32), pltpu.VMEM((tm, d), F32),
                        pltpu.VMEM((SUBLANES, slab_rows, col_chunk), F32)],
        compiler_params=_params("parallel"),
        name="conformer_dwconv_out",
    )(u, u, x, w_dw, b_dw.reshape(1, d), ln_g.reshape(1, d), ln_b.reshape(1, d), w_pw2.astype(BF16),
      b_pw2.reshape(1, d))


def _in_proj_kernel(x_ref, g_ref, wn_ref, wt_ref, perm_ref, on_ref, oc_ref, oq_ref, ov_ref, *, n_qg, row_chunk):
    hn = _rms_rows(x_ref[...], g_ref[...]).astype(BF16)
    nk = on_ref.shape[1]
    on_ref[...] = _dot(hn, wn_ref[:, 0:nk])
    kcvc = _dot(hn, wn_ref[:, nk:]).astype(BF16)
    regrouped = _dot(perm_ref[...], kcvc).astype(BF16)
    per = regrouped.shape[0] // CMP_STRIDE
    for l in range(CMP_STRIDE):
        oc_ref[0, l] = regrouped[l * per:(l + 1) * per, :]
    for r0 in range(0, n_qg, row_chunk):
        r1 = min(r0 + row_chunk, n_qg)
        oq_ref[r0:r1, :] = _dot_nt(wt_ref[r0:r1, :], hn)
    vt = _dot_nt(wt_ref[n_qg:, :], hn).astype(BF16)
    ones = jnp.ones((BF16_ROWS, V_TILE), BF16)
    for jj in range(ov_ref.shape[1]):
        for g in range(N_KV):
            for half in range(2):
                r0 = g * LANES + half * HEAD_DIM
                ov_ref[0, jj, g, half * V_ROWS:half * V_ROWS + HEAD_DIM, :] = (
                    vt[r0:r0 + HEAD_DIM, jj * V_TILE:(jj + 1) * V_TILE])
                ov_ref[0, jj, g, half * V_ROWS + HEAD_DIM:(half + 1) * V_ROWS, :] = ones


def _in_proj(x, g, wn, wt, batch, seq):
    t, d = x.shape
    tm = min(512, seq)
    n_qg = N_HEADS * HEAD_DIM + N_KV * GATE_ROWS
    assert seq % tm == 0 and tm % V_TILE == 0 and wt.shape[0] == n_qg + N_KV * LANES
    per_seq = seq // tm
    nk = N_KV * LANES
    ncv = wn.shape[1] - nk
    per = tm // CMP_STRIDE
    rows = np.arange(tm)
    perm = np.zeros((tm, tm), np.float32)
    perm[(rows % CMP_STRIDE) * per + rows // CMP_STRIDE, rows] = 1.0
    kern = functools.partial(_in_proj_kernel, n_qg=n_qg, row_chunk=512)
    return pl.pallas_call(
        kern,
        out_shape=(jax.ShapeDtypeStruct((t, nk), F32),
                   jax.ShapeDtypeStruct((t // tm, CMP_STRIDE, per, ncv), BF16),
                   jax.ShapeDtypeStruct((n_qg, t), F32),
                   jax.ShapeDtypeStruct((batch, seq // V_TILE, N_KV, 2 * V_ROWS, V_TILE), BF16)),
        grid=(t // tm,),
        in_specs=[
            pl.BlockSpec((tm, d), lambda i: (i, 0)),
            pl.BlockSpec((1, d), lambda i: (0, 0)),
            pl.BlockSpec(wn.shape, lambda i: (0, 0)),
            pl.BlockSpec(wt.shape, lambda i: (0, 0)),
            pl.BlockSpec((tm, tm), lambda i: (0, 0)),
        ],
        out_specs=(pl.BlockSpec((tm, nk), lambda i: (i, 0)),
                   pl.BlockSpec((1, CMP_STRIDE, per, ncv), lambda i: (i, 0, 0, 0)),
                   pl.BlockSpec((n_qg, tm), lambda i: (0, i)),
                   pl.BlockSpec((1, tm // V_TILE, N_KV, 2 * V_ROWS, V_TILE),
                                lambda i: (i // per_seq, i % per_seq, 0, 0, 0))),
        compiler_params=_params("parallel"),
        name="nsa_in_proj",
    )(x, g.reshape(1, d), wn, wt, jnp.asarray(perm, BF16))


def _out_proj_kernel(a_ref, x_ref, w_ref, o_ref):
    o_ref[...] = x_ref[...] + _dot(a_ref[...], w_ref[...])


def _out_proj(a, x, w):
    t, d = x.shape
    tm = min(512, t)
    return pl.pallas_call(
        _out_proj_kernel,
        out_shape=jax.ShapeDtypeStruct((t, d), F32),
        grid=(t // tm,),
        in_specs=[
            pl.BlockSpec((tm, a.shape[1]), lambda i: (i, 0)),
            pl.BlockSpec((tm, d), lambda i: (i, 0)),
            pl.BlockSpec(w.shape, lambda i: (0, 0)),
        ],
        out_specs=pl.BlockSpec((tm, d), lambda i: (i, 0)),
        compiler_params=_params("parallel"),
        name="nsa_out_proj",
    )(a, x, w.astype(BF16))


def _kprep_kernel(k_ref, gain_ref, bd_ref, c_ref, s1_ref, s2_ref, kaug_ref, *, ts, seq):
    i = pl.program_id(0)
    bd = bd_ref[...]
    gain = gain_ref[...]
    c, s1, s2 = c_ref[...], s1_ref[...], s2_ref[...]
    tpos = (i * ts) % seq + lax.broadcasted_iota(jnp.int32, (ts, LANES), 0)
    lane = lax.broadcasted_iota(jnp.int32, (ts, LANES), 1)
    onehot = jnp.where(tpos // SEL_LEN == lane, 1.0, 0.0).astype(BF16)
    for g in range(N_KV):
        xb = k_ref[:, g * LANES:(g + 1) * LANES]
        xr = _rope(_head_rms(xb, bd, gain), c, s1, s2)
        kaug_ref[:, 2 * g * LANES:(2 * g + 1) * LANES] = xr.astype(BF16)
        kaug_ref[:, (2 * g + 1) * LANES:(2 * g + 2) * LANES] = onehot


def _kprep(proj, gain, bd, tabs, seq):
    t = proj.shape[0]
    ts = min(512, seq)
    assert seq % ts == 0
    nk = N_KV * LANES
    kern = functools.partial(_kprep_kernel, ts=ts, seq=seq)
    tab = lambda: pl.BlockSpec((ts, LANES), lambda i: (i % (seq // ts), 0))
    return pl.pallas_call(
        kern,
        out_shape=jax.ShapeDtypeStruct((t, 2 * nk), BF16),
        grid=(t // ts,),
        in_specs=[
            pl.BlockSpec((ts, nk), lambda i: (i, 0)),
            pl.BlockSpec((1, LANES), lambda i: (0, 0)),
            pl.BlockSpec((LANES, LANES), lambda i: (0, 0)),
            tab(), tab(), tab(),
        ],
        out_specs=pl.BlockSpec((ts, 2 * nk), lambda i: (i, 0)),
        compiler_params=_params("parallel"),
        name="nsa_key_prep",
    )(proj, gain, bd, *tabs)


def _compress_kernel(x_ref, w1k_ref, w1v_ref, ck_ref, cv_ref, w2k_ref, w2v_ref, gain_ref, o_ref, ot_ref,
                     acck_ref, accv_ref):
    l = pl.program_id(1)
    nc = acck_ref.shape[0]
    hid = w2k_ref.shape[0]
    hk = w1k_ref.shape[1]

    @pl.when(l == 0)
    def _():
        acck_ref[...] = jnp.zeros_like(acck_ref)
        accv_ref[...] = jnp.zeros_like(accv_ref)

    x = x_ref[0].reshape(nc, 2 * hk)
    acck_ref[...] += _dot(x[:, 0:hk], w1k_ref[0])
    accv_ref[...] += _dot(x[:, hk:], w1v_ref[0])

    @pl.when(l == pl.num_programs(1) - 1)
    def _():
        for g in range(N_KV):
            def hidden(acc_ref, c_ref):
                first = acc_ref[:, g * hid:(g + 1) * hid] + c_ref[0:1, :]
                second = acc_ref[:, (N_KV + g) * hid:(N_KV + g + 1) * hid] + c_ref[1:2, :]
                pre = first + pltpu.roll(second, nc - 1, 0)
                return (pre * _sigmoid(pre)).astype(BF16)

            kv = _dot(hidden(acck_ref, ck_ref), w2k_ref[...]) + _dot(hidden(accv_ref, cv_ref), w2v_ref[...])
            is_k = lax.broadcasted_iota(jnp.int32, kv.shape, 1) < HEAD_DIM
            ss = jnp.sum(jnp.where(is_k, kv * kv, 0.0), axis=-1, keepdims=True)
            kn = kv * lax.rsqrt(ss * (1.0 / HEAD_DIM) + EPS) * gain_ref[...]
            out = jnp.where(is_k, kn, kv)
            o_ref[g] = out.astype(BF16)
            ot_ref[g] = out.T.astype(BF16)


def _compress(kcvc, pe_k, pe_v, ck_w1, ck_w2, cv_w1, cv_w2, kc_g, batch, seq):
    nc = seq // CMP_STRIDE
    hid = ck_w1.shape[1]
    hk = N_KV * HEAD_DIM
    tiles, _, per, width = kcvc.shape
    assert CMP_LEN == 2 * CMP_STRIDE and width == 2 * hk and tiles * per == batch * nc
    x5 = kcvc.reshape(batch, tiles // batch, CMP_STRIDE, per, width)
    pad = lambda w, left: jnp.pad(w, ((0, 0), (HEAD_DIM, 0) if left else (0, HEAD_DIM))).astype(BF16)
    gain = jnp.concatenate([kc_g, jnp.ones((HEAD_DIM,), F32)]).reshape(1, LANES)

    def first_layer(pe, w1):
        w = w1.reshape(2, CMP_STRIDE, HEAD_DIM, hid)
        rows = lambda a, k: jnp.pad(a, ((0, 0), (k * HEAD_DIM, (N_KV - 1 - k) * HEAD_DIM), (0, 0)))
        bd = jnp.concatenate([rows(w[s], k) for s in range(2) for k in range(N_KV)], axis=2)
        const = jnp.einsum("sld,sldh->sh", pe.reshape(2, CMP_STRIDE, HEAD_DIM), w, precision=lax.Precision.HIGHEST)
        return bd.astype(BF16), const

    w1k, const_k = first_layer(pe_k, ck_w1)
    w1v, const_v = first_layer(pe_v, cv_w1)
    wide = 2 * N_KV * hid
    step_w = lambda: pl.BlockSpec((1, hk, wide), lambda b, l: (l, 0, 0))
    full = lambda shape: pl.BlockSpec(shape, lambda b, l: (0,) * len(shape))
    return pl.pallas_call(
        _compress_kernel,
        out_shape=(jax.ShapeDtypeStruct((batch * N_KV, nc, LANES), BF16),
                   jax.ShapeDtypeStruct((batch * N_KV, LANES, nc), BF16)),
        grid=(batch, CMP_STRIDE),
        in_specs=[pl.BlockSpec((1, tiles // batch, None, per, width), lambda b, l: (b, 0, l, 0, 0)),
                  step_w(), step_w(), full((2, hid)), full((2, hid)),
                  full((hid, LANES)), full((hid, LANES)), full((1, LANES))],
        out_specs=(pl.BlockSpec((N_KV, nc, LANES), lambda b, l: (b, 0, 0)),
                   pl.BlockSpec((N_KV, LANES, nc), lambda b, l: (b, 0, 0))),
        scratch_shapes=[pltpu.VMEM((nc, wide), F32), pltpu.VMEM((nc, wide), F32)],
        compiler_params=_params("parallel", "arbitrary"),
        name="nsa_compress",
    )(x5, w1k, w1v, const_k, const_v, pad(ck_w2, False), pad(cv_w2, True), gain)


def _attn_kernel(qt_ref, gate_ref, kaug_ref, vt_ref, kvc_ref, kvct_ref, cmapt_ref, qg_ref, cos_ref, sin_ref,
                 o_ref,
                 qaug_ref, qwin_ref, oc_ref, ms_ref, ls_ref, accs_ref, mw_ref, lw_ref, accw_ref,
                 sa_ref, sb_ref, wdiag_ref, wold_ref, imp_ref,
                 *, tq, tk, k_top, n_cmp_var):
    qi = pl.program_id(2)
    t0 = qi * tq
    cols = GROUP * tq
    half = ROT_DIM // 2

    gain = qg_ref[...]
    cos, sin = cos_ref[...], sin_ref[...]
    nope, rope = [], []
    for r in range(GROUP):
        x = qt_ref[r * HEAD_DIM:(r + 1) * HEAD_DIM, :]
        ss = jnp.sum(x * x, axis=0, keepdims=True)
        xn = x * lax.rsqrt(ss * (1.0 / HEAD_DIM) + EPS) * gain
        x1, x2 = xn[0:half], xn[half:ROT_DIM]
        nope.append(xn)
        rope.append(jnp.concatenate([x1 * cos - x2 * sin, x2 * cos + x1 * sin, xn[ROT_DIM:]], axis=0))
    zeros = jnp.zeros((HEAD_DIM, cols), BF16)
    q_nope = jnp.concatenate([jnp.concatenate(nope, axis=1).astype(BF16), zeros], axis=0)
    q_rope = jnp.concatenate(rope, axis=1).astype(BF16)
    qwin_ref[...] = jnp.concatenate([zeros, q_rope], axis=0)

    tpos = t0 + lax.broadcasted_iota(jnp.int32, (1, tq), 1)

    def per_head(x):
        return jnp.concatenate([x] * GROUP, axis=1)

    def fold_tile(s, keep, k0, v_rows, m_ref, l_ref, acc_ref):
        if keep is not None:
            s = jnp.where(per_head(keep), s, NEG)
        m_prev = m_ref[...]
        m_new = jnp.maximum(m_prev, jnp.max(s, axis=0, keepdims=True))
        p = jnp.exp2(s - m_new)
        alpha = jnp.exp2(m_prev - m_new)
        pb = p.astype(BF16)
        v_tile0 = k0 // V_TILE
        pv = _dot(vt_ref[0, v_tile0, 0, v_rows, :], pb[0:V_TILE])
        for c in range(1, s.shape[0] // V_TILE):
            pv = pv + _dot(vt_ref[0, v_tile0 + c, 0, v_rows, :], pb[c * V_TILE:(c + 1) * V_TILE])
        l_ref[...] = alpha * l_ref[...] + pv[HEAD_DIM:HEAD_DIM + 1]
        acc_ref[...] = alpha * acc_ref[...] + pv[0:HEAD_DIM]
        m_ref[...] = m_new

    def back(k0, size):
        return tpos - (k0 + lax.broadcasted_iota(jnp.int32, (size, 1), 0))

    def sel_scores(k0):
        return _dot(kaug_ref[pl.ds(k0, tk), :], qaug_ref[...])

    def win_scores(k0, size):
        return _dot(kaug_ref[pl.ds(k0, size), 0:LANES], qwin_ref[...])

    for m_ref, l_ref, acc_ref in ((ms_ref, ls_ref, accs_ref), (mw_ref, lw_ref, accw_ref)):
        m_ref[...] = jnp.full_like(m_ref, NEG)
        l_ref[...] = jnp.zeros_like(l_ref)
        acc_ref[...] = jnp.zeros_like(acc_ref)

    t0a = pl.multiple_of(t0, tq)
    k_old = pl.multiple_of(jnp.maximum(t0 - WINDOW, 0), tq)
    wdiag_ref[...] = win_scores(t0a, tq)
    wold_ref[...] = win_scores(k_old, WINDOW)
    qaug_ref[0:LANES, :] = jnp.concatenate([q_rope, zeros], axis=0)
    sa_ref[...] = _dot(kaug_ref[0:tk, 0:LANES], qaug_ref[0:LANES, :])
    win_state = (slice(V_ROWS, 2 * V_ROWS), mw_ref, lw_ref, accw_ref)
    fold_tile(wdiag_ref[...], back(t0a, tq) >= 0, t0a, *win_state)
    b_old = back(k_old, WINDOW)
    fold_tile(wold_ref[...], (b_old < WINDOW) & (b_old > tpos - t0), k_old, *win_state)

    nc = kvc_ref.shape[1]

    def cmp_branch(rows):
        kvc = kvc_ref[0, 0:rows, :]
        cmp_end = lax.broadcasted_iota(jnp.int32, (rows, 1), 0) * CMP_STRIDE + (CMP_LEN - 1)
        s_c = jnp.where(per_head(cmp_end <= tpos), _dot(kvc, q_nope), NEG)
        e = jnp.exp2(s_c - jnp.maximum(jnp.max(s_c, axis=0, keepdims=True), 0.1 * NEG))
        p_c = e * (1.0 / jnp.maximum(jnp.sum(e, axis=0, keepdims=True), 1e-30))
        oc_ref[...] = _dot(kvct_ref[0, HEAD_DIM:, 0:rows], p_c.astype(BF16))
        p_sum = p_c[:, 0:tq]
        for r in range(1, GROUP):
            p_sum = p_sum + p_c[:, r * tq:(r + 1) * tq]
        hi, lo = _split_bf16(p_sum)
        imp_ref[...] = _dot(cmapt_ref[:, 0:rows], hi) + _dot(cmapt_ref[:, 0:rows], lo)

    variant = (t0 + tq - 1) // (CMP_STRIDE * (nc // n_cmp_var))
    for v in range(n_cmp_var):
        pl.when(variant == v)(functools.partial(cmp_branch, (v + 1) * (nc // n_cmp_var)))
    imp = imp_ref[...]

    blk = lax.broadcasted_iota(jnp.int32, (LANES, tq), 0)
    cur = tpos // SEL_LEN
    forced = (blk == 0) | (blk == cur) | (blk == cur - 1)
    taken = -3e38
    imp = jnp.where(blk > cur, -1.0, jnp.where(forced, taken, imp))
    blk_f = blk.astype(F32)
    for _ in range(max(k_top - 3, 0)):
        mx = jnp.max(imp, axis=0, keepdims=True)
        idx = jnp.min(jnp.where(imp == mx, blk_f, float(LANES)), axis=0, keepdims=True)
        imp = jnp.where(blk_f == idx, taken, imp)
    bias = jnp.where(imp == taken, 0.0, NEG)
    qaug_ref[LANES:, :] = per_head(bias.astype(BF16))
    blocks0 = tk // SEL_LEN
    bias0 = jnp.broadcast_to(bias[0:blocks0, None, :], (blocks0, SEL_LEN, tq)).reshape(tk, tq)
    sa_ref[...] = sa_ref[...] + per_head(bias0)


    n = t0 // tk
    sel_state = (slice(0, V_ROWS), ms_ref, ls_ref, accs_ref)

    def tile_pair(k0):
        sb_ref[...] = sel_scores(k0 + tk)
        fold_tile(sa_ref[...], None, k0, *sel_state)
        sa_ref[...] = sel_scores(k0 + 2 * tk)
        fold_tile(sb_ref[...], None, k0 + tk, *sel_state)

    def quad_body(i, carry):
        k0 = pl.multiple_of(4 * i * tk, tk)
        tile_pair(k0)
        tile_pair(k0 + 2 * tk)
        return carry

    n_pairs = n // 2
    lax.fori_loop(0, n_pairs // 2, quad_body, 0)

    @pl.when(n_pairs % 2 == 1)
    def _():
        tile_pair(pl.multiple_of((n_pairs - 1) * 2 * tk, tk))

    k_diag = pl.multiple_of(n * tk, tk)

    def diagonal_fold(buf_ref):
        if tk == tq:
            fold_tile(buf_ref[...], back(k_diag, tk) >= 0, k_diag, *sel_state)
            return
        pl.when(t0 % tk == 0)(lambda: fold_tile(buf_ref[0:tq, :], back(k_diag, tq) >= 0, k_diag, *sel_state))
        pl.when(t0 % tk != 0)(lambda: fold_tile(buf_ref[...], back(k_diag, tk) >= 0, k_diag, *sel_state))

    @pl.when(n % 2 == 1)
    def _():
        sb_ref[...] = sel_scores(k_diag)
        fold_tile(sa_ref[...], None, k_diag - tk, *sel_state)
        diagonal_fold(sb_ref)

    @pl.when(n % 2 == 0)
    def _():
        diagonal_fold(sa_ref)

    o_s = accs_ref[...] / ls_ref[...]
    o_w = accw_ref[...] / lw_ref[...]
    o_c = oc_ref[...]
    sig = _sigmoid(gate_ref[...])
    heads = []
    for r in range(GROUP):
        cs = slice(r * tq, (r + 1) * tq)
        g_c, g_s, g_w = (sig[3 * r + b:3 * r + b + 1, :] for b in range(3))
        heads.append(g_c * o_c[:, cs] + g_s * o_s[:, cs] + g_w * o_w[:, cs])
    o_ref[...] = jnp.concatenate(heads, axis=0).T.astype(o_ref.dtype)


def _attention(qgt, kaug, vt, kvc, kvct, cmapt, qg, cos, sin, batch, seq):
    t = qgt.shape[1]
    tq = min(Q_TILE, seq)
    tk = min(512, seq)
    nq = seq // tq
    nc = kvc.shape[1]
    assert seq % tk == 0 and tk % tq == 0 and WINDOW % tq == 0 and tq % V_TILE == 0 and seq >= WINDOW
    cols = GROUP * tq
    hq = GROUP * HEAD_DIM
    k_top = min(SEL_TOPK, seq // SEL_LEN)
    n_cmp_var = max(1, min(4, nc // LANES))
    assert k_top >= 3 and nc % n_cmp_var == 0
    kern = functools.partial(_attn_kernel, tq=tq, tk=tk, k_top=k_top, n_cmp_var=n_cmp_var)
    tab = lambda: pl.BlockSpec((ROT_DIM // 2, tq), lambda b, g, qi: (0, qi))
    stat = lambda: pltpu.VMEM((1, cols), F32)
    acc = lambda: pltpu.VMEM((HEAD_DIM, cols), F32)
    return pl.pallas_call(
        kern,
        out_shape=jax.ShapeDtypeStruct((t, N_HEADS * HEAD_DIM), BF16),
        grid=(batch, N_KV, nq),
        in_specs=[
            pl.BlockSpec((hq, tq), lambda b, g, qi: (g, b * nq + qi)),
            pl.BlockSpec((GATE_ROWS, tq), lambda b, g, qi: (N_HEADS * HEAD_DIM // GATE_ROWS + g, b * nq + qi)),
            pl.BlockSpec((seq, 2 * LANES), lambda b, g, qi: (b, g)),
            pl.BlockSpec((1, seq // V_TILE, 1, 2 * V_ROWS, V_TILE), lambda b, g, qi: (b, 0, g, 0, 0)),
            pl.BlockSpec((1, nc, LANES), lambda b, g, qi: (b * N_KV + g, 0, 0)),
            pl.BlockSpec((1, LANES, nc), lambda b, g, qi: (b * N_KV + g, 0, 0)),
            pl.BlockSpec((LANES, nc), lambda b, g, qi: (0, 0)),
            pl.BlockSpec((HEAD_DIM, tq), lambda b, g, qi: (0, 0)),
            tab(), tab(),
        ],
        out_specs=pl.BlockSpec((tq, hq), lambda b, g, qi: (b * nq + qi, g)),
        scratch_shapes=[
            pltpu.VMEM((2 * LANES, cols), BF16), pltpu.VMEM((LANES, cols), BF16), acc(),
            stat(), stat(), acc(), stat(), stat(), acc(),
            pltpu.VMEM((tk, cols), F32), pltpu.VMEM((tk, cols), F32), pltpu.VMEM((tq, cols), F32), pltpu.VMEM((WINDOW, cols), F32),
            pltpu.VMEM((LANES, tq), F32),
        ],
        compiler_params=_params("parallel", "parallel", "arbitrary"),
        name="nsa_attention",
    )(qgt, qgt, kaug, vt, kvc, kvct, cmapt, qg, cos, sin)


def _rope_angles(seq):
    half = ROT_DIM // 2
    inv_freq = ROPE_THETA ** (-jnp.arange(half, dtype=F32) * (2.0 / ROT_DIM))
    ang = jnp.arange(seq, dtype=F32)[:, None] * inv_freq[None, :]
    return jnp.cos(ang), jnp.sin(ang)


def _rope_tables(cos, sin):
    seq, half = cos.shape
    zeros = jnp.zeros((seq, HEAD_DIM - ROT_DIM), F32)
    zh = jnp.zeros((seq, half), F32)
    c = jnp.concatenate([cos, cos, zeros + 1.0], axis=1)
    s1 = jnp.concatenate([-sin, zh, zeros], axis=1)
    s2 = jnp.concatenate([zh, sin, zeros], axis=1)
    reps = LANES // HEAD_DIM
    return tuple(jnp.tile(a, (1, reps)) for a in (c, s1, s2))


def _cmp_to_sel_t(nc, n_sel):
    start_c = np.arange(nc)[None, :] * CMP_STRIDE
    start_s = np.arange(LANES)[:, None] * SEL_LEN
    ov = np.minimum(start_c + CMP_LEN, start_s + SEL_LEN) - np.maximum(start_c, start_s)
    m = np.maximum(ov, 0).astype(np.float32) / CMP_LEN
    m[n_sel:, :] = 0.0
    m[:, nc - 1:] = 0.0
    return jnp.asarray(m, BF16)


def _nsa_in_weights(w_in):
    d = w_in.shape[0]
    hq, hk = N_HEADS * HEAD_DIM, N_KV * HEAD_DIM
    offs = np.cumsum([0, hq] + [hk] * 6)
    q, kc, vc, ks, vs, kw, vw = (w_in[:, offs[n]:offs[n + 1]] for n in range(7))
    gl = w_in[:, offs[7]:]
    per_group = lambda a, b: jnp.stack([a.reshape(d, N_KV, HEAD_DIM), b.reshape(d, N_KV, HEAD_DIM)],
                                       axis=2).reshape(d, N_KV * LANES)
    gates = jnp.pad(gl.reshape(d, N_KV, 3 * GROUP), ((0, 0), (0, 0), (0, GATE_ROWS - 3 * GROUP)))
    wn = jnp.concatenate([per_group(ks, kw), kc, vc], axis=1).astype(BF16)
    wt = jnp.concatenate([q, gates.reshape(d, N_KV * GATE_ROWS), per_group(vs, vw)], axis=1).T.astype(BF16)
    return wn, wt


def _nsa_layer(x, g, w_in, q_g, kc_g, ks_g, kw_g, pe_k, pe_v, ck_w1, ck_w2, cv_w1, cv_w2, w_out, batch, seq):
    nc = seq // CMP_STRIDE
    n_sel = seq // SEL_LEN
    assert n_sel <= LANES and seq % CMP_STRIDE == 0 and 3 * GROUP <= GATE_ROWS
    wn, wt = _nsa_in_weights(w_in)
    proj, kcvc, qgt, vt = _in_proj(x, g, wn, wt, batch, seq)
    cos, sin = _rope_angles(seq)
    seg = np.arange(LANES) // HEAD_DIM
    bd = jnp.asarray(seg[:, None] == seg[None, :], BF16)
    kaug = _kprep(proj, jnp.concatenate([ks_g, kw_g]).reshape(1, LANES), bd, _rope_tables(cos, sin), seq)
    kvc, kvct = _compress(kcvc, pe_k, pe_v, ck_w1, ck_w2, cv_w1, cv_w2, kc_g, batch, seq)
    tq = min(Q_TILE, seq)
    qg = jnp.broadcast_to((q_g * float(HEAD_DIM ** -0.5 * np.log2(np.e)))[:, None], (HEAD_DIM, tq))
    attn = _attention(qgt, kaug, vt, kvc, kvct, _cmp_to_sel_t(nc, n_sel), qg, cos.T, sin.T, batch, seq)
    return _out_proj(attn, x, w_out)


def kernel(x, mix_norm_g, ffn_norm_g, conv_w_pw1, conv_b_pw1, conv_w_dw, conv_b_dw, conv_ln_g, conv_ln_b, conv_w_pw2, conv_b_pw2, nsa_w_in, nsa_q_norm, nsa_kc_norm, nsa_ks_norm, nsa_kw_norm, nsa_pe_k, nsa_pe_v, nsa_ck_w1, nsa_ck_w2, nsa_cv_w1, nsa_cv_w2, nsa_w_out, ffn_w_up, ffn_w_dw, ffn_b_dw, ffn_w_down):
    batch, seq, d = x.shape
    depth = mix_norm_g.shape[0]
    n_mixers = 2
    h = x.reshape(batch * seq, d)
    w_up_all, w_down_all = ffn_w_up.astype(BF16), ffn_w_down.astype(BF16)
    for i in range(depth):
        j = i // n_mixers
        if i % n_mixers == 0:
            u = _conformer_glu(h, mix_norm_g[i], conv_w_pw1[j], conv_b_pw1[j])
            h = _conformer_conv_out(u, h, conv_w_dw[j], conv_b_dw[j], conv_ln_g[j], conv_ln_b[j],
                                    conv_w_pw2[j], conv_b_pw2[j], seq)
        else:
            h = _nsa_layer(h, mix_norm_g[i], nsa_w_in[j], nsa_q_norm[j], nsa_kc_norm[j], nsa_ks_norm[j],
                           nsa_kw_norm[j], nsa_pe_k[j], nsa_pe_v[j], nsa_ck_w1[j], nsa_ck_w2[j], nsa_cv_w1[j],
                           nsa_cv_w2[j], nsa_w_out[j], batch, seq)
        h = _conv_ffn(h, ffn_norm_g[i], w_up_all, ffn_w_dw[i], ffn_b_dw[i], w_down_all, i, seq)
    return h.reshape(batch, seq, d)
```

```python
import functools

import numpy as np
import jax
import jax.numpy as jnp
from jax import lax
from jax.experimental import pallas as pl
from jax.experimental.pallas import tpu as pltpu

N_HEADS = 16
HEAD_DIM = 64
N_KV = 4
GROUP = N_HEADS // N_KV
ROT_DIM = HEAD_DIM // 4
ROPE_THETA = 500000.0
CMP_LEN = 32
CMP_STRIDE = 16
SEL_LEN = 64
SEL_TOPK = 16
WINDOW = 512
EPS = 1e-6
NEG = -1e30

LANES = 128
SUBLANES = 8
BF16_ROWS = 16
V_TILE = 256
Q_TILE = 512
V_ROWS = HEAD_DIM + BF16_ROWS
GATE_ROWS = 16
VMEM_LIMIT = 48 * 1024 * 1024

F32 = jnp.float32
BF16 = jnp.bfloat16
NT_DIMS = (((1,), (1,)), ((), ()))


def _params(*sem):
    return pltpu.CompilerParams(dimension_semantics=sem, vmem_limit_bytes=VMEM_LIMIT)


def _dot(a, b):
    return jnp.dot(a, b, preferred_element_type=F32)


def _dot_nt(a, b):
    return lax.dot_general(a, b, NT_DIMS, preferred_element_type=F32)


def _split_bf16(x):
    hi = x.astype(BF16)
    lo = (x - hi.astype(F32)).astype(BF16)
    return hi, lo


def _rms_rows(x, g):
    ms = jnp.mean(x * x, axis=-1, keepdims=True)
    return x * lax.rsqrt(ms + EPS) * g


def _sigmoid(x):
    return 1.0 / (1.0 + jnp.exp(-x))


def _head_rms(xb, bd, gain):
    hi, lo = _split_bf16(xb * xb)
    ss = _dot(hi, bd) + _dot(lo, bd)
    return xb * lax.rsqrt(ss * (1.0 / HEAD_DIM) + EPS) * gain


def _rope(xb, c, s1, s2):
    half = ROT_DIM // 2
    return xb * c + pltpu.roll(xb, LANES - half, 1) * s1 + pltpu.roll(xb, half, 1) * s2


def _ffn_kernel(x_ref, halo_ref, g_ref, wup_ref, wdw_ref, bdw_ref, wd_ref, o_ref,
                hn_ref, a0_ref, v0_ref, a1_ref, v1_ref, *, tm, tiles_per_seq, nch):
    i = pl.program_id(0)
    hl = BF16_ROWS
    g = g_ref[...]
    hn_ref[hl:, :] = _rms_rows(x_ref[...], g).astype(BF16)
    first = (i % tiles_per_seq) == 0
    hn_ref[0:hl, :] = jnp.where(first, 0.0, _rms_rows(halo_ref[...], g)).astype(BF16)
    o_ref[...] = x_ref[...]

    tf = a0_ref.shape[1]

    def cols(c):
        start = c * tf
        return pl.ds(start if isinstance(start, int) else pl.multiple_of(start, tf), tf)

    def up(c, a_ref, v_ref):
        a_ref[...] = _dot(hn_ref[...], wup_ref[:, cols(c)])
        v_ref[...] = _dot(hn_ref[hl:, :], wup_ref[:, cols(nch + c)])

    def down(c, a_ref, v_ref):
        w = wdw_ref[c]
        cv = (w[0:1] * a_ref[hl - 2:hl - 2 + tm, :] + w[1:2] * a_ref[hl - 1:hl - 1 + tm, :]
              + w[2:3] * a_ref[hl:hl + tm, :] + bdw_ref[c])
        act = (cv * _sigmoid(cv) * v_ref[...]).astype(BF16)
        o_ref[...] += _dot(act, wd_ref[c])

    buf0, buf1 = (a0_ref, v0_ref), (a1_ref, v1_ref)
    up(0, *buf0)

    def chunk_pair(c):
        up(c + 1, *buf1)
        down(c, *buf0)
        up(c + 2, *buf0)
        down(c + 1, *buf1)

    for p in range((nch - 1) // 2):
        chunk_pair(2 * p)
    if nch % 2 == 1:
        down(nch - 1, *buf0)
    else:
        up(nch - 1, *buf1)
        down(nch - 2, *buf0)
        down(nch - 1, *buf1)


def _conv_ffn(x, g, w_up_all, w_dw, b_dw, w_down_all, layer, seq):
    t, d = x.shape
    dff = w_down_all.shape[1]
    tm = min(512, seq)
    tf = 256
    assert seq % tm == 0 and dff % tf == 0 and w_dw.shape[0] == 3
    nch = dff // tf
    hl = BF16_ROWS
    kern = functools.partial(_ffn_kernel, tm=tm, tiles_per_seq=seq // tm, nch=nch)
    wup = w_up_all
    wdn = w_down_all.reshape(w_down_all.shape[0], nch, tf, d)
    wdw = w_dw.reshape(3, nch, tf).transpose(1, 0, 2)
    bdw = b_dw.reshape(nch, 1, tf)
    resident = lambda a: pl.BlockSpec(a.shape, lambda i: (0,) * a.ndim, pipeline_mode=pl.Buffered(1))
    of_layer = lambda a: pl.BlockSpec((None,) + a.shape[1:], lambda i: (layer,) + (0,) * (a.ndim - 1),
                                      pipeline_mode=pl.Buffered(1))
    return pl.pallas_call(
        kern,
        out_shape=jax.ShapeDtypeStruct((t, d), F32),
        grid=(t // tm,),
        in_specs=[
            pl.BlockSpec((tm, d), lambda i: (i, 0)),
            pl.BlockSpec((hl, d), lambda i: (jnp.maximum(i * (tm // hl) - 1, 0), 0)),
            pl.BlockSpec((1, d), lambda i: (0, 0)),
            of_layer(wup), resident(wdw), resident(bdw), of_layer(wdn),
        ],
        out_specs=pl.BlockSpec((tm, d), lambda i: (i, 0)),
        scratch_shapes=[
            pltpu.VMEM((tm + hl, d), BF16),
            pltpu.VMEM((tm + hl, tf), F32), pltpu.VMEM((tm, tf), F32),
            pltpu.VMEM((tm + hl, tf), F32), pltpu.VMEM((tm, tf), F32),
        ],
        compiler_params=_params("parallel"),
        name="conv_ffn",
    )(x, x, g.reshape(1, d), wup, wdw, bdw, wdn)


def _glu_kernel(x_ref, g_ref, wa_ref, wg_ref, ba_ref, bg_ref, o_ref, hn_ref):
    @pl.when(pl.program_id(1) == 0)
    def _():
        hn_ref[...] = _rms_rows(x_ref[...], g_ref[...]).astype(BF16)

    th = hn_ref.shape[0] // 2
    halves = [(_dot(hn_ref[h * th:(h + 1) * th, :], wa_ref[...]), _dot(hn_ref[h * th:(h + 1) * th, :], wg_ref[...]))
              for h in range(2)]
    for h, (a, gate) in enumerate(halves):
        o_ref[h * th:(h + 1) * th, :] = (a + ba_ref[...]) * _sigmoid(gate + bg_ref[...])


def _conformer_glu(x, g, w_pw1, b_pw1):
    t, d = x.shape
    tm = min(1024, t)
    tn = 256
    nj = d // tn
    w = w_pw1.astype(BF16)
    b = b_pw1.reshape(1, 2 * d)
    return pl.pallas_call(
        _glu_kernel,
        out_shape=jax.ShapeDtypeStruct((t, d), F32),
        grid=(t // tm, nj),
        in_specs=[
            pl.BlockSpec((tm, d), lambda i, j: (i, 0)),
            pl.BlockSpec((1, d), lambda i, j: (0, 0)),
            pl.BlockSpec((d, tn), lambda i, j: (0, j)),
            pl.BlockSpec((d, tn), lambda i, j: (0, j + nj)),
            pl.BlockSpec((1, tn), lambda i, j: (0, j)),
            pl.BlockSpec((1, tn), lambda i, j: (0, j + nj)),
        ],
        out_specs=pl.BlockSpec((tm, tn), lambda i, j: (i, j)),
        scratch_shapes=[pltpu.VMEM((tm, d), BF16)],
        compiler_params=_params("parallel", "arbitrary"),
        name="conformer_glu",
    )(x, g.reshape(1, d), w, w, b, b)


def _dwconv_kernel(u_ref, halo_ref, x_ref, wdw_ref, bdw_ref, lng_ref, lnb_ref, w2_ref, b2_ref, o_ref,
                   ext_ref, cv_ref, slab_ref, *, tm, halo, width, tiles_per_seq, row_chunk, col_chunk):
    i = pl.program_id(0)
    d = u_ref.shape[1]
    first = (i % tiles_per_seq) == 0
    ext_ref[0:halo, :] = jnp.where(first, 0.0, halo_ref[...])
    ext_ref[halo:, :] = u_ref[...]
    off = halo - (width - 1)
    for r0 in range(0, tm, row_chunk):
        for c0 in range(0, d, col_chunk):
            cs = slice(c0, c0 + col_chunk)
            acc = jnp.broadcast_to(bdw_ref[:, cs], (row_chunk, col_chunk))
            for r in range(min(SUBLANES, width)):
                taps = range(r, width, SUBLANES)
                lo = r0 + off + r
                rows = row_chunk + (len(taps) - 1) * SUBLANES
                slab_ref[r, 0:rows, :] = ext_ref[lo:lo + rows, cs]
                for q, k in enumerate(taps):
                    acc = acc + wdw_ref[k:k + 1, cs] * slab_ref[r, q * SUBLANES:q * SUBLANES + row_chunk, :]
            cv_ref[r0:r0 + row_chunk, cs] = acc
    u = cv_ref[...]
    mu = jnp.mean(u, axis=-1, keepdims=True)
    uc = u - mu
    var = jnp.mean(uc * uc, axis=-1, keepdims=True)
    y = uc * lax.rsqrt(var + EPS) * lng_ref[...] + lnb_ref[...]
    s = (y * _sigmoid(y)).astype(BF16)
    o_ref[...] = x_ref[...] + _dot(s, w2_ref[...]) + b2_ref[...]


def _conformer_conv_out(u, x, w_dw, b_dw, ln_g, ln_b, w_pw2, b_pw2, seq):
    t, d = x.shape
    width = w_dw.shape[0]
    halo = 32
    assert width - 1 <= halo
    tm = min(256, seq)
    assert seq % tm == 0 and tm % halo == 0
    row_chunk, col_chunk = 64, 256
    kern = functools.partial(_dwconv_kernel, tm=tm, halo=halo, width=width, tiles_per_seq=seq // tm,
                             row_chunk=row_chunk, col_chunk=col_chunk)
    slab_rows = row_chunk + (width - 1) // SUBLANES * SUBLANES
    vec = lambda: pl.BlockSpec((1, d), lambda i: (0, 0))
    return pl.pallas_call(
        kern,
        out_shape=jax.ShapeDtypeStruct((t, d), F32),
        grid=(t // tm,),
        in_specs=[
            pl.BlockSpec((tm, d), lambda i: (i, 0)),
            pl.BlockSpec((halo, d), lambda i: (jnp.maximum(i * (tm // halo) - 1, 0), 0)),
            pl.BlockSpec((tm, d), lambda i: (i, 0)),
            pl.BlockSpec((width, d), lambda i: (0, 0)),
            vec(), vec(), vec(),
            pl.BlockSpec((d, d), lambda i: (0, 0)),
            vec(),
        ],
        out_specs=pl.BlockSpec((tm, d), lambda i: (i, 0)),
        scratch_shapes=[pltpu.VMEM((tm + halo, d), F32), pltpu.VMEM((tm, d), F32),
                        pltpu.VMEM((SUBLANES, slab_rows, col_chunk), F32)],
        compiler_params=_params("parallel"),
        name="conformer_dwconv_out",
    )(u, u, x, w_dw, b_dw.reshape(1, d), ln_g.reshape(1, d), ln_b.reshape(1, d), w_pw2.astype(BF16),
      b_pw2.reshape(1, d))


def _in_proj_kernel(x_ref, g_ref, wn_ref, wt_ref, perm_ref, on_ref, oc_ref, oq_ref, ov_ref, *, n_qg, row_chunk):
    hn = _rms_rows(x_ref[...], g_ref[...]).astype(BF16)
    nk = on_ref.shape[1]
    on_ref[...] = _dot(hn, wn_ref[:, 0:nk])
    kcvc = _dot(hn, wn_ref[:, nk:]).astype(BF16)
    regrouped = _dot(perm_ref[...], kcvc).astype(BF16)
    per = regrouped.shape[0] // CMP_STRIDE
    for l in range(CMP_STRIDE):
        oc_ref[0, l] = regrouped[l * per:(l + 1) * per, :]
    for r0 in range(0, n_qg, row_chunk):
        r1 = min(r0 + row_chunk, n_qg)
        oq_ref[r0:r1, :] = _dot_nt(wt_ref[r0:r1, :], hn)
    vt = _dot_nt(wt_ref[n_qg:, :], hn).astype(BF16)
    ones = jnp.ones((BF16_ROWS, V_TILE), BF16)
    for jj in range(ov_ref.shape[1]):
        for g in range(N_KV):
            for half in range(2):
                r0 = g * LANES + half * HEAD_DIM
                ov_ref[0, jj, g, half * V_ROWS:half * V_ROWS + HEAD_DIM, :] = (
                    vt[r0:r0 + HEAD_DIM, jj * V_TILE:(jj + 1) * V_TILE])
                ov_ref[0, jj, g, half * V_ROWS + HEAD_DIM:(half + 1) * V_ROWS, :] = ones


def _in_proj(x, g, wn, wt, batch, seq):
    t, d = x.shape
    tm = min(512, seq)
    n_qg = N_HEADS * HEAD_DIM + N_KV * GATE_ROWS
    assert seq % tm == 0 and tm % V_TILE == 0 and wt.shape[0] == n_qg + N_KV * LANES
    per_seq = seq // tm
    nk = N_KV * LANES
    ncv = wn.shape[1] - nk
    per = tm // CMP_STRIDE
    rows = np.arange(tm)
    perm = np.zeros((tm, tm), np.float32)
    perm[(rows % CMP_STRIDE) * per + rows // CMP_STRIDE, rows] = 1.0
    kern = functools.partial(_in_proj_kernel, n_qg=n_qg, row_chunk=512)
    return pl.pallas_call(
        kern,
        out_shape=(jax.ShapeDtypeStruct((t, nk), F32),
                   jax.ShapeDtypeStruct((t // tm, CMP_STRIDE, per, ncv), BF16),
                   jax.ShapeDtypeStruct((n_qg, t), F32),
                   jax.ShapeDtypeStruct((batch, seq // V_TILE, N_KV, 2 * V_ROWS, V_TILE), BF16)),
        grid=(t // tm,),
        in_specs=[
            pl.BlockSpec((tm, d), lambda i: (i, 0)),
            pl.BlockSpec((1, d), lambda i: (0, 0)),
            pl.BlockSpec(wn.shape, lambda i: (0, 0)),
            pl.BlockSpec(wt.shape, lambda i: (0, 0)),
            pl.BlockSpec((tm, tm), lambda i: (0, 0)),
        ],
        out_specs=(pl.BlockSpec((tm, nk), lambda i: (i, 0)),
                   pl.BlockSpec((1, CMP_STRIDE, per, ncv), lambda i: (i, 0, 0, 0)),
                   pl.BlockSpec((n_qg, tm), lambda i: (0, i)),
                   pl.BlockSpec((1, tm // V_TILE, N_KV, 2 * V_ROWS, V_TILE),
                                lambda i: (i // per_seq, i % per_seq, 0, 0, 0))),
        compiler_params=_params("parallel"),
        name="nsa_in_proj",
    )(x, g.reshape(1, d), wn, wt, jnp.asarray(perm, BF16))


def _out_proj_kernel(a_ref, x_ref, w_ref, o_ref):
    o_ref[...] = x_ref[...] + _dot(a_ref[...], w_ref[...])


def _out_proj(a, x, w):
    t, d = x.shape
    tm = min(512, t)
    return pl.pallas_call(
        _out_proj_kernel,
        out_shape=jax.ShapeDtypeStruct((t, d), F32),
        grid=(t // tm,),
        in_specs=[
            pl.BlockSpec((tm, a.shape[1]), lambda i: (i, 0)),
            pl.BlockSpec((tm, d), lambda i: (i, 0)),
            pl.BlockSpec(w.shape, lambda i: (0, 0)),
        ],
        out_specs=pl.BlockSpec((tm, d), lambda i: (i, 0)),
        compiler_params=_params("parallel"),
        name="nsa_out_proj",
    )(a, x, w.astype(BF16))


def _kprep_kernel(k_ref, gain_ref, bd_ref, c_ref, s1_ref, s2_ref, kaug_ref, *, ts, seq):
    i = pl.program_id(0)
    bd = bd_ref[...]
    gain = gain_ref[...]
    c, s1, s2 = c_ref[...], s1_ref[...], s2_ref[...]
    tpos = (i * ts) % seq + lax.broadcasted_iota(jnp.int32, (ts, LANES), 0)
    lane = lax.broadcasted_iota(jnp.int32, (ts, LANES), 1)
    onehot = jnp.where(tpos // SEL_LEN == lane, 1.0, 0.0).astype(BF16)
    for g in range(N_KV):
        xb = k_ref[:, g * LANES:(g + 1) * LANES]
        xr = _rope(_head_rms(xb, bd, gain), c, s1, s2)
        kaug_ref[:, 2 * g * LANES:(2 * g + 1) * LANES] = xr.astype(BF16)
        kaug_ref[:, (2 * g + 1) * LANES:(2 * g + 2) * LANES] = onehot


def _kprep(proj, gain, bd, tabs, seq):
    t = proj.shape[0]
    ts = min(512, seq)
    assert seq % ts == 0
    nk = N_KV * LANES
    kern = functools.partial(_kprep_kernel, ts=ts, seq=seq)
    tab = lambda: pl.BlockSpec((ts, LANES), lambda i: (i % (seq // ts), 0))
    return pl.pallas_call(
        kern,
        out_shape=jax.ShapeDtypeStruct((t, 2 * nk), BF16),
        grid=(t // ts,),
        in_specs=[
            pl.BlockSpec((ts, nk), lambda i: (i, 0)),
            pl.BlockSpec((1, LANES), lambda i: (0, 0)),
            pl.BlockSpec((LANES, LANES), lambda i: (0, 0)),
            tab(), tab(), tab(),
        ],
        out_specs=pl.BlockSpec((ts, 2 * nk), lambda i: (i, 0)),
        compiler_params=_params("parallel"),
        name="nsa_key_prep",
    )(proj, gain, bd, *tabs)


def _compress_kernel(x_ref, w1k_ref, w1v_ref, ck_ref, cv_ref, w2k_ref, w2v_ref, gain_ref, o_ref, ot_ref,
                     acck_ref, accv_ref):
    l = pl.program_id(1)
    nc = acck_ref.shape[0]
    hid = w2k_ref.shape[0]
    hk = w1k_ref.shape[1]

    @pl.when(l == 0)
    def _():
        acck_ref[...] = jnp.zeros_like(acck_ref)
        accv_ref[...] = jnp.zeros_like(accv_ref)

    x = x_ref[0].reshape(nc, 2 * hk)
    acck_ref[...] += _dot(x[:, 0:hk], w1k_ref[0])
    accv_ref[...] += _dot(x[:, hk:], w1v_ref[0])

    @pl.when(l == pl.num_programs(1) - 1)
    def _():
        for g in range(N_KV):
            def hidden(acc_ref, c_ref):
                first = acc_ref[:, g * hid:(g + 1) * hid] + c_ref[0:1, :]
                second = acc_ref[:, (N_KV + g) * hid:(N_KV + g + 1) * hid] + c_ref[1:2, :]
                pre = first + pltpu.roll(second, nc - 1, 0)
                return (pre * _sigmoid(pre)).astype(BF16)

            kv = _dot(hidden(acck_ref, ck_ref), w2k_ref[...]) + _dot(hidden(accv_ref, cv_ref), w2v_ref[...])
            is_k = lax.broadcasted_iota(jnp.int32, kv.shape, 1) < HEAD_DIM
            ss = jnp.sum(jnp.where(is_k, kv * kv, 0.0), axis=-1, keepdims=True)
            kn = kv * lax.rsqrt(ss * (1.0 / HEAD_DIM) + EPS) * gain_ref[...]
            out = jnp.where(is_k, kn, kv)
            o_ref[g] = out.astype(BF16)
            ot_ref[g] = out.T.astype(BF16)


def _compress(kcvc, pe_k, pe_v, ck_w1, ck_w2, cv_w1, cv_w2, kc_g, batch, seq):
    nc = seq // CMP_STRIDE
    hid = ck_w1.shape[1]
    hk = N_KV * HEAD_DIM
    tiles, _, per, width = kcvc.shape
    assert CMP_LEN == 2 * CMP_STRIDE and width == 2 * hk and tiles * per == batch * nc
    x5 = kcvc.reshape(batch, tiles // batch, CMP_STRIDE, per, width)
    pad = lambda w, left: jnp.pad(w, ((0, 0), (HEAD_DIM, 0) if left else (0, HEAD_DIM))).astype(BF16)
    gain = jnp.concatenate([kc_g, jnp.ones((HEAD_DIM,), F32)]).reshape(1, LANES)

    def first_layer(pe, w1):
        w = w1.reshape(2, CMP_STRIDE, HEAD_DIM, hid)
        rows = lambda a, k: jnp.pad(a, ((0, 0), (k * HEAD_DIM, (N_KV - 1 - k) * HEAD_DIM), (0, 0)))
        bd = jnp.concatenate([rows(w[s], k) for s in range(2) for k in range(N_KV)], axis=2)
        const = jnp.einsum("sld,sldh->sh", pe.reshape(2, CMP_STRIDE, HEAD_DIM), w, precision=lax.Precision.HIGHEST)
        return bd.astype(BF16), const

    w1k, const_k = first_layer(pe_k, ck_w1)
    w1v, const_v = first_layer(pe_v, cv_w1)
    wide = 2 * N_KV * hid
    step_w = lambda: pl.BlockSpec((1, hk, wide), lambda b, l: (l, 0, 0))
    full = lambda shape: pl.BlockSpec(shape, lambda b, l: (0,) * len(shape))
    return pl.pallas_call(
        _compress_kernel,
        out_shape=(jax.ShapeDtypeStruct((batch * N_KV, nc, LANES), BF16),
                   jax.ShapeDtypeStruct((batch * N_KV, LANES, nc), BF16)),
        grid=(batch, CMP_STRIDE),
        in_specs=[pl.BlockSpec((1, tiles // batch, None, per, width), lambda b, l: (b, 0, l, 0, 0)),
                  step_w(), step_w(), full((2, hid)), full((2, hid)),
                  full((hid, LANES)), full((hid, LANES)), full((1, LANES))],
        out_specs=(pl.BlockSpec((N_KV, nc, LANES), lambda b, l: (b, 0, 0)),
                   pl.BlockSpec((N_KV, LANES, nc), lambda b, l: (b, 0, 0))),
        scratch_shapes=[pltpu.VMEM((nc, wide), F32), pltpu.VMEM((nc, wide), F32)],
        compiler_params=_params("parallel", "arbitrary"),
        name="nsa_compress",
    )(x5, w1k, w1v, const_k, const_v, pad(ck_w2, False), pad(cv_w2, True), gain)


def _attn_kernel(qt_ref, gate_ref, kaug_ref, vt_ref, kvc_ref, kvct_ref, cmapt_ref, qg_ref, cos_ref, sin_ref,
                 o_ref,
                 qaug_ref, qwin_ref, oc_ref, ms_ref, ls_ref, accs_ref, mw_ref, lw_ref, accw_ref,
                 sa_ref, sb_ref, wdiag_ref, wold_ref, imp_ref,
                 *, tq, tk, k_top, n_cmp_var):
    qi = pl.program_id(2)
    t0 = qi * tq
    cols = GROUP * tq
    half = ROT_DIM // 2

    gain = qg_ref[...]
    cos, sin = cos_ref[...], sin_ref[...]
    nope, rope = [], []
    for r in range(GROUP):
        x = qt_ref[r * HEAD_DIM:(r + 1) * HEAD_DIM, :]
        ss = jnp.sum(x * x, axis=0, keepdims=True)
        xn = x * lax.rsqrt(ss * (1.0 / HEAD_DIM) + EPS) * gain
        x1, x2 = xn[0:half], xn[half:ROT_DIM]
        nope.append(xn)
        rope.append(jnp.concatenate([x1 * cos - x2 * sin, x2 * cos + x1 * sin, xn[ROT_DIM:]], axis=0))
    zeros = jnp.zeros((HEAD_DIM, cols), BF16)
    q_nope = jnp.concatenate([jnp.concatenate(nope, axis=1).astype(BF16), zeros], axis=0)
    q_rope = jnp.concatenate(rope, axis=1).astype(BF16)
    qwin_ref[...] = jnp.concatenate([zeros, q_rope], axis=0)

    tpos = t0 + lax.broadcasted_iota(jnp.int32, (1, tq), 1)

    def per_head(x):
        return jnp.concatenate([x] * GROUP, axis=1)

    def fold_tile(s, keep, k0, v_rows, m_ref, l_ref, acc_ref):
        if keep is not None:
            s = jnp.where(per_head(keep), s, NEG)
        m_prev = m_ref[...]
        m_new = jnp.maximum(m_prev, jnp.max(s, axis=0, keepdims=True))
        p = jnp.exp2(s - m_new)
        alpha = jnp.exp2(m_prev - m_new)
        pb = p.astype(BF16)
        v_tile0 = k0 // V_TILE
        pv = _dot(vt_ref[0, v_tile0, 0, v_rows, :], pb[0:V_TILE])
        for c in range(1, s.shape[0] // V_TILE):
            pv = pv + _dot(vt_ref[0, v_tile0 + c, 0, v_rows, :], pb[c * V_TILE:(c + 1) * V_TILE])
        l_ref[...] = alpha * l_ref[...] + pv[HEAD_DIM:HEAD_DIM + 1]
        acc_ref[...] = alpha * acc_ref[...] + pv[0:HEAD_DIM]
        m_ref[...] = m_new

    def back(k0, size):
        return tpos - (k0 + lax.broadcasted_iota(jnp.int32, (size, 1), 0))

    def sel_scores(k0):
        return _dot(kaug_ref[pl.ds(k0, tk), :], qaug_ref[...])

    def win_scores(k0, size):
        return _dot(kaug_ref[pl.ds(k0, size), 0:LANES], qwin_ref[...])

    for m_ref, l_ref, acc_ref in ((ms_ref, ls_ref, accs_ref), (mw_ref, lw_ref, accw_ref)):
        m_ref[...] = jnp.full_like(m_ref, NEG)
        l_ref[...] = jnp.zeros_like(l_ref)
        acc_ref[...] = jnp.zeros_like(acc_ref)

    t0a = pl.multiple_of(t0, tq)
    k_old = pl.multiple_of(jnp.maximum(t0 - WINDOW, 0), tq)
    wdiag_ref[...] = win_scores(t0a, tq)
    wold_ref[...] = win_scores(k_old, WINDOW)
    qaug_ref[0:LANES, :] = jnp.concatenate([q_rope, zeros], axis=0)
    sa_ref[...] = _dot(kaug_ref[0:tk, 0:LANES], qaug_ref[0:LANES, :])
    win_state = (slice(V_ROWS, 2 * V_ROWS), mw_ref, lw_ref, accw_ref)
    fold_tile(wdiag_ref[...], back(t0a, tq) >= 0, t0a, *win_state)
    b_old = back(k_old, WINDOW)
    fold_tile(wold_ref[...], (b_old < WINDOW) & (b_old > tpos - t0), k_old, *win_state)

    nc = kvc_ref.shape[1]

    def cmp_branch(rows):
        kvc = kvc_ref[0, 0:rows, :]
        cmp_end = lax.broadcasted_iota(jnp.int32, (rows, 1), 0) * CMP_STRIDE + (CMP_LEN - 1)
        s_c = jnp.where(per_head(cmp_end <= tpos), _dot(kvc, q_nope), NEG)
        e = jnp.exp2(s_c - jnp.maximum(jnp.max(s_c, axis=0, keepdims=True), 0.1 * NEG))
        p_c = e * (1.0 / jnp.maximum(jnp.sum(e, axis=0, keepdims=True), 1e-30))
        oc_ref[...] = _dot(kvct_ref[0, HEAD_DIM:, 0:rows], p_c.astype(BF16))
        p_sum = p_c[:, 0:tq]
        for r in range(1, GROUP):
            p_sum = p_sum + p_c[:, r * tq:(r + 1) * tq]
        hi, lo = _split_bf16(p_sum)
        imp_ref[...] = _dot(cmapt_ref[:, 0:rows], hi) + _dot(cmapt_ref[:, 0:rows], lo)

    variant = (t0 + tq - 1) // (CMP_STRIDE * (nc // n_cmp_var))
    for v in range(n_cmp_var):
        pl.when(variant == v)(functools.partial(cmp_branch, (v + 1) * (nc // n_cmp_var)))
    imp = imp_ref[...]

    blk = lax.broadcasted_iota(jnp.int32, (LANES, tq), 0)
    cur = tpos // SEL_LEN
    forced = (blk == 0) | (blk == cur) | (blk == cur - 1)
    taken = -3e38
    imp = jnp.where(blk > cur, -1.0, jnp.where(forced, taken, imp))
    blk_f = blk.astype(F32)
    for _ in range(max(k_top - 3, 0)):
        mx = jnp.max(imp, axis=0, keepdims=True)
        idx = jnp.min(jnp.where(imp == mx, blk_f, float(LANES)), axis=0, keepdims=True)
        imp = jnp.where(blk_f == idx, taken, imp)
    bias = jnp.where(imp == taken, 0.0, NEG)
    qaug_ref[LANES:, :] = per_head(bias.astype(BF16))
    blocks0 = tk // SEL_LEN
    bias0 = jnp.broadcast_to(bias[0:blocks0, None, :], (blocks0, SEL_LEN, tq)).reshape(tk, tq)
    sa_ref[...] = sa_ref[...] + per_head(bias0)


    n = t0 // tk
    sel_state = (slice(0, V_ROWS), ms_ref, ls_ref, accs_ref)

    def tile_pair(k0):
        sb_ref[...] = sel_scores(k0 + tk)
        fold_tile(sa_ref[...], None, k0, *sel_state)
        sa_ref[...] = sel_scores(k0 + 2 * tk)
        fold_tile(sb_ref[...], None, k0 + tk, *sel_state)

    def quad_body(i, carry):
        k0 = pl.multiple_of(4 * i * tk, tk)
        tile_pair(k0)
        tile_pair(k0 + 2 * tk)
        return carry

    n_pairs = n // 2
    lax.fori_loop(0, n_pairs // 2, quad_body, 0)

    @pl.when(n_pairs % 2 == 1)
    def _():
        tile_pair(pl.multiple_of((n_pairs - 1) * 2 * tk, tk))

    k_diag = pl.multiple_of(n * tk, tk)

    def diagonal_fold(buf_ref):
        if tk == tq:
            fold_tile(buf_ref[...], back(k_diag, tk) >= 0, k_diag, *sel_state)
            return
        pl.when(t0 % tk == 0)(lambda: fold_tile(buf_ref[0:tq, :], back(k_diag, tq) >= 0, k_diag, *sel_state))
        pl.when(t0 % tk != 0)(lambda: fold_tile(buf_ref[...], back(k_diag, tk) >= 0, k_diag, *sel_state))

    @pl.when(n % 2 == 1)
    def _():
        sb_ref[...] = sel_scores(k_diag)
        fold_tile(sa_ref[...], None, k_diag - tk, *sel_state)
        diagonal_fold(sb_ref)

    @pl.when(n % 2 == 0)
    def _():
        diagonal_fold(sa_ref)

    o_s = accs_ref[...] / ls_ref[...]
    o_w = accw_ref[...] / lw_ref[...]
    o_c = oc_ref[...]
    sig = _sigmoid(gate_ref[...])
    heads = []
    for r in range(GROUP):
        cs = slice(r * tq, (r + 1) * tq)
        g_c, g_s, g_w = (sig[3 * r + b:3 * r + b + 1, :] for b in range(3))
        heads.append(g_c * o_c[:, cs] + g_s * o_s[:, cs] + g_w * o_w[:, cs])
    o_ref[...] = jnp.concatenate(heads, axis=0).T.astype(o_ref.dtype)


def _attention(qgt, kaug, vt, kvc, kvct, cmapt, qg, cos, sin, batch, seq):
    t = qgt.shape[1]
    tq = min(Q_TILE, seq)
    tk = min(512, seq)
    nq = seq // tq
    nc = kvc.shape[1]
    assert seq % tk == 0 and tk % tq == 0 and WINDOW % tq == 0 and tq % V_TILE == 0 and seq >= WINDOW
    cols = GROUP * tq
    hq = GROUP * HEAD_DIM
    k_top = min(SEL_TOPK, seq // SEL_LEN)
    n_cmp_var = max(1, min(4, nc // LANES))
    assert k_top >= 3 and nc % n_cmp_var == 0
    kern = functools.partial(_attn_kernel, tq=tq, tk=tk, k_top=k_top, n_cmp_var=n_cmp_var)
    tab = lambda: pl.BlockSpec((ROT_DIM // 2, tq), lambda b, g, qi: (0, qi))
    stat = lambda: pltpu.VMEM((1, cols), F32)
    acc = lambda: pltpu.VMEM((HEAD_DIM, cols), F32)
    return pl.pallas_call(
        kern,
        out_shape=jax.ShapeDtypeStruct((t, N_HEADS * HEAD_DIM), BF16),
        grid=(batch, N_KV, nq),
        in_specs=[
            pl.BlockSpec((hq, tq), lambda b, g, qi: (g, b * nq + qi)),
            pl.BlockSpec((GATE_ROWS, tq), lambda b, g, qi: (N_HEADS * HEAD_DIM // GATE_ROWS + g, b * nq + qi)),
            pl.BlockSpec((seq, 2 * LANES), lambda b, g, qi: (b, g)),
            pl.BlockSpec((1, seq // V_TILE, 1, 2 * V_ROWS, V_TILE), lambda b, g, qi: (b, 0, g, 0, 0)),
            pl.BlockSpec((1, nc, LANES), lambda b, g, qi: (b * N_KV + g, 0, 0)),
            pl.BlockSpec((1, LANES, nc), lambda b, g, qi: (b * N_KV + g, 0, 0)),
            pl.BlockSpec((LANES, nc), lambda b, g, qi: (0, 0)),
            pl.BlockSpec((HEAD_DIM, tq), lambda b, g, qi: (0, 0)),
            tab(), tab(),
        ],
        out_specs=pl.BlockSpec((tq, hq), lambda b, g, qi: (b * nq + qi, g)),
        scratch_shapes=[
            pltpu.VMEM((2 * LANES, cols), BF16), pltpu.VMEM((LANES, cols), BF16), acc(),
            stat(), stat(), acc(), stat(), stat(), acc(),
            pltpu.VMEM((tk, cols), F32), pltpu.VMEM((tk, cols), F32), pltpu.VMEM((tq, cols), F32), pltpu.VMEM((WINDOW, cols), F32),
            pltpu.VMEM((LANES, tq), F32),
        ],
        compiler_params=_params("parallel", "parallel", "arbitrary"),
        name="nsa_attention",
    )(qgt, qgt, kaug, vt, kvc, kvct, cmapt, qg, cos, sin)


def _rope_angles(seq):
    half = ROT_DIM // 2
    inv_freq = ROPE_THETA ** (-jnp.arange(half, dtype=F32) * (2.0 / ROT_DIM))
    ang = jnp.arange(seq, dtype=F32)[:, None] * inv_freq[None, :]
    return jnp.cos(ang), jnp.sin(ang)


def _rope_tables(cos, sin):
    seq, half = cos.shape
    zeros = jnp.zeros((seq, HEAD_DIM - ROT_DIM), F32)
    zh = jnp.zeros((seq, half), F32)
    c = jnp.concatenate([cos, cos, zeros + 1.0], axis=1)
    s1 = jnp.concatenate([-sin, zh, zeros], axis=1)
    s2 = jnp.concatenate([zh, sin, zeros], axis=1)
    reps = LANES // HEAD_DIM
    return tuple(jnp.tile(a, (1, reps)) for a in (c, s1, s2))


def _cmp_to_sel_t(nc, n_sel):
    start_c = np.arange(nc)[None, :] * CMP_STRIDE
    start_s = np.arange(LANES)[:, None] * SEL_LEN
    ov = np.minimum(start_c + CMP_LEN, start_s + SEL_LEN) - np.maximum(start_c, start_s)
    m = np.maximum(ov, 0).astype(np.float32) / CMP_LEN
    m[n_sel:, :] = 0.0
    m[:, nc - 1:] = 0.0
    return jnp.asarray(m, BF16)


def _nsa_in_weights(w_in):
    d = w_in.shape[0]
    hq, hk = N_HEADS * HEAD_DIM, N_KV * HEAD_DIM
    offs = np.cumsum([0, hq] + [hk] * 6)
    q, kc, vc, ks, vs, kw, vw = (w_in[:, offs[n]:offs[n + 1]] for n in range(7))
    gl = w_in[:, offs[7]:]
    per_group = lambda a, b: jnp.stack([a.reshape(d, N_KV, HEAD_DIM), b.reshape(d, N_KV, HEAD_DIM)],
                                       axis=2).reshape(d, N_KV * LANES)
    gates = jnp.pad(gl.reshape(d, N_KV, 3 * GROUP), ((0, 0), (0, 0), (0, GATE_ROWS - 3 * GROUP)))
    wn = jnp.concatenate([per_group(ks, kw), kc, vc], axis=1).astype(BF16)
    wt = jnp.concatenate([q, gates.reshape(d, N_KV * GATE_ROWS), per_group(vs, vw)], axis=1).T.astype(BF16)
    return wn, wt


def _nsa_layer(x, g, w_in, q_g, kc_g, ks_g, kw_g, pe_k, pe_v, ck_w1, ck_w2, cv_w1, cv_w2, w_out, batch, seq):
    nc = seq // CMP_STRIDE
    n_sel = seq // SEL_LEN
    assert n_sel <= LANES and seq % CMP_STRIDE == 0 and 3 * GROUP <= GATE_ROWS
    wn, wt = _nsa_in_weights(w_in)
    proj, kcvc, qgt, vt = _in_proj(x, g, wn, wt, batch, seq)
    cos, sin = _rope_angles(seq)
    seg = np.arange(LANES) // HEAD_DIM
    bd = jnp.asarray(seg[:, None] == seg[None, :], BF16)
    kaug = _kprep(proj, jnp.concatenate([ks_g, kw_g]).reshape(1, LANES), bd, _rope_tables(cos, sin), seq)
    kvc, kvct = _compress(kcvc, pe_k, pe_v, ck_w1, ck_w2, cv_w1, cv_w2, kc_g, batch, seq)
    tq = min(Q_TILE, seq)
    qg = jnp.broadcast_to((q_g * float(HEAD_DIM ** -0.5 * np.log2(np.e)))[:, None], (HEAD_DIM, tq))
    attn = _attention(qgt, kaug, vt, kvc, kvct, _cmp_to_sel_t(nc, n_sel), qg, cos.T, sin.T, batch, seq)
    return _out_proj(attn, x, w_out)


def kernel(x, mix_norm_g, ffn_norm_g, conv_w_pw1, conv_b_pw1, conv_w_dw, conv_b_dw, conv_ln_g, conv_ln_b, conv_w_pw2, conv_b_pw2, nsa_w_in, nsa_q_norm, nsa_kc_norm, nsa_ks_norm, nsa_kw_norm, nsa_pe_k, nsa_pe_v, nsa_ck_w1, nsa_ck_w2, nsa_cv_w1, nsa_cv_w2, nsa_w_out, ffn_w_up, ffn_w_dw, ffn_b_dw, ffn_w_down):
    batch, seq, d = x.shape
    depth = mix_norm_g.shape[0]
    n_mixers = 2
    h = x.reshape(batch * seq, d)
    w_up_all, w_down_all = ffn_w_up.astype(BF16), ffn_w_down.astype(BF16)
    for i in range(depth):
        j = i // n_mixers
        if i % n_mixers == 0:
            u = _conformer_glu(h, mix_norm_g[i], conv_w_pw1[j], conv_b_pw1[j])
            h = _conformer_conv_out(u, h, conv_w_dw[j], conv_b_dw[j], conv_ln_g[j], conv_ln_b[j],
                                    conv_w_pw2[j], conv_b_pw2[j], seq)
        else:
            h = _nsa_layer(h, mix_norm_g[i], nsa_w_in[j], nsa_q_norm[j], nsa_kc_norm[j], nsa_ks_norm[j],
                           nsa_kw_norm[j], nsa_pe_k[j], nsa_pe_v[j], nsa_ck_w1[j], nsa_ck_w2[j], nsa_cv_w1[j],
                           nsa_cv_w2[j], nsa_w_out[j], batch, seq)
        h = _conv_ffn(h, ffn_norm_g[i], w_up_all, ffn_w_dw[i], ffn_b_dw[i], w_down_all, i, seq)
    return h.reshape(batch, seq, d)
```

```python
import functools

import numpy as np
import jax
import jax.numpy as jnp
from jax import lax
from jax.experimental import pallas as pl
from jax.experimental.pallas import tpu as pltpu

N_HEADS = 16
HEAD_DIM = 64
N_KV = 4
GROUP = N_HEADS // N_KV
ROT_DIM = HEAD_DIM // 4
ROPE_THETA = 500000.0
CMP_LEN = 32
CMP_STRIDE = 16
SEL_LEN = 64
SEL_TOPK = 16
WINDOW = 512
EPS = 1e-6
NEG = -1e30

LANES = 128
SUBLANES = 8
BF16_ROWS = 16
V_TILE = 256
Q_TILE = 512
V_ROWS = HEAD_DIM + BF16_ROWS
GATE_ROWS = 16
VMEM_LIMIT = 48 * 1024 * 1024

F32 = jnp.float32
BF16 = jnp.bfloat16
NT_DIMS = (((1,), (1,)), ((), ()))


def _params(*sem):
    return pltpu.CompilerParams(dimension_semantics=sem, vmem_limit_bytes=VMEM_LIMIT)


def _dot(a, b):
    return jnp.dot(a, b, preferred_element_type=F32)


def _dot_nt(a, b):
    return lax.dot_general(a, b, NT_DIMS, preferred_element_type=F32)


def _split_bf16(x):
    hi = x.astype(BF16)
    lo = (x - hi.astype(F32)).astype(BF16)
    return hi, lo


def _rms_rows(x, g):
    ms = jnp.mean(x * x, axis=-1, keepdims=True)
    return x * lax.rsqrt(ms + EPS) * g


def _sigmoid(x):
    return 1.0 / (1.0 + jnp.exp(-x))


def _head_rms(xb, bd, gain):
    hi, lo = _split_bf16(xb * xb)
    ss = _dot(hi, bd) + _dot(lo, bd)
    return xb * lax.rsqrt(ss * (1.0 / HEAD_DIM) + EPS) * gain


def _rope(xb, c, s1, s2):
    half = ROT_DIM // 2
    return xb * c + pltpu.roll(xb, LANES - half, 1) * s1 + pltpu.roll(xb, half, 1) * s2


def _ffn_kernel(x_ref, halo_ref, g_ref, wup_ref, wdw_ref, bdw_ref, wd_ref, o_ref,
                hn_ref, a0_ref, v0_ref, a1_ref, v1_ref, *, tm, tiles_per_seq, nch):
    i = pl.program_id(0)
    hl = BF16_ROWS
    g = g_ref[...]
    hn_ref[hl:, :] = _rms_rows(x_ref[...], g).astype(BF16)
    first = (i % tiles_per_seq) == 0
    hn_ref[0:hl, :] = jnp.where(first, 0.0, _rms_rows(halo_ref[...], g)).astype(BF16)
    o_ref[...] = x_ref[...]

    tf = a0_ref.shape[1]

    def cols(c):
        start = c * tf
        return pl.ds(start if isinstance(start, int) else pl.multiple_of(start, tf), tf)

    def up(c, a_ref, v_ref):
        a_ref[...] = _dot(hn_ref[...], wup_ref[:, cols(c)])
        v_ref[...] = _dot(hn_ref[hl:, :], wup_ref[:, cols(nch + c)])

    def down(c, a_ref, v_ref):
        w = wdw_ref[c]
        cv = (w[0:1] * a_ref[hl - 2:hl - 2 + tm, :] + w[1:2] * a_ref[hl - 1:hl - 1 + tm, :]
              + w[2:3] * a_ref[hl:hl + tm, :] + bdw_ref[c])
        act = (cv * _sigmoid(cv) * v_ref[...]).astype(BF16)
        o_ref[...] += _dot(act, wd_ref[c])

    buf0, buf1 = (a0_ref, v0_ref), (a1_ref, v1_ref)
    up(0, *buf0)

    def chunk_pair(c):
        up(c + 1, *buf1)
        down(c, *buf0)
        up(c + 2, *buf0)
        down(c + 1, *buf1)

    for p in range((nch - 1) // 2):
        chunk_pair(2 * p)
    if nch % 2 == 1:
        down(nch - 1, *buf0)
    else:
        up(nch - 1, *buf1)
        down(nch - 2, *buf0)
        down(nch - 1, *buf1)


def _conv_ffn(x, g, w_up_all, w_dw, b_dw, w_down_all, layer, seq):
    t, d = x.shape
    dff = w_down_all.shape[1]
    tm = min(512, seq)
    tf = 256
    assert seq % tm == 0 and dff % tf == 0 and w_dw.shape[0] == 3
    nch = dff // tf
    hl = BF16_ROWS
    kern = functools.partial(_ffn_kernel, tm=tm, tiles_per_seq=seq // tm, nch=nch)
    wup = w_up_all
    wdn = w_down_all.reshape(w_down_all.shape[0], nch, tf, d)
    wdw = w_dw.reshape(3, nch, tf).transpose(1, 0, 2)
    bdw = b_dw.reshape(nch, 1, tf)
    resident = lambda a: pl.BlockSpec(a.shape, lambda i: (0,) * a.ndim, pipeline_mode=pl.Buffered(1))
    of_layer = lambda a: pl.BlockSpec((None,) + a.shape[1:], lambda i: (layer,) + (0,) * (a.ndim - 1),
                                      pipeline_mode=pl.Buffered(1))
    return pl.pallas_call(
        kern,
        out_shape=jax.ShapeDtypeStruct((t, d), F32),
        grid=(t // tm,),
        in_specs=[
            pl.BlockSpec((tm, d), lambda i: (i, 0)),
            pl.BlockSpec((hl, d), lambda i: (jnp.maximum(i * (tm // hl) - 1, 0), 0)),
            pl.BlockSpec((1, d), lambda i: (0, 0)),
            of_layer(wup), resident(wdw), resident(bdw), of_layer(wdn),
        ],
        out_specs=pl.BlockSpec((tm, d), lambda i: (i, 0)),
        scratch_shapes=[
            pltpu.VMEM((tm + hl, d), BF16),
            pltpu.VMEM((tm + hl, tf), F32), pltpu.VMEM((tm, tf), F32),
            pltpu.VMEM((tm + hl, tf), F32), pltpu.VMEM((tm, tf), F32),
        ],
        compiler_params=_params("parallel"),
        name="conv_ffn",
    )(x, x, g.reshape(1, d), wup, wdw, bdw, wdn)


def _glu_kernel(x_ref, g_ref, w_ref, b_ref, o_ref, hn_ref, *, tn):
    hn_ref[...] = _rms_rows(x_ref[...], g_ref[...]).astype(BF16)
    d = o_ref.shape[1]
    for j in range(d // tn):
        cs, gs = slice(j * tn, (j + 1) * tn), slice(d + j * tn, d + (j + 1) * tn)
        a = _dot(hn_ref[...], w_ref[:, cs]) + b_ref[:, cs]
        gate = _dot(hn_ref[...], w_ref[:, gs]) + b_ref[:, gs]
        o_ref[:, cs] = a * _sigmoid(gate)


def _conformer_glu(x, g, w_pw1, b_pw1):
    t, d = x.shape
    tm = min(1024, t)
    w = w_pw1.astype(BF16)
    b = b_pw1.reshape(1, 2 * d)
    once = pl.Buffered(1)
    return pl.pallas_call(
        functools.partial(_glu_kernel, tn=256),
        out_shape=jax.ShapeDtypeStruct((t, d), F32),
        grid=(t // tm,),
        in_specs=[
            pl.BlockSpec((tm, d), lambda i: (i, 0)),
            pl.BlockSpec((1, d), lambda i: (0, 0)),
            pl.BlockSpec((d, 2 * d), lambda i: (0, 0), pipeline_mode=once),
            pl.BlockSpec((1, 2 * d), lambda i: (0, 0), pipeline_mode=once),
        ],
        out_specs=pl.BlockSpec((tm, d), lambda i: (i, 0)),
        scratch_shapes=[pltpu.VMEM((tm, d), BF16)],
        compiler_params=_params("parallel"),
        name="conformer_glu",
    )(x, g.reshape(1, d), w, b)


def _dwconv_kernel(u_ref, halo_ref, x_ref, wdw_ref, bdw_ref, lng_ref, lnb_ref, w2_ref, b2_ref, o_ref,
                   ext_ref, cv_ref, slab_ref, *, tm, halo, width, tiles_per_seq, row_chunk, col_chunk):
    i = pl.program_id(0)
    d = u_ref.shape[1]
    first = (i % tiles_per_seq) == 0
    ext_ref[0:halo, :] = jnp.where(first, 0.0, halo_ref[...])
    ext_ref[halo:, :] = u_ref[...]
    off = halo - (width - 1)
    for r0 in range(0, tm, row_chunk):
        for c0 in range(0, d, col_chunk):
            cs = slice(c0, c0 + col_chunk)
            acc = jnp.broadcast_to(bdw_ref[:, cs], (row_chunk, col_chunk))
            for r in range(min(SUBLANES, width)):
                taps = range(r, width, SUBLANES)
                lo = r0 + off + r
                rows = row_chunk + (len(taps) - 1) * SUBLANES
                slab_ref[r, 0:rows, :] = ext_ref[lo:lo + rows, cs]
                for q, k in enumerate(taps):
                    acc = acc + wdw_ref[k:k + 1, cs] * slab_ref[r, q * SUBLANES:q * SUBLANES + row_chunk, :]
            cv_ref[r0:r0 + row_chunk, cs] = acc
    u = cv_ref[...]
    mu = jnp.mean(u, axis=-1, keepdims=True)
    uc = u - mu
    var = jnp.mean(uc * uc, axis=-1, keepdims=True)
    y = uc * lax.rsqrt(var + EPS) * lng_ref[...] + lnb_ref[...]
    s = (y * _sigmoid(y)).astype(BF16)
    o_ref[...] = x_ref[...] + _dot(s, w2_ref[...]) + b2_ref[...]


def _conformer_conv_out(u, x, w_dw, b_dw, ln_g, ln_b, w_pw2, b_pw2, seq):
    t, d = x.shape
    width = w_dw.shape[0]
    halo = 32
    assert width - 1 <= halo
    tm = min(256, seq)
    assert seq % tm == 0 and tm % halo == 0
    row_chunk, col_chunk = 64, 256
    kern = functools.partial(_dwconv_kernel, tm=tm, halo=halo, width=width, tiles_per_seq=seq // tm,
                             row_chunk=row_chunk, col_chunk=col_chunk)
    slab_rows = row_chunk + (width - 1) // SUBLANES * SUBLANES
    vec = lambda: pl.BlockSpec((1, d), lambda i: (0, 0))
    return pl.pallas_call(
        kern,
        out_shape=jax.ShapeDtypeStruct((t, d), F32),
        grid=(t // tm,),
        in_specs=[
            pl.BlockSpec((tm, d), lambda i: (i, 0)),
            pl.BlockSpec((halo, d), lambda i: (jnp.maximum(i * (tm // halo) - 1, 0), 0)),
            pl.BlockSpec((tm, d), lambda i: (i, 0)),
            pl.BlockSpec((width, d), lambda i: (0, 0)),
            vec(), vec(), vec(),
            pl.BlockSpec((d, d), lambda i: (0, 0)),
            vec(),
        ],
        out_specs=pl.BlockSpec((tm, d), lambda i: (i, 0)),
        scratch_shapes=[pltpu.VMEM((tm + halo, d), F32), pltpu.VMEM((tm, d), F32),
                        pltpu.VMEM((SUBLANES, slab_rows, col_chunk), F32)],
        compiler_params=_params("parallel"),
        name="conformer_dwconv_out",
    )(u, u, x, w_dw, b_dw.reshape(1, d), ln_g.reshape(1, d), ln_b.reshape(1, d), w_pw2.astype(BF16),
      b_pw2.reshape(1, d))


def _in_proj_kernel(x_ref, g_ref, wn_ref, wt_ref, perm_ref, on_ref, oc_ref, oq_ref, ov_ref, *, n_qg, row_chunk):
    hn = _rms_rows(x_ref[...], g_ref[...]).astype(BF16)
    nk = on_ref.shape[1]
    on_ref[...] = _dot(hn, wn_ref[:, 0:nk])
    kcvc = _dot(hn, wn_ref[:, nk:]).astype(BF16)
    regrouped = _dot(perm_ref[...], kcvc).astype(BF16)
    per = regrouped.shape[0] // CMP_STRIDE
    for l in range(CMP_STRIDE):
        oc_ref[0, l] = regrouped[l * per:(l + 1) * per, :]
    for r0 in range(0, n_qg, row_chunk):
        r1 = min(r0 + row_chunk, n_qg)
        oq_ref[r0:r1, :] = _dot_nt(wt_ref[r0:r1, :], hn)
    vt = _dot_nt(wt_ref[n_qg:, :], hn).astype(BF16)
    ones = jnp.ones((BF16_ROWS, V_TILE), BF16)
    for jj in range(ov_ref.shape[1]):
        for g in range(N_KV):
            for half in range(2):
                r0 = g * LANES + half * HEAD_DIM
                ov_ref[0, jj, g, half * V_ROWS:half * V_ROWS + HEAD_DIM, :] = (
                    vt[r0:r0 + HEAD_DIM, jj * V_TILE:(jj + 1) * V_TILE])
                ov_ref[0, jj, g, half * V_ROWS + HEAD_DIM:(half + 1) * V_ROWS, :] = ones


def _in_proj(x, g, wn, wt, batch, seq):
    t, d = x.shape
    tm = min(512, seq)
    n_qg = N_HEADS * HEAD_DIM + N_KV * GATE_ROWS
    assert seq % tm == 0 and tm % V_TILE == 0 and wt.shape[0] == n_qg + N_KV * LANES
    per_seq = seq // tm
    nk = N_KV * LANES
    ncv = wn.shape[1] - nk
    per = tm // CMP_STRIDE
    rows = np.arange(tm)
    perm = np.zeros((tm, tm), np.float32)
    perm[(rows % CMP_STRIDE) * per + rows // CMP_STRIDE, rows] = 1.0
    kern = functools.partial(_in_proj_kernel, n_qg=n_qg, row_chunk=512)
    return pl.pallas_call(
        kern,
        out_shape=(jax.ShapeDtypeStruct((t, nk), F32),
                   jax.ShapeDtypeStruct((t // tm, CMP_STRIDE, per, ncv), BF16),
                   jax.ShapeDtypeStruct((n_qg, t), F32),
                   jax.ShapeDtypeStruct((batch, seq // V_TILE, N_KV, 2 * V_ROWS, V_TILE), BF16)),
        grid=(t // tm,),
        in_specs=[
            pl.BlockSpec((tm, d), lambda i: (i, 0)),
            pl.BlockSpec((1, d), lambda i: (0, 0)),
            pl.BlockSpec(wn.shape, lambda i: (0, 0)),
            pl.BlockSpec(wt.shape, lambda i: (0, 0)),
            pl.BlockSpec((tm, tm), lambda i: (0, 0)),
        ],
        out_specs=(pl.BlockSpec((tm, nk), lambda i: (i, 0)),
                   pl.BlockSpec((1, CMP_STRIDE, per, ncv), lambda i: (i, 0, 0, 0)),
                   pl.BlockSpec((n_qg, tm), lambda i: (0, i)),
                   pl.BlockSpec((1, tm // V_TILE, N_KV, 2 * V_ROWS, V_TILE),
                                lambda i: (i // per_seq, i % per_seq, 0, 0, 0))),
        compiler_params=_params("parallel"),
        name="nsa_in_proj",
    )(x, g.reshape(1, d), wn, wt, jnp.asarray(perm, BF16))


def _out_proj_kernel(a_ref, x_ref, w_ref, o_ref):
    o_ref[...] = x_ref[...] + _dot(a_ref[...], w_ref[...])


def _out_proj(a, x, w):
    t, d = x.shape
    tm = min(512, t)
    return pl.pallas_call(
        _out_proj_kernel,
        out_shape=jax.ShapeDtypeStruct((t, d), F32),
        grid=(t // tm,),
        in_specs=[
            pl.BlockSpec((tm, a.shape[1]), lambda i: (i, 0)),
            pl.BlockSpec((tm, d), lambda i: (i, 0)),
            pl.BlockSpec(w.shape, lambda i: (0, 0)),
        ],
        out_specs=pl.BlockSpec((tm, d), lambda i: (i, 0)),
        compiler_params=_params("parallel"),
        name="nsa_out_proj",
    )(a, x, w.astype(BF16))


def _kprep_kernel(k_ref, gain_ref, bd_ref, c_ref, s1_ref, s2_ref, kaug_ref, *, ts, seq):
    i = pl.program_id(0)
    bd = bd_ref[...]
    gain = gain_ref[...]
    c, s1, s2 = c_ref[...], s1_ref[...], s2_ref[...]
    tpos = (i * ts) % seq + lax.broadcasted_iota(jnp.int32, (ts, LANES), 0)
    lane = lax.broadcasted_iota(jnp.int32, (ts, LANES), 1)
    onehot = jnp.where(tpos // SEL_LEN == lane, 1.0, 0.0).astype(BF16)
    for g in range(N_KV):
        xb = k_ref[:, g * LANES:(g + 1) * LANES]
        xr = _rope(_head_rms(xb, bd, gain), c, s1, s2)
        kaug_ref[:, 2 * g * LANES:(2 * g + 1) * LANES] = xr.astype(BF16)
        kaug_ref[:, (2 * g + 1) * LANES:(2 * g + 2) * LANES] = onehot


def _kprep(proj, gain, bd, tabs, seq):
    t = proj.shape[0]
    ts = min(512, seq)
    assert seq % ts == 0
    nk = N_KV * LANES
    kern = functools.partial(_kprep_kernel, ts=ts, seq=seq)
    tab = lambda: pl.BlockSpec((ts, LANES), lambda i: (i % (seq // ts), 0))
    return pl.pallas_call(
        kern,
        out_shape=jax.ShapeDtypeStruct((t, 2 * nk), BF16),
        grid=(t // ts,),
        in_specs=[
            pl.BlockSpec((ts, nk), lambda i: (i, 0)),
            pl.BlockSpec((1, LANES), lambda i: (0, 0)),
            pl.BlockSpec((LANES, LANES), lambda i: (0, 0)),
            tab(), tab(), tab(),
        ],
        out_specs=pl.BlockSpec((ts, 2 * nk), lambda i: (i, 0)),
        compiler_params=_params("parallel"),
        name="nsa_key_prep",
    )(proj, gain, bd, *tabs)


def _compress_kernel(x_ref, w1k_ref, w1v_ref, ck_ref, cv_ref, w2k_ref, w2v_ref, gain_ref, o_ref, ot_ref,
                     acck_ref, accv_ref):
    l = pl.program_id(1)
    nc = acck_ref.shape[0]
    hid = w2k_ref.shape[0]
    hk = w1k_ref.shape[1]

    @pl.when(l == 0)
    def _():
        acck_ref[...] = jnp.zeros_like(acck_ref)
        accv_ref[...] = jnp.zeros_like(accv_ref)

    x = x_ref[0].reshape(nc, 2 * hk)
    acck_ref[...] += _dot(x[:, 0:hk], w1k_ref[0])
    accv_ref[...] += _dot(x[:, hk:], w1v_ref[0])

    @pl.when(l == pl.num_programs(1) - 1)
    def _():
        for g in range(N_KV):
            def hidden(acc_ref, c_ref):
                first = acc_ref[:, g * hid:(g + 1) * hid] + c_ref[0:1, :]
                second = acc_ref[:, (N_KV + g) * hid:(N_KV + g + 1) * hid] + c_ref[1:2, :]
                pre = first + pltpu.roll(second, nc - 1, 0)
                return (pre * _sigmoid(pre)).astype(BF16)

            kv = _dot(hidden(acck_ref, ck_ref), w2k_ref[...]) + _dot(hidden(accv_ref, cv_ref), w2v_ref[...])
            is_k = lax.broadcasted_iota(jnp.int32, kv.shape, 1) < HEAD_DIM
            ss = jnp.sum(jnp.where(is_k, kv * kv, 0.0), axis=-1, keepdims=True)
            kn = kv * lax.rsqrt(ss * (1.0 / HEAD_DIM) + EPS) * gain_ref[...]
            out = jnp.where(is_k, kn, kv)
            o_ref[g] = out.astype(BF16)
            ot_ref[g] = out.T.astype(BF16)


def _compress(kcvc, pe_k, pe_v, ck_w1, ck_w2, cv_w1, cv_w2, kc_g, batch, seq):
    nc = seq // CMP_STRIDE
    hid = ck_w1.shape[1]
    hk = N_KV * HEAD_DIM
    tiles, _, per, width = kcvc.shape
    assert CMP_LEN == 2 * CMP_STRIDE and width == 2 * hk and tiles * per == batch * nc
    x5 = kcvc.reshape(batch, tiles // batch, CMP_STRIDE, per, width)
    pad = lambda w, left: jnp.pad(w, ((0, 0), (HEAD_DIM, 0) if left else (0, HEAD_DIM))).astype(BF16)
    gain = jnp.concatenate([kc_g, jnp.ones((HEAD_DIM,), F32)]).reshape(1, LANES)

    def first_layer(pe, w1):
        w = w1.reshape(2, CMP_STRIDE, HEAD_DIM, hid)
        rows = lambda a, k: jnp.pad(a, ((0, 0), (k * HEAD_DIM, (N_KV - 1 - k) * HEAD_DIM), (0, 0)))
        bd = jnp.concatenate([rows(w[s], k) for s in range(2) for k in range(N_KV)], axis=2)
        const = jnp.einsum("sld,sldh->sh", pe.reshape(2, CMP_STRIDE, HEAD_DIM), w, precision=lax.Precision.HIGHEST)
        return bd.astype(BF16), const

    w1k, const_k = first_layer(pe_k, ck_w1)
    w1v, const_v = first_layer(pe_v, cv_w1)
    wide = 2 * N_KV * hid
    step_w = lambda: pl.BlockSpec((1, hk, wide), lambda b, l: (l, 0, 0))
    full = lambda shape: pl.BlockSpec(shape, lambda b, l: (0,) * len(shape))
    return pl.pallas_call(
        _compress_kernel,
        out_shape=(jax.ShapeDtypeStruct((batch * N_KV, nc, LANES), BF16),
                   jax.ShapeDtypeStruct((batch * N_KV, LANES, nc), BF16)),
        grid=(batch, CMP_STRIDE),
        in_specs=[pl.BlockSpec((1, tiles // batch, None, per, width), lambda b, l: (b, 0, l, 0, 0)),
                  step_w(), step_w(), full((2, hid)), full((2, hid)),
                  full((hid, LANES)), full((hid, LANES)), full((1, LANES))],
        out_specs=(pl.BlockSpec((N_KV, nc, LANES), lambda b, l: (b, 0, 0)),
                   pl.BlockSpec((N_KV, LANES, nc), lambda b, l: (b, 0, 0))),
        scratch_shapes=[pltpu.VMEM((nc, wide), F32), pltpu.VMEM((nc, wide), F32)],
        compiler_params=_params("parallel", "arbitrary"),
        name="nsa_compress",
    )(x5, w1k, w1v, const_k, const_v, pad(ck_w2, False), pad(cv_w2, True), gain)


def _attn_kernel(qt_ref, gate_ref, kaug_ref, vt_ref, kvc_ref, kvct_ref, cmapt_ref, qg_ref, cos_ref, sin_ref,
                 o_ref,
                 qaug_ref, qwin_ref, oc_ref, ms_ref, ls_ref, accs_ref, mw_ref, lw_ref, accw_ref,
                 sa_ref, sb_ref, wdiag_ref, wold_ref, imp_ref,
                 *, tq, tk, k_top, n_cmp_var):
    qi = pl.program_id(2)
    t0 = qi * tq
    cols = GROUP * tq
    half = ROT_DIM // 2

    gain = qg_ref[...]
    cos, sin = cos_ref[...], sin_ref[...]
    nope, rope = [], []
    for r in range(GROUP):
        x = qt_ref[r * HEAD_DIM:(r + 1) * HEAD_DIM, :]
        ss = jnp.sum(x * x, axis=0, keepdims=True)
        xn = x * lax.rsqrt(ss * (1.0 / HEAD_DIM) + EPS) * gain
        x1, x2 = xn[0:half], xn[half:ROT_DIM]
        nope.append(xn)
        rope.append(jnp.concatenate([x1 * cos - x2 * sin, x2 * cos + x1 * sin, xn[ROT_DIM:]], axis=0))
    zeros = jnp.zeros((HEAD_DIM, cols), BF16)
    q_nope = jnp.concatenate([jnp.concatenate(nope, axis=1).astype(BF16), zeros], axis=0)
    q_rope = jnp.concatenate(rope, axis=1).astype(BF16)
    qwin_ref[...] = jnp.concatenate([zeros, q_rope], axis=0)

    tpos = t0 + lax.broadcasted_iota(jnp.int32, (1, tq), 1)

    def per_head(x):
        return jnp.concatenate([x] * GROUP, axis=1)

    def fold_tile(s, keep, k0, v_rows, m_ref, l_ref, acc_ref):
        if keep is not None:
            s = jnp.where(per_head(keep), s, NEG)
        m_prev = m_ref[...]
        m_new = jnp.maximum(m_prev, jnp.max(s, axis=0, keepdims=True))
        p = jnp.exp2(s - m_new)
        alpha = jnp.exp2(m_prev - m_new)
        pb = p.astype(BF16)
        v_tile0 = k0 // V_TILE
        pv = _dot(vt_ref[0, v_tile0, 0, v_rows, :], pb[0:V_TILE])
        for c in range(1, s.shape[0] // V_TILE):
            pv = pv + _dot(vt_ref[0, v_tile0 + c, 0, v_rows, :], pb[c * V_TILE:(c + 1) * V_TILE])
        l_ref[...] = alpha * l_ref[...] + pv[HEAD_DIM:HEAD_DIM + 1]
        acc_ref[...] = alpha * acc_ref[...] + pv[0:HEAD_DIM]
        m_ref[...] = m_new

    def back(k0, size):
        return tpos - (k0 + lax.broadcasted_iota(jnp.int32, (size, 1), 0))

    def sel_scores(k0):
        return _dot(kaug_ref[pl.ds(k0, tk), :], qaug_ref[...])

    def win_scores(k0, size):
        return _dot(kaug_ref[pl.ds(k0, size), 0:LANES], qwin_ref[...])

    for m_ref, l_ref, acc_ref in ((ms_ref, ls_ref, accs_ref), (mw_ref, lw_ref, accw_ref)):
        m_ref[...] = jnp.full_like(m_ref, NEG)
        l_ref[...] = jnp.zeros_like(l_ref)
        acc_ref[...] = jnp.zeros_like(acc_ref)

    t0a = pl.multiple_of(t0, tq)
    k_old = pl.multiple_of(jnp.maximum(t0 - WINDOW, 0), tq)
    wdiag_ref[...] = win_scores(t0a, tq)
    wold_ref[...] = win_scores(k_old, WINDOW)
    qaug_ref[0:LANES, :] = jnp.concatenate([q_rope, zeros], axis=0)
    sa_ref[...] = _dot(kaug_ref[0:tk, 0:LANES], qaug_ref[0:LANES, :])
    win_state = (slice(V_ROWS, 2 * V_ROWS), mw_ref, lw_ref, accw_ref)
    fold_tile(wdiag_ref[...], back(t0a, tq) >= 0, t0a, *win_state)
    b_old = back(k_old, WINDOW)
    fold_tile(wold_ref[...], (b_old < WINDOW) & (b_old > tpos - t0), k_old, *win_state)

    nc = kvc_ref.shape[1]

    def cmp_branch(rows):
        kvc = kvc_ref[0, 0:rows, :]
        cmp_end = lax.broadcasted_iota(jnp.int32, (rows, 1), 0) * CMP_STRIDE + (CMP_LEN - 1)
        s_c = jnp.where(per_head(cmp_end <= tpos), _dot(kvc, q_nope), NEG)
        e = jnp.exp2(s_c - jnp.maximum(jnp.max(s_c, axis=0, keepdims=True), 0.1 * NEG))
        p_c = e * (1.0 / jnp.maximum(jnp.sum(e, axis=0, keepdims=True), 1e-30))
        oc_ref[...] = _dot(kvct_ref[0, HEAD_DIM:, 0:rows], p_c.astype(BF16))
        p_sum = p_c[:, 0:tq]
        for r in range(1, GROUP):
            p_sum = p_sum + p_c[:, r * tq:(r + 1) * tq]
        hi, lo = _split_bf16(p_sum)
        imp_ref[...] = _dot(cmapt_ref[:, 0:rows], hi) + _dot(cmapt_ref[:, 0:rows], lo)

    variant = (t0 + tq - 1) // (CMP_STRIDE * (nc // n_cmp_var))
    for v in range(n_cmp_var):
        pl.when(variant == v)(functools.partial(cmp_branch, (v + 1) * (nc // n_cmp_var)))
    imp = imp_ref[...]

    blk = lax.broadcasted_iota(jnp.int32, (LANES, tq), 0)
    cur = tpos // SEL_LEN
    forced = (blk == 0) | (blk == cur) | (blk == cur - 1)
    taken = -3e38
    imp = jnp.where(blk > cur, -1.0, jnp.where(forced, taken, imp))
    blk_f = blk.astype(F32)
    for _ in range(max(k_top - 3, 0)):
        mx = jnp.max(imp, axis=0, keepdims=True)
        idx = jnp.min(jnp.where(imp == mx, blk_f, float(LANES)), axis=0, keepdims=True)
        imp = jnp.where(blk_f == idx, taken, imp)
    bias = jnp.where(imp == taken, 0.0, NEG)
    qaug_ref[LANES:, :] = per_head(bias.astype(BF16))
    blocks0 = tk // SEL_LEN
    bias0 = jnp.broadcast_to(bias[0:blocks0, None, :], (blocks0, SEL_LEN, tq)).reshape(tk, tq)
    sa_ref[...] = sa_ref[...] + per_head(bias0)


    n = t0 // tk
    sel_state = (slice(0, V_ROWS), ms_ref, ls_ref, accs_ref)

    def tile_pair(k0):
        sb_ref[...] = sel_scores(k0 + tk)
        fold_tile(sa_ref[...], None, k0, *sel_state)
        sa_ref[...] = sel_scores(k0 + 2 * tk)
        fold_tile(sb_ref[...], None, k0 + tk, *sel_state)

    def quad_body(i, carry):
        k0 = pl.multiple_of(4 * i * tk, tk)
        tile_pair(k0)
        tile_pair(k0 + 2 * tk)
        return carry

    n_pairs = n // 2
    lax.fori_loop(0, n_pairs // 2, quad_body, 0)

    @pl.when(n_pairs % 2 == 1)
    def _():
        tile_pair(pl.multiple_of((n_pairs - 1) * 2 * tk, tk))

    k_diag = pl.multiple_of(n * tk, tk)

    def diagonal_fold(buf_ref):
        if tk == tq:
            fold_tile(buf_ref[...], back(k_diag, tk) >= 0, k_diag, *sel_state)
            return
        pl.when(t0 % tk == 0)(lambda: fold_tile(buf_ref[0:tq, :], back(k_diag, tq) >= 0, k_diag, *sel_state))
        pl.when(t0 % tk != 0)(lambda: fold_tile(buf_ref[...], back(k_diag, tk) >= 0, k_diag, *sel_state))

    @pl.when(n % 2 == 1)
    def _():
        sb_ref[...] = sel_scores(k_diag)
        fold_tile(sa_ref[...], None, k_diag - tk, *sel_state)
        diagonal_fold(sb_ref)

    @pl.when(n % 2 == 0)
    def _():
        diagonal_fold(sa_ref)

    o_s = accs_ref[...] / ls_ref[...]
    o_w = accw_ref[...] / lw_ref[...]
    o_c = oc_ref[...]
    sig = _sigmoid(gate_ref[...])
    heads = []
    for r in range(GROUP):
        cs = slice(r * tq, (r + 1) * tq)
        g_c, g_s, g_w = (sig[3 * r + b:3 * r + b + 1, :] for b in range(3))
        heads.append(g_c * o_c[:, cs] + g_s * o_s[:, cs] + g_w * o_w[:, cs])
    o_ref[...] = jnp.concatenate(heads, axis=0).T.astype(o_ref.dtype)


def _attention(qgt, kaug, vt, kvc, kvct, cmapt, qg, cos, sin, batch, seq):
    t = qgt.shape[1]
    tq = min(Q_TILE, seq)
    tk = min(512, seq)
    nq = seq // tq
    nc = kvc.shape[1]
    assert seq % tk == 0 and tk % tq == 0 and WINDOW % tq == 0 and tq % V_TILE == 0 and seq >= WINDOW
    cols = GROUP * tq
    hq = GROUP * HEAD_DIM
    k_top = min(SEL_TOPK, seq // SEL_LEN)
    n_cmp_var = max(1, min(4, nc // LANES))
    assert k_top >= 3 and nc % n_cmp_var == 0
    kern = functools.partial(_attn_kernel, tq=tq, tk=tk, k_top=k_top, n_cmp_var=n_cmp_var)
    tab = lambda: pl.BlockSpec((ROT_DIM // 2, tq), lambda b, g, qi: (0, qi))
    stat = lambda: pltpu.VMEM((1, cols), F32)
    acc = lambda: pltpu.VMEM((HEAD_DIM, cols), F32)
    return pl.pallas_call(
        kern,
        out_shape=jax.ShapeDtypeStruct((t, N_HEADS * HEAD_DIM), BF16),
        grid=(batch, N_KV, nq),
        in_specs=[
            pl.BlockSpec((hq, tq), lambda b, g, qi: (g, b * nq + qi)),
            pl.BlockSpec((GATE_ROWS, tq), lambda b, g, qi: (N_HEADS * HEAD_DIM // GATE_ROWS + g, b * nq + qi)),
            pl.BlockSpec((seq, 2 * LANES), lambda b, g, qi: (b, g)),
            pl.BlockSpec((1, seq // V_TILE, 1, 2 * V_ROWS, V_TILE), lambda b, g, qi: (b, 0, g, 0, 0)),
            pl.BlockSpec((1, nc, LANES), lambda b, g, qi: (b * N_KV + g, 0, 0)),
            pl.BlockSpec((1, LANES, nc), lambda b, g, qi: (b * N_KV + g, 0, 0)),
            pl.BlockSpec((LANES, nc), lambda b, g, qi: (0, 0)),
            pl.BlockSpec((HEAD_DIM, tq), lambda b, g, qi: (0, 0)),
            tab(), tab(),
        ],
        out_specs=pl.BlockSpec((tq, hq), lambda b, g, qi: (b * nq + qi, g)),
        scratch_shapes=[
            pltpu.VMEM((2 * LANES, cols), BF16), pltpu.VMEM((LANES, cols), BF16), acc(),
            stat(), stat(), acc(), stat(), stat(), acc(),
            pltpu.VMEM((tk, cols), F32), pltpu.VMEM((tk, cols), F32), pltpu.VMEM((tq, cols), F32), pltpu.VMEM((WINDOW, cols), F32),
            pltpu.VMEM((LANES, tq), F32),
        ],
        compiler_params=_params("parallel", "parallel", "arbitrary"),
        name="nsa_attention",
    )(qgt, qgt, kaug, vt, kvc, kvct, cmapt, qg, cos, sin)


def _rope_angles(seq):
    half = ROT_DIM // 2
    inv_freq = ROPE_THETA ** (-jnp.arange(half, dtype=F32) * (2.0 / ROT_DIM))
    ang = jnp.arange(seq, dtype=F32)[:, None] * inv_freq[None, :]
    return jnp.cos(ang), jnp.sin(ang)


def _rope_tables(cos, sin):
    seq, half = cos.shape
    zeros = jnp.zeros((seq, HEAD_DIM - ROT_DIM), F32)
    zh = jnp.zeros((seq, half), F32)
    c = jnp.concatenate([cos, cos, zeros + 1.0], axis=1)
    s1 = jnp.concatenate([-sin, zh, zeros], axis=1)
    s2 = jnp.concatenate([zh, sin, zeros], axis=1)
    reps = LANES // HEAD_DIM
    return tuple(jnp.tile(a, (1, reps)) for a in (c, s1, s2))


def _cmp_to_sel_t(nc, n_sel):
    start_c = np.arange(nc)[None, :] * CMP_STRIDE
    start_s = np.arange(LANES)[:, None] * SEL_LEN
    ov = np.minimum(start_c + CMP_LEN, start_s + SEL_LEN) - np.maximum(start_c, start_s)
    m = np.maximum(ov, 0).astype(np.float32) / CMP_LEN
    m[n_sel:, :] = 0.0
    m[:, nc - 1:] = 0.0
    return jnp.asarray(m, BF16)


def _nsa_in_weights(w_in):
    d = w_in.shape[0]
    hq, hk = N_HEADS * HEAD_DIM, N_KV * HEAD_DIM
    offs = np.cumsum([0, hq] + [hk] * 6)
    q, kc, vc, ks, vs, kw, vw = (w_in[:, offs[n]:offs[n + 1]] for n in range(7))
    gl = w_in[:, offs[7]:]
    per_group = lambda a, b: jnp.stack([a.reshape(d, N_KV, HEAD_DIM), b.reshape(d, N_KV, HEAD_DIM)],
                                       axis=2).reshape(d, N_KV * LANES)
    gates = jnp.pad(gl.reshape(d, N_KV, 3 * GROUP), ((0, 0), (0, 0), (0, GATE_ROWS - 3 * GROUP)))
    wn = jnp.concatenate([per_group(ks, kw), kc, vc], axis=1).astype(BF16)
    wt = jnp.concatenate([q, gates.reshape(d, N_KV * GATE_ROWS), per_group(vs, vw)], axis=1).T.astype(BF16)
    return wn, wt


def _nsa_layer(x, g, w_in, q_g, kc_g, ks_g, kw_g, pe_k, pe_v, ck_w1, ck_w2, cv_w1, cv_w2, w_out, batch, seq):
    nc = seq // CMP_STRIDE
    n_sel = seq // SEL_LEN
    assert n_sel <= LANES and seq % CMP_STRIDE == 0 and 3 * GROUP <= GATE_ROWS
    wn, wt = _nsa_in_weights(w_in)
    proj, kcvc, qgt, vt = _in_proj(x, g, wn, wt, batch, seq)
    cos, sin = _rope_angles(seq)
    seg = np.arange(LANES) // HEAD_DIM
    bd = jnp.asarray(seg[:, None] == seg[None, :], BF16)
    kaug = _kprep(proj, jnp.concatenate([ks_g, kw_g]).reshape(1, LANES), bd, _rope_tables(cos, sin), seq)
    kvc, kvct = _compress(kcvc, pe_k, pe_v, ck_w1, ck_w2, cv_w1, cv_w2, kc_g, batch, seq)
    tq = min(Q_TILE, seq)
    qg = jnp.broadcast_to((q_g * float(HEAD_DIM ** -0.5 * np.log2(np.e)))[:, None], (HEAD_DIM, tq))
    attn = _attention(qgt, kaug, vt, kvc, kvct, _cmp_to_sel_t(nc, n_sel), qg, cos.T, sin.T, batch, seq)
    return _out_proj(attn, x, w_out)


def kernel(x, mix_norm_g, ffn_norm_g, conv_w_pw1, conv_b_pw1, conv_w_dw, conv_b_dw, conv_ln_g, conv_ln_b, conv_w_pw2, conv_b_pw2, nsa_w_in, nsa_q_norm, nsa_kc_norm, nsa_ks_norm, nsa_kw_norm, nsa_pe_k, nsa_pe_v, nsa_ck_w1, nsa_ck_w2, nsa_cv_w1, nsa_cv_w2, nsa_w_out, ffn_w_up, ffn_w_dw, ffn_b_dw, ffn_w_down):
    batch, seq, d = x.shape
    depth = mix_norm_g.shape[0]
    n_mixers = 2
    h = x.reshape(batch * seq, d)
    w_up_all, w_down_all = ffn_w_up.astype(BF16), ffn_w_down.astype(BF16)
    for i in range(depth):
        j = i // n_mixers
        if i % n_mixers == 0:
            u = _conformer_glu(h, mix_norm_g[i], conv_w_pw1[j], conv_b_pw1[j])
            h = _conformer_conv_out(u, h, conv_w_dw[j], conv_b_dw[j], conv_ln_g[j], conv_ln_b[j],
                                    conv_w_pw2[j], conv_b_pw2[j], seq)
        else:
            h = _nsa_layer(h, mix_norm_g[i], nsa_w_in[j], nsa_q_norm[j], nsa_kc_norm[j], nsa_ks_norm[j],
                           nsa_kw_norm[j], nsa_pe_k[j], nsa_pe_v[j], nsa_ck_w1[j], nsa_ck_w2[j], nsa_cv_w1[j],
                           nsa_cv_w2[j], nsa_w_out[j], batch, seq)
        h = _conv_ffn(h, ffn_norm_g[i], w_up_all, ffn_w_dw[i], ffn_b_dw[i], w_down_all, i, seq)
    return h.reshape(batch, seq, d)
```
